```python
import jax, jax.numpy as jnp
from jax import lax
import numpy as np

D_MODEL = 1024
BATCH = 8
SEQ = 4096
DEPTH = 1
DEC_BATCH = 128
DEC_SEQ = 8
PAST_LEN = 8192
PAGE_SIZE = 128

HEAD_DIM = 64
N_HEADS = D_MODEL // (2 * HEAD_DIM)
N_KV_HEADS = N_HEADS // 2
KV_GROUP = N_HEADS // N_KV_HEADS
ATTN_WIDTH = N_HEADS * HEAD_DIM
KV_WIDTH = N_KV_HEADS * HEAD_DIM
N_IDX_HEADS = 8
IDX_DIM = 64
TOPK_MAX = 256
Q_BLOCK = 128
ROPE_THETA = 10000.0
POOL_WINDOWS = (2, 4, 8, 16)
N_POOL_GROUPS = len(POOL_WINDOWS)
POOL_WIDTH = D_MODEL // 2
POOL_GROUP = POOL_WIDTH // N_POOL_GROUPS
POOL_HIST = max(POOL_WINDOWS) - 1
N_GROUPS = 4
EXPERTS_PER_GROUP = 8
N_EXPERTS = N_GROUPS * EXPERTS_PER_GROUP
TOP_K = 2
D_EXPERT = D_MODEL // 2
MOE_BLOCK = 128
RMS_EPS = 1e-6
SEGMENTS = (ATTN_WIDTH, KV_WIDTH, KV_WIDTH, N_IDX_HEADS * IDX_DIM, IDX_DIM, N_IDX_HEADS,
            POOL_WIDTH, D_MODEL, D_MODEL)
PROJ_WIDTH = sum(SEGMENTS)

kernel_name = 'dsa_pool_gated_hmoe_step'


def rms_norm(x, g):
    xf = x.astype(jnp.float32)
    y = xf * lax.rsqrt(jnp.mean(xf * xf, axis=-1, keepdims=True) + RMS_EPS)
    return (y * g.astype(jnp.float32)).astype(x.dtype)


def rope(x, pos):
    half = x.shape[-1] // 2
    inv_freq = 1.0 / (ROPE_THETA ** (jnp.arange(half, dtype=jnp.float32) / half))
    ang = pos.astype(jnp.float32)[:, None] * inv_freq[None, :]
    cos = jnp.cos(ang)[None, :, None, :]
    sin = jnp.sin(ang)[None, :, None, :]
    xf = x.astype(jnp.float32)
    x1, x2 = xf[..., :half], xf[..., half:]
    return jnp.concatenate([x1 * cos - x2 * sin, x1 * sin + x2 * cos], axis=-1).astype(x.dtype)


def project_mixer_inputs(xn, w_in, pos):
    B, T, _ = xn.shape
    cuts = [int(c) for c in np.cumsum(SEGMENTS)[:-1]]
    q, k, v, qi, ki, wi, u, ga, gb = jnp.split(xn @ w_in, cuts, axis=-1)
    q = rope(q.reshape(B, T, N_HEADS, HEAD_DIM), pos)
    k = rope(k.reshape(B, T, N_KV_HEADS, HEAD_DIM), pos)
    v = v.reshape(B, T, N_KV_HEADS, HEAD_DIM)
    qi = rope(qi.reshape(B, T, N_IDX_HEADS, IDX_DIM), pos)
    ki = rope(ki[:, :, None, :], pos)[:, :, 0, :]
    return q, k, v, qi, ki, wi, u, ga, gb


def indexer_select(qi, wi, ki_all, q_pos, n_sel):
    L = ki_all.shape[1]
    s = jnp.einsum('bqhd,bld->bqhl', qi, ki_all).astype(jnp.float32) * (IDX_DIM ** -0.5)
    w = wi.astype(jnp.float32) * (N_IDX_HEADS ** -0.5)
    score = jnp.einsum('bqhl,bqh->bql', jax.nn.relu(s), w)
    admissible = jnp.arange(L)[None, None, :] <= q_pos[None, :, None]
    score = jnp.where(admissible, score, -jnp.inf)
    _, idx = lax.top_k(score, n_sel)
    valid = idx <= q_pos[None, :, None]
    return idx, valid


def attend_selected(q, k_s, v_s, valid):
    B, Q = q.shape[:2]
    qg = q.reshape(B, Q, N_KV_HEADS, KV_GROUP, HEAD_DIM)
    logits = jnp.einsum('bqhgd,bqnhd->bqhgn', qg, k_s).astype(jnp.float32) * (HEAD_DIM ** -0.5)
    logits = jnp.where(valid[:, :, None, None, :], logits, -jnp.inf)
    p = jax.nn.softmax(logits, axis=-1)
    o = jnp.einsum('bqhgn,bqnhd->bqhgd', p.astype(v_s.dtype), v_s)
    return o.reshape(B, Q, ATTN_WIDTH)


def prompt_sparse_attention(q, k, v, qi, ki, wi):
    B, S = q.shape[:2]
    n_sel = min(TOPK_MAX, S // 4)
    nb = S // Q_BLOCK
    bidx = jnp.arange(B)[:, None, None]

    def to_blocks(a):
        return jnp.moveaxis(a.reshape((B, nb, Q_BLOCK) + a.shape[2:]), 1, 0)

    def one_block(args):
        qb, qib, wib, posb = args
        idx, valid = indexer_select(qib, wib, ki, posb, n_sel)
        return attend_selected(qb, k[bidx, idx], v[bidx, idx], valid)

    pos = jnp.arange(S).reshape(nb, Q_BLOCK)
    o = lax.map(one_block, (to_blocks(q), to_blocks(qi), to_blocks(wi), pos))
    return jnp.moveaxis(o, 0, 1).reshape(B, S, ATTN_WIDTH)


def sample_sparse_attention(q, k_new, v_new, qi, ki_new, wi, l, cache_k, cache_v, cache_idx_k, page_table):
    DB, DS = q.shape[:2]
    ps = cache_k.shape[2]
    past = page_table.shape[1] * ps
    n_sel = min(TOPK_MAX, (past + DS) // 4)
    ki_past = cache_idx_k[l, page_table].reshape(DB, past, IDX_DIM)
    ki_all = jnp.concatenate([ki_past.astype(ki_new.dtype), ki_new], axis=1)
    q_pos = past + jnp.arange(DS)
    idx, valid = indexer_select(qi, wi, ki_all, q_pos, n_sel)
    bidx = jnp.arange(DB)[:, None, None]
    in_past = (idx < past)[..., None, None]
    ip = jnp.minimum(idx, past - 1)
    phys = page_table[bidx, ip // ps]
    off = ip % ps
    jn = jnp.clip(idx - past, 0, DS - 1)
    k_s = jnp.where(in_past, cache_k[l, phys, off].astype(k_new.dtype), k_new[bidx, jn])
    v_s = jnp.where(in_past, cache_v[l, phys, off].astype(v_new.dtype), v_new[bidx, jn])
    return attend_selected(q, k_s, v_s, valid)


def multiscale_pool_diff(u_all, n_hist):
    B, Ltot, C = u_all.shape
    uf = u_all.astype(jnp.float32)
    csum = jnp.concatenate([jnp.zeros((B, 1, C), jnp.float32), jnp.cumsum(uf, axis=1)], axis=1)
    i = jnp.arange(n_hist, Ltot)
    hi = csum[:, i + 1]
    cur = uf[:, n_hist:]
    outs = []
    for g, w in enumerate(POOL_WINDOWS):
        lo = jnp.maximum(i + 1 - w, 0)
        cnt = (i + 1 - lo).astype(jnp.float32)[None, :, None]
        sl = slice(g * POOL_GROUP, (g + 1) * POOL_GROUP)
        outs.append((hi[..., sl] - csum[:, lo, sl]) / cnt - cur[..., sl])
    return jnp.concatenate(outs, axis=-1).astype(u_all.dtype)


def pool_branch(u_all, n_hist, w_pool, pool_scale):
    d = multiscale_pool_diff(u_all, n_hist)
    B, T, _ = d.shape
    y = jnp.einsum('btgc,gcd->btgd', d.reshape(B, T, N_POOL_GROUPS, POOL_GROUP), w_pool)
    return y.reshape(B, T, POOL_WIDTH) * pool_scale


def merge_branches(x, attn_o, pool_o, ga, gb, w_branch_attn, w_branch_pool, w_out):
    m = jax.nn.sigmoid(ga) * (attn_o @ w_branch_attn) + jax.nn.sigmoid(gb) * (pool_o @ w_branch_pool)
    return x + m @ w_out


def moe_ffn(xn, w_router_group, b_router_group, w_router_expert, b_router_expert,
            w_exp_gate, w_exp_up, w_exp_down):
    shp = xn.shape
    xt = xn.reshape(-1, shp[-1])
    T = xt.shape[0]
    xf = xt.astype(jnp.float32)
    g_logits = xf @ w_router_group.astype(jnp.float32) + b_router_group.astype(jnp.float32)
    g_sel = jnp.argmax(g_logits, axis=-1)
    g_w = jnp.take_along_axis(jax.nn.softmax(g_logits, axis=-1), g_sel[:, None], axis=-1)[:, 0]
    e_logits = (xf @ w_router_expert.astype(jnp.float32) + b_router_expert.astype(jnp.float32))
    e_logits = e_logits.reshape(T, N_GROUPS, EXPERTS_PER_GROUP)
    e_in_group = jnp.take_along_axis(e_logits, g_sel[:, None, None], axis=1)[:, 0]
    top_v, top_i = lax.top_k(e_in_group, TOP_K)
    e_w = jax.nn.softmax(top_v, axis=-1) * g_w[:, None]
    e_id = g_sel[:, None] * EXPERTS_PER_GROUP + top_i
    n_assign = T * TOP_K
    e_flat = e_id.reshape(-1)
    tok_flat = jnp.repeat(jnp.arange(T, dtype=jnp.int32), TOP_K)
    w_flat = e_w.reshape(-1)
    order = jnp.argsort(e_flat)
    e_sorted = e_flat[order]
    counts = jnp.zeros((N_EXPERTS,), jnp.int32).at[e_flat].add(1)
    padded = (counts + MOE_BLOCK - 1) // MOE_BLOCK * MOE_BLOCK
    start = jnp.cumsum(counts) - counts
    pend = jnp.cumsum(padded)
    pstart = pend - padded
    dest = pstart[e_sorted] + (jnp.arange(n_assign) - start[e_sorted])
    n_blocks = (n_assign + N_EXPERTS * (MOE_BLOCK - 1)) // MOE_BLOCK + 1
    buf_tok = jnp.zeros((n_blocks * MOE_BLOCK,), jnp.int32).at[dest].set(tok_flat[order])
    buf_w = jnp.zeros((n_blocks * MOE_BLOCK,), jnp.float32).at[dest].set(w_flat[order])
    block_e = jnp.minimum(jnp.searchsorted(pend, jnp.arange(n_blocks) * MOE_BLOCK, side='right'),
                          N_EXPERTS - 1)

    def run_block(args):
        tok, wt, e = args
        xb = xt[tok]
        hb = jax.nn.silu(xb @ w_exp_gate[e]) * (xb @ w_exp_up[e])
        return (hb @ w_exp_down[e]) * wt[:, None].astype(xb.dtype)

    yb = lax.map(run_block, (buf_tok.reshape(n_blocks, MOE_BLOCK), buf_w.reshape(n_blocks, MOE_BLOCK), block_e))
    out = jnp.zeros_like(xt).at[buf_tok].add(yb.reshape(-1, shp[-1]).astype(xt.dtype))
    return out.reshape(shp)


def prompt_layer(x, lw):
    (attn_norm_g, w_in, w_pool, pool_scale, w_branch_attn, w_branch_pool, w_out, ffn_norm_g,
     w_rg, b_rg, w_re, b_re, w_eg, w_eu, w_ed) = lw
    S = x.shape[1]
    pos = jnp.arange(S)
    xn = rms_norm(x, attn_norm_g)
    q, k, v, qi, ki, wi, u, ga, gb = project_mixer_inputs(xn, w_in, pos)
    attn_o = prompt_sparse_attention(q, k, v, qi, ki, wi)
    pool_o = pool_branch(u, 0, w_pool, pool_scale)
    h = merge_branches(x, attn_o, pool_o, ga, gb, w_branch_attn, w_branch_pool, w_out)
    h = h + moe_ffn(rms_norm(h, ffn_norm_g), w_rg, b_rg, w_re, b_re, w_eg, w_eu, w_ed)
    return h, k, v, ki, u[:, -POOL_HIST:]


def sample_layer(x, l, cache_k, cache_v, cache_idx_k, state_pool, page_table, lw):
    (attn_norm_g, w_in, w_pool, pool_scale, w_branch_attn, w_branch_pool, w_out, ffn_norm_g,
     w_rg, b_rg, w_re, b_re, w_eg, w_eu, w_ed) = lw
    DS = x.shape[1]
    past = page_table.shape[1] * cache_k.shape[2]
    pos = past + jnp.arange(DS)
    xn = rms_norm(x, attn_norm_g)
    q, k, v, qi, ki, wi, u, ga, gb = project_mixer_inputs(xn, w_in, pos)
    attn_o = sample_sparse_attention(q, k, v, qi, ki, wi, l, cache_k, cache_v, cache_idx_k, page_table)
    n_hist = state_pool.shape[2]
    u_all = jnp.concatenate([state_pool[l].astype(u.dtype), u], axis=1)
    pool_o = pool_branch(u_all, n_hist, w_pool, pool_scale)
    h = merge_branches(x, attn_o, pool_o, ga, gb, w_branch_attn, w_branch_pool, w_out)
    h = h + moe_ffn(rms_norm(h, ffn_norm_g), w_rg, b_rg, w_re, b_re, w_eg, w_eu, w_ed)
    return h, k, v, ki, u_all[:, -n_hist:]


def setup_inputs(seed: int = 0) -> dict:
    key = jax.random.key(seed)
    ks = jax.random.split(key, 24)
    f32 = jnp.float32
    n_pages = PAST_LEN // PAGE_SIZE
    n_phys = (DEC_BATCH * n_pages * 5) // 4

    def nrm(k, shape, scale):
        return jax.random.normal(k, shape, f32) * scale

    def gain(k, shape):
        return 1.0 + 0.1 * jax.random.normal(k, shape, f32)

    page_table = jax.random.permutation(ks[6], n_phys)[: DEC_BATCH * n_pages]
    page_table = page_table.reshape(DEC_BATCH, n_pages).astype(jnp.int32)
    return {
        'x_prompt': nrm(ks[0], (BATCH, SEQ, D_MODEL), 1.0),
        'x_sample': nrm(ks[1], (DEC_BATCH, DEC_SEQ, D_MODEL), 1.0),
        'cache_k': nrm(ks[2], (DEPTH, n_phys, PAGE_SIZE, N_KV_HEADS, HEAD_DIM), 1.0),
        'cache_v': nrm(ks[3], (DEPTH, n_phys, PAGE_SIZE, N_KV_HEADS, HEAD_DIM), 1.0),
        'cache_idx_k': nrm(ks[4], (DEPTH, n_phys, PAGE_SIZE, IDX_DIM), 1.0),
        'state_pool': nrm(ks[5], (DEPTH, DEC_BATCH, POOL_HIST, POOL_WIDTH), 1.0),
        'page_table': page_table,
        'attn_norm_g': gain(ks[7], (DEPTH, D_MODEL)),
        'w_in': nrm(ks[8], (DEPTH, D_MODEL, PROJ_WIDTH), D_MODEL ** -0.5),
        'w_pool': nrm(ks[9], (DEPTH, N_POOL_GROUPS, POOL_GROUP, POOL_GROUP), POOL_GROUP ** -0.5),
        'pool_scale': gain(ks[10], (DEPTH, POOL_WIDTH)),
        'w_branch_attn': nrm(ks[11], (DEPTH, ATTN_WIDTH, D_MODEL), ATTN_WIDTH ** -0.5),
        'w_branch_pool': nrm(ks[12], (DEPTH, POOL_WIDTH, D_MODEL), POOL_WIDTH ** -0.5),
        'w_out': nrm(ks[13], (DEPTH, D_MODEL, D_MODEL), D_MODEL ** -0.5),
        'ffn_norm_g': gain(ks[14], (DEPTH, D_MODEL)),
        'w_router_group': nrm(ks[15], (DEPTH, D_MODEL, N_GROUPS), D_MODEL ** -0.5),
        'b_router_group': nrm(ks[16], (DEPTH, N_GROUPS), 0.01),
        'w_router_expert': nrm(ks[17], (DEPTH, D_MODEL, N_EXPERTS), D_MODEL ** -0.5),
        'b_router_expert': nrm(ks[18], (DEPTH, N_EXPERTS), 0.01),
        'w_exp_gate': nrm(ks[19], (DEPTH, N_EXPERTS, D_MODEL, D_EXPERT), D_MODEL ** -0.5),
        'w_exp_up': nrm(ks[20], (DEPTH, N_EXPERTS, D_MODEL, D_EXPERT), D_MODEL ** -0.5),
        'w_exp_down': nrm(ks[21], (DEPTH, N_EXPERTS, D_EXPERT, D_MODEL), D_EXPERT ** -0.5),
        'final_norm_g': gain(ks[22], (D_MODEL,)),
    }


def reference(x_prompt, x_sample, cache_k, cache_v, cache_idx_k, state_pool, page_table,
              attn_norm_g, w_in, w_pool, pool_scale, w_branch_attn, w_branch_pool, w_out,
              ffn_norm_g, w_router_group, b_router_group, w_router_expert, b_router_expert,
              w_exp_gate, w_exp_up, w_exp_down, final_norm_g):
    hp, hs = x_prompt, x_sample
    kp_l, vp_l, kip_l, up_l = [], [], [], []
    ks_l, vs_l, kis_l, us_l = [], [], [], []
    for l in range(DEPTH):
        lw = (attn_norm_g[l], w_in[l], w_pool[l], pool_scale[l], w_branch_attn[l], w_branch_pool[l],
              w_out[l], ffn_norm_g[l], w_router_group[l], b_router_group[l], w_router_expert[l],
              b_router_expert[l], w_exp_gate[l], w_exp_up[l], w_exp_down[l])
        hp, kp, vp, kip, up = prompt_layer(hp, lw)
        hs, ks_, vs, kis, us = sample_layer(hs, l, cache_k, cache_v, cache_idx_k, state_pool, page_table, lw)
        kp_l.append(kp); vp_l.append(vp); kip_l.append(kip); up_l.append(up)
        ks_l.append(ks_); vs_l.append(vs); kis_l.append(kis); us_l.append(us)
    y_prompt = rms_norm(hp, final_norm_g)
    y_sample = rms_norm(hs, final_norm_g)
    new_k_prompt = jnp.stack(kp_l, 0)
    new_v_prompt = jnp.stack(vp_l, 0)
    new_idx_k_prompt = jnp.stack(kip_l, 0)
    new_pool_prompt = jnp.stack(up_l, 0)
    new_k_sample = jnp.stack(ks_l, 0)
    new_v_sample = jnp.stack(vs_l, 0)
    new_idx_k_sample = jnp.stack(kis_l, 0)
    new_pool_sample = jnp.stack(us_l, 0)
    return (y_prompt, y_sample, new_k_prompt, new_v_prompt, new_idx_k_prompt, new_pool_prompt,
            new_k_sample, new_v_sample, new_idx_k_sample, new_pool_sample)
```

```python
import functools

import jax
import jax.numpy as jnp
from jax import lax
from jax.experimental import pallas as pl
from jax.experimental.pallas import tpu as pltpu

D_MODEL = 1024
HEAD_DIM = 64
N_HEADS = 8
N_KV_HEADS = 4
ATTN_WIDTH = N_HEADS * HEAD_DIM
KV_WIDTH = N_KV_HEADS * HEAD_DIM
N_IDX_HEADS = 8
IDX_DIM = 64
TOPK_MAX = 256
ROPE_THETA = 10000.0
POOL_WINDOWS = (2, 4, 8, 16)
POOL_WIDTH = 512
POOL_GROUP = 128
POOL_HIST = 15
N_GROUPS = 4
EXPERTS_PER_GROUP = 8
N_EXPERTS = 32
TOP_K = 2
D_EXPERT = 512
RMS_EPS = 1e-6
SEGMENTS = (ATTN_WIDTH, KV_WIDTH, KV_WIDTH, N_IDX_HEADS * IDX_DIM, IDX_DIM, N_IDX_HEADS,
            POOL_WIDTH, D_MODEL, D_MODEL)

LANES = 128
VMEM_LIMIT = 56 * 1024 * 1024
KEY_CHUNK = 256
Q_ROWS = 128
PROJ_ROWS = 256
HIST_ROWS = 16
NEG_BIG = -1e30
INT_MIN = -2 ** 31
KEY_NEG_INF = -2139095041

_SEG = dict(q=(0, 512), k=(512, 768), v=(768, 1024), qi=(1024, 1536), kw=(1536, 1664),
            u=(1664, 2176), ga=(2176, 3200), gb=(3200, 4224))
PROJ_PACKED = 4224

f32 = jnp.float32
bf16 = jnp.bfloat16
i32 = jnp.int32


def _params(*sem):
    return pltpu.CompilerParams(dimension_semantics=sem, vmem_limit_bytes=VMEM_LIMIT)


def _rms(x, g):
    return x * lax.rsqrt(jnp.mean(x * x, axis=-1, keepdims=True) + RMS_EPS) * g


def _to_key(x):
    x = jnp.where(x == 0.0, 0.0, x)
    b = lax.bitcast_convert_type(x, i32)
    return b ^ ((b >> 31) & 0x7FFFFFFF)


def _rope(x, cos, sin_signed):
    lane = lax.broadcasted_iota(i32, (x.shape[0], LANES), 1)
    first_half = (lane % HEAD_DIM) < HEAD_DIM // 2
    outs = []
    for c in range(x.shape[1] // LANES):
        xc = x[:, c * LANES:(c + 1) * LANES]
        partner = jnp.where(first_half, pltpu.roll(xc, LANES - HEAD_DIM // 2, 1),
                            pltpu.roll(xc, HEAD_DIM // 2, 1))
        outs.append(xc * cos + partner * sin_signed)
    return outs[0] if len(outs) == 1 else jnp.concatenate(outs, axis=1)


def _proj_kernel(x_ref, g_ref, w_ref, cos_ref, sin_ref,
                 q_ref, k_ref, v_ref, qi_ref, kw_ref, u_ref, ga_ref, gb_ref):
    xb = _rms(x_ref[...], g_ref[...]).astype(bf16)
    cos = cos_ref[...]
    sin = sin_ref[...]

    def seg(name):
        a, b = _SEG[name]
        return jnp.dot(xb, w_ref[:, a:b], preferred_element_type=f32)

    q_ref[...] = (_rope(seg('q'), cos, sin) * (HEAD_DIM ** -0.5)).astype(bf16)
    k_ref[...] = _rope(seg('k'), cos, sin)
    v_ref[...] = seg('v')
    qi_ref[...] = (_rope(seg('qi'), cos, sin) * (IDX_DIM ** -0.5)).astype(bf16)
    kw = seg('kw')
    lane = lax.broadcasted_iota(i32, kw.shape, 1)
    kw_ref[...] = jnp.where(lane < IDX_DIM, _rope(kw, cos, sin), kw * (N_IDX_HEADS ** -0.5))
    u_ref[...] = seg('u')
    ga_ref[...] = seg('ga')
    gb_ref[...] = seg('gb')


def _project(x2d, g, w_packed, cos_tab, sin_tab):
    n = x2d.shape[0]
    tm = PROJ_ROWS
    ntab = cos_tab.shape[0] // tm
    row = lambda i: (i, 0)
    const = lambda i: (0, 0)
    tab = lambda i: (i % ntab, 0)
    widths = (512, 256, 256, 512, 128, 512, 1024, 1024)
    dtypes = (bf16, f32, f32, bf16, f32, f32, f32, f32)
    return pl.pallas_call(
        _proj_kernel,
        grid=(n // tm,),
        in_specs=[pl.BlockSpec((tm, D_MODEL), row),
                  pl.BlockSpec((1, D_MODEL), const),
                  pl.BlockSpec((D_MODEL, PROJ_PACKED), const),
                  pl.BlockSpec((tm, LANES), tab),
                  pl.BlockSpec((tm, LANES), tab)],
        out_specs=[pl.BlockSpec((tm, w), row) for w in widths],
        out_shape=[jax.ShapeDtypeStruct((n, w), d) for w, d in zip(widths, dtypes)],
        compiler_params=_params("arbitrary"),
        name="project",
    )(x2d, g, w_packed, cos_tab, sin_tab)


def _count(load_chunk, nch, rows, pred):
    def body(c, acc):
        ind = jnp.where(pred(load_chunk(c), c), 1.0, 0.0)
        for s in range(KEY_CHUNK // LANES):
            acc = acc + ind[:, s * LANES:(s + 1) * LANES]
        return acc
    acc = jnp.zeros((rows, LANES), f32)
    if isinstance(nch, int):
        for c in range(nch):
            acc = body(c, acc)
    else:
        acc = lax.fori_loop(0, nch, body, acc)
    return jnp.sum(acc, axis=-1, keepdims=True)


def _kth_largest_key(load_chunk, nch, rows, ksel, nbits):
    def bit_body(it, t):
        cand = t + lax.shift_left(jnp.int32(1), 31 - it)
        cnt = _count(load_chunk, nch, rows, lambda kk, c: kk >= cand)
        return jnp.where(cnt >= ksel, cand, t)
    t = lax.fori_loop(0, nbits, bit_body, jnp.full((rows, 1), INT_MIN, i32))
    return jnp.maximum(t, KEY_NEG_INF + 1)


def _demote_excess_ties(load_chunk, store_chunk, nch, rows, ksel, t, idx_bits):
    c_ge = _count(load_chunk, nch, rows, lambda kk, c: kk >= t)

    @pl.when(jnp.max(c_ge) > ksel)
    def _():
        need = ksel - _count(load_chunk, nch, rows, lambda kk, c: kk > t)

        def col_of(c):
            return c * KEY_CHUNK + lax.broadcasted_iota(i32, (rows, KEY_CHUNK), 1)

        def bit_body(it, p):
            cand = p + lax.shift_left(jnp.int32(1), idx_bits - 1 - it)
            cnt = _count(load_chunk, nch, rows,
                         lambda kk, c: jnp.where(kk == t, col_of(c), cand) < cand)
            return jnp.where(cnt < need, cand, p)
        p = lax.fori_loop(0, idx_bits, bit_body, jnp.zeros((rows, 1), i32))

        def rewrite(c, carry):
            kk = load_chunk(c)
            drop = jnp.where(kk == t, col_of(c), p) > p
            store_chunk(c, jnp.where(drop, t - 1, kk))
            return carry
        if isinstance(nch, int):
            for c in range(nch):
                rewrite(c, 0)
        else:
            lax.fori_loop(0, nch, rewrite, 0)


def _prompt_attn_kernel(qi_ref, q_ref, kwq_ref, kw_ref, k_ref, v_ref, o_ref,
                        kit2, kt2, vb, keys, qil, ql, m_s, l_s, acc_s, *, ksel):
    i = pl.program_id(1)
    seq = kw_ref.shape[0]
    ck = KEY_CHUNK
    half = HEAD_DIM

    @pl.when(i == 0)
    def _prepare_keys():
        def body(c, carry):
            rows = pl.ds(pl.multiple_of(c * ck, ck), ck)
            kit = kw_ref[rows, :].T[0:half].astype(bf16)
            kit2[c, 0:half, :] = kit
            kit2[c, half:2 * half, :] = kit
            kc = k_ref[rows, :]
            for pair in range(2):
                kt = kc[:, pair * LANES:(pair + 1) * LANES].T.astype(bf16)
                for sub in range(2):
                    one = kt[sub * half:(sub + 1) * half]
                    kt2[c, 2 * pair + sub, 0:half, :] = one
                    kt2[c, 2 * pair + sub, half:2 * half, :] = one
            vb[c] = v_ref[rows, :].astype(bf16)
            return carry
        lax.fori_loop(0, seq // ck, body, 0)

    lane = lax.broadcasted_iota(i32, (Q_ROWS, LANES), 1)
    low = lane < half
    for p in range(4):
        c = qi_ref[:, p * LANES:(p + 1) * LANES].astype(f32)
        qil[(2 * p) * Q_ROWS:(2 * p + 1) * Q_ROWS, :] = jnp.where(low, c, 0.0).astype(bf16)
        qil[(2 * p + 1) * Q_ROWS:(2 * p + 2) * Q_ROWS, :] = jnp.where(low, 0.0, c).astype(bf16)
        c = q_ref[:, p * LANES:(p + 1) * LANES].astype(f32)
        ql[p, 0:Q_ROWS, :] = jnp.where(low, c, 0.0).astype(bf16)
        ql[p, Q_ROWS:2 * Q_ROWS, :] = jnp.where(low, 0.0, c).astype(bf16)
    w_head = [kwq_ref[:, IDX_DIM + h:IDX_DIM + h + 1] for h in range(N_IDX_HEADS)]

    nch = (i + 2) // 2
    row_id = i * Q_ROWS + lax.broadcasted_iota(i32, (Q_ROWS, ck), 0)

    def score_chunk(c, carry):
        s = jnp.dot(qil[...], kit2[c], preferred_element_type=f32)
        acc = None
        for h in range(N_IDX_HEADS):
            term = jnp.maximum(s[h * Q_ROWS:(h + 1) * Q_ROWS], 0.0) * w_head[h]
            acc = term if acc is None else acc + term
        col = c * ck + lax.broadcasted_iota(i32, (Q_ROWS, ck), 1)
        keys[c] = _to_key(jnp.where(col <= row_id, acc, -jnp.inf))
        return carry
    lax.fori_loop(0, nch, score_chunk, 0)

    load = lambda c: keys[c]

    def store(c, val):
        keys[c] = val
    nbits = jnp.where((i + 1) * Q_ROWS <= ksel, 0, 32)
    t = _kth_largest_key(load, nch, Q_ROWS, ksel, nbits)
    _demote_excess_ties(load, store, nch, Q_ROWS, ksel, t, (seq - 1).bit_length())

    m_s[...] = jnp.full(m_s.shape, NEG_BIG, f32)
    l_s[...] = jnp.zeros(l_s.shape, f32)
    acc_s[...] = jnp.zeros(acc_s.shape, f32)

    def attend_chunk(c, carry):
        bias = jnp.where(keys[c] >= t, 0.0, NEG_BIG)
        bias2 = jnp.concatenate([bias, bias], axis=0)
        for p in range(N_KV_HEADS):
            lg = jnp.dot(ql[p], kt2[c, p], preferred_element_type=f32) + bias2
            m_old = m_s[p]
            m_new = jnp.maximum(m_old, jnp.max(lg, axis=-1, keepdims=True))
            alpha = jnp.exp(m_old - m_new)
            pr = jnp.exp(lg - m_new[:, 0:1])
            l_s[p] = alpha * l_s[p] + jnp.sum(pr, axis=-1, keepdims=True)
            pv = jnp.dot(pr.astype(bf16), vb[c, :, (p // 2) * LANES:(p // 2 + 1) * LANES],
                         preferred_element_type=f32)
            acc_s[p] = acc_s[p] * alpha + pv
            m_s[p] = m_new
        return carry
    lax.fori_loop(0, nch, attend_chunk, 0)

    for p in range(N_KV_HEADS):
        o = acc_s[p] / l_s[p]
        a, b = o[0:Q_ROWS], o[Q_ROWS:2 * Q_ROWS]
        if p % 2 == 0:
            b = pltpu.roll(b, half, 1)
        else:
            a = pltpu.roll(a, half, 1)
        o_ref[:, p * LANES:(p + 1) * LANES] = jnp.where(low, a, b).astype(bf16)


def _prompt_attention(qi, q, kw, k, v, batch, seq, ksel):
    nqb = seq // Q_ROWS
    nck = seq // KEY_CHUNK
    qrow = lambda b, i: (b * nqb + i, 0)
    whole = lambda b, i: (b, 0)
    return pl.pallas_call(
        functools.partial(_prompt_attn_kernel, ksel=ksel),
        grid=(batch, nqb),
        in_specs=[pl.BlockSpec((Q_ROWS, 512), qrow),
                  pl.BlockSpec((Q_ROWS, 512), qrow),
                  pl.BlockSpec((Q_ROWS, LANES), qrow),
                  pl.BlockSpec((seq, LANES), whole),
                  pl.BlockSpec((seq, KV_WIDTH), whole),
                  pl.BlockSpec((seq, KV_WIDTH), whole)],
        out_specs=pl.BlockSpec((Q_ROWS, ATTN_WIDTH), qrow),
        out_shape=jax.ShapeDtypeStruct((batch * seq, ATTN_WIDTH), bf16),
        scratch_shapes=[pltpu.VMEM((nck, LANES, KEY_CHUNK), bf16),
                        pltpu.VMEM((nck, N_KV_HEADS, LANES, KEY_CHUNK), bf16),
                        pltpu.VMEM((nck, KEY_CHUNK, KV_WIDTH), bf16),
                        pltpu.VMEM((nck, Q_ROWS, KEY_CHUNK), i32),
                        pltpu.VMEM((N_IDX_HEADS * Q_ROWS, LANES), bf16),
                        pltpu.VMEM((N_KV_HEADS, 2 * Q_ROWS, LANES), bf16),
                        pltpu.VMEM((N_KV_HEADS, 2 * Q_ROWS, LANES), f32),
                        pltpu.VMEM((N_KV_HEADS, 2 * Q_ROWS, LANES), f32),
                        pltpu.VMEM((N_KV_HEADS, 2 * Q_ROWS, LANES), f32)],
        compiler_params=_params("arbitrary", "arbitrary"),
        name="prompt_attention",
    )(qi, q, kw, kw, k, v)


PAGES_PER_STEP = 8


def _sample_score_kernel(pt_ref, qi_ref, w_ref, kin_ref, *refs):
    pages = refs[:PAGES_PER_STEP]
    out_ref = refs[PAGES_PER_STEP]
    j = pl.program_id(1)
    last = pl.num_programs(1) - 1
    qi = qi_ref[...]
    w = w_ref[...]

    def score(page):
        s = lax.dot_general(qi, page.astype(bf16), (((1,), (1,)), ((), ())),
                            preferred_element_type=f32)
        s = jnp.maximum(s, 0.0) * w
        return jnp.sum(s.reshape(N_IDX_HEADS, 8, s.shape[-1]), axis=0)

    @pl.when(j < last)
    def _():
        for n in range(PAGES_PER_STEP):
            out_ref[:, n * LANES:(n + 1) * LANES] = _to_key(score(pages[n][...]))

    @pl.when(j == last)
    def _():
        sc = score(kin_ref[...])
        col = lax.broadcasted_iota(i32, sc.shape, 1)
        row = lax.broadcasted_iota(i32, sc.shape, 0)
        out_ref[...] = jnp.full(out_ref.shape, KEY_NEG_INF, i32)
        out_ref[:, 0:LANES] = _to_key(jnp.where(col <= row, sc, -jnp.inf))


def _page_spec(width, n, n_pages):
    def index(b, j, pt):
        return (pt[b, jnp.minimum(j * PAGES_PER_STEP + n, n_pages - 1)], 0, 0)
    return pl.BlockSpec((None, LANES, width), index)


def _sample_scores(page_table, qi_s, w_s, ki_new, idx_pages):
    db, n_pages = page_table.shape
    steps = n_pages // PAGES_PER_STEP + 1
    per_b = lambda b, j, pt: (b, 0, 0)
    step_w = PAGES_PER_STEP * LANES
    return pl.pallas_call(
        _sample_score_kernel,
        grid_spec=pltpu.PrefetchScalarGridSpec(
            num_scalar_prefetch=1,
            grid=(db, steps),
            in_specs=[pl.BlockSpec((None, 64, IDX_DIM), per_b),
                      pl.BlockSpec((None, 64, 1), per_b),
                      pl.BlockSpec((None, LANES, IDX_DIM), per_b)]
                     + [_page_spec(IDX_DIM, n, n_pages) for n in range(PAGES_PER_STEP)],
            out_specs=pl.BlockSpec((None, 8, step_w), lambda b, j, pt: (b, 0, j)),
        ),
        out_shape=jax.ShapeDtypeStruct((db, 8, steps * step_w), i32),
        compiler_params=_params("arbitrary", "arbitrary"),
        name="sample_scores",
    )(page_table, qi_s, w_s, ki_new, *([idx_pages] * PAGES_PER_STEP))


def _sample_threshold_kernel(keys_in, keys_out, t_out, *, ksel):
    rows, width = keys_in.shape
    nch = width // KEY_CHUNK
    keys_out[...] = keys_in[...]
    load = lambda c: keys_out[:, c * KEY_CHUNK:(c + 1) * KEY_CHUNK]

    def store(c, val):
        keys_out[:, c * KEY_CHUNK:(c + 1) * KEY_CHUNK] = val
    t = _kth_largest_key(load, nch, rows, ksel, 32)
    _demote_excess_ties(load, store, nch, rows, ksel, t, (width - 1).bit_length())
    t_out[...] = jnp.broadcast_to(t, t_out.shape)


def _sample_threshold(keys2d, ksel):
    n, width = keys2d.shape
    rows = 128
    return pl.pallas_call(
        functools.partial(_sample_threshold_kernel, ksel=ksel),
        grid=(n // rows,),
        in_specs=[pl.BlockSpec((rows, width), lambda i: (i, 0))],
        out_specs=[pl.BlockSpec((rows, width), lambda i: (i, 0)),
                   pl.BlockSpec((rows, LANES), lambda i: (i, 0))],
        out_shape=[jax.ShapeDtypeStruct((n, width), i32),
                   jax.ShapeDtypeStruct((n, LANES), i32)],
        compiler_params=_params("arbitrary"),
        name="sample_threshold",
    )(keys2d)


def _sample_attn_kernel(pt_ref, qbd_ref, keys_ref, t_ref, kn_ref, vn_ref, *refs):
    n = PAGES_PER_STEP
    k_pages, v_pages = refs[:n], refs[n:2 * n]
    o_ref, m_s, l_s, acc_s = refs[2 * n:]
    j = pl.program_id(1)
    last = pl.num_programs(1) - 1

    @pl.when(j == 0)
    def _():
        m_s[...] = jnp.full(m_s.shape, NEG_BIG, f32)
        l_s[...] = jnp.zeros(l_s.shape, f32)
        acc_s[...] = jnp.zeros(acc_s.shape, f32)

    t = t_ref[:, 0:1]

    def update(kcat, vcat, kk):
        bias = jnp.where(kk >= t, 0.0, NEG_BIG)
        bias = jnp.concatenate([bias] * N_HEADS, axis=0)
        lg = lax.dot_general(qbd_ref[...], kcat, (((1,), (1,)), ((), ())),
                             preferred_element_type=f32) + bias
        m_old = m_s[...]
        m_new = jnp.maximum(m_old, jnp.max(lg, axis=-1, keepdims=True))
        alpha = jnp.exp(m_old - m_new)
        pr = jnp.exp(lg - m_new[:, 0:1])
        l_s[...] = alpha * l_s[...] + jnp.sum(pr, axis=-1, keepdims=True)
        pv = jnp.dot(pr.astype(bf16), vcat, preferred_element_type=f32)
        acc_s[...] = acc_s[...] * jnp.concatenate([alpha, alpha], axis=1) + pv
        m_s[...] = m_new

    @pl.when(j < last)
    def _():
        kcat = jnp.concatenate([r[...] for r in k_pages], axis=0).astype(bf16)
        vcat = jnp.concatenate([r[...] for r in v_pages], axis=0).astype(bf16)
        update(kcat, vcat, keys_ref[...])

    @pl.when(j == last)
    def _():
        update(kn_ref[...], vn_ref[...], keys_ref[:, 0:LANES])
        o = acc_s[...] / jnp.concatenate([l_s[...], l_s[...]], axis=1)
        for h in range(N_HEADS):
            kv = h // 2
            o_ref[:, h * HEAD_DIM:(h + 1) * HEAD_DIM] = (
                o[h * 8:(h + 1) * 8, kv * HEAD_DIM:(kv + 1) * HEAD_DIM])


def _sample_attention(page_table, q_bd, keys3d, t3d, k_new, v_new, k_pages, v_pages):
    db, n_pages = page_table.shape
    steps = n_pages // PAGES_PER_STEP + 1
    per_b = lambda b, j, pt: (b, 0, 0)
    step_w = PAGES_PER_STEP * LANES
    return pl.pallas_call(
        _sample_attn_kernel,
        grid_spec=pltpu.PrefetchScalarGridSpec(
            num_scalar_prefetch=1,
            grid=(db, steps),
            in_specs=[pl.BlockSpec((None, 64, KV_WIDTH), per_b),
                      pl.BlockSpec((None, 8, step_w), lambda b, j, pt: (b, 0, j)),
                      pl.BlockSpec((None, 8, LANES), per_b),
                      pl.BlockSpec((None, LANES, KV_WIDTH), per_b),
                      pl.BlockSpec((None, LANES, KV_WIDTH), per_b)]
                     + [_page_spec(KV_WIDTH, n, n_pages) for n in range(PAGES_PER_STEP)] * 2,
            out_specs=pl.BlockSpec((None, 8, ATTN_WIDTH), per_b),
            scratch_shapes=[pltpu.VMEM((64, LANES), f32),
                            pltpu.VMEM((64, LANES), f32),
                            pltpu.VMEM((64, KV_WIDTH), f32)],
        ),
        out_shape=jax.ShapeDtypeStruct((db, 8, ATTN_WIDTH), f32),
        compiler_params=_params("arbitrary", "arbitrary"),
        name="sample_attention",
    )(page_table, q_bd, keys3d, t3d, k_new, v_new,
      *([k_pages] * PAGES_PER_STEP), *([v_pages] * PAGES_PER_STEP))


def _pool_diff_prompt(u_ref, hist_ref, seq):
    tm = u_ref.shape[0]
    start = (pl.program_id(0) * tm) % seq
    u = u_ref[...]
    hist = hist_ref[...] * jnp.where(start == 0, 0.0, 1.0)
    ext = jnp.concatenate([hist, u], axis=0)
    pos = start + lax.broadcasted_iota(i32, (tm, 1), 0)
    outs = []
    for g, win in enumerate(POOL_WINDOWS):
        s = ext[:, g * POOL_GROUP:(g + 1) * POOL_GROUP]
        span = 1
        while span < win:
            s = s + pltpu.roll(s, span, 0)
            span *= 2
        cnt = jnp.minimum(pos + 1, win).astype(f32)
        outs.append(s[HIST_ROWS:] / cnt - u[:, g * POOL_GROUP:(g + 1) * POOL_GROUP])
    return outs


def _pool_diff_sample(u_ref):
    nb = u_ref.shape[0]
    outs = []
    for g, win in enumerate(POOL_WINDOWS):
        lanes = slice(g * POOL_GROUP, (g + 1) * POOL_GROUP)
        cur = u_ref[:, HIST_ROWS:HIST_ROWS + 8, lanes]
        s = cur
        for d in range(1, win):
            s = s + u_ref[:, HIST_ROWS - d:HIST_ROWS + 8 - d, lanes]
        outs.append((s / float(win) - cur).reshape(nb * 8, POOL_GROUP))
    return outs


def _merge_kernel(x_ref, ao_ref, u_ref, hist_ref, ga_ref, gb_ref, wpool_ref, pscale_ref,
                  wba_ref, wbp_ref, wout_ref, fg_ref, wr_ref, br_ref,
                  h_ref, hn_ref, route_ref, *, seq):
    diffs = _pool_diff_sample(u_ref) if seq is None else _pool_diff_prompt(u_ref, hist_ref, seq)
    pool = jnp.concatenate(
        [jnp.dot(d.astype(bf16), wpool_ref[g], preferred_element_type=f32)
         for g, d in enumerate(diffs)], axis=1) * pscale_ref[...]
    br_a = jnp.dot(ao_ref[...].astype(bf16), wba_ref[...], preferred_element_type=f32)
    br_p = jnp.dot(pool.astype(bf16), wbp_ref[...], preferred_element_type=f32)
    sig = lambda z: 1.0 / (1.0 + jnp.exp(-z))
    mix = sig(ga_ref[...]) * br_a + sig(gb_ref[...]) * br_p
    h = x_ref[...] + jnp.dot(mix.astype(bf16), wout_ref[...], preferred_element_type=f32)
    h_ref[...] = h
    hn = _rms(h, fg_ref[...])
    hn_ref[...] = hn

    logit = jnp.dot(hn, wr_ref[...], preferred_element_type=f32,
                    precision=lax.Precision.HIGHEST) + br_ref[...]
    lane_i = lax.broadcasted_iota(i32, logit.shape, 1)
    lane = lane_i.astype(f32)
    big = float(LANES)
    gl = jnp.where(lane_i < N_EXPERTS, -jnp.inf,
                   jnp.where(lane_i < N_EXPERTS + N_GROUPS, logit, -jnp.inf))
    gmax = jnp.max(gl, axis=-1, keepdims=True)
    g_sel = jnp.min(jnp.where(gl == gmax, lane, big), axis=-1, keepdims=True) - N_EXPERTS
    g_w = 1.0 / jnp.sum(jnp.exp(gl - gmax), axis=-1, keepdims=True)
    group_of_lane = (lane_i // EXPERTS_PER_GROUP).astype(f32)
    el = jnp.where(group_of_lane == g_sel, logit, -jnp.inf)
    v1 = jnp.max(el, axis=-1, keepdims=True)
    i1 = jnp.min(jnp.where(el == v1, lane, big), axis=-1, keepdims=True)
    el2 = jnp.where(lane == i1, -jnp.inf, el)
    v2 = jnp.max(el2, axis=-1, keepdims=True)
    i2 = jnp.min(jnp.where(el2 == v2, lane, big), axis=-1, keepdims=True)
    e21 = jnp.exp(v2 - v1)
    w1 = g_w / (1.0 + e21)
    w2 = g_w * e21 / (1.0 + e21)
    route_ref[...] = jnp.where(lane_i == 0, i1,
                     jnp.where(lane_i == 1, i2,
                     jnp.where(lane_i == 2, w1, jnp.where(lane_i == 3, w2, 0.0))))


def _merge(x2d, ao, u, ga, gb, lw, seq):
    n = x2d.shape[0]
    tm = PROJ_ROWS
    row = lambda i: (i, 0)
    const2 = lambda i: (0, 0)
    const3 = lambda i: (0, 0, 0)
    if seq is None:
        u_spec = pl.BlockSpec((tm // 8, HIST_ROWS + 8, POOL_WIDTH), lambda i: (i, 0, 0))
        hist = jnp.zeros((HIST_ROWS, POOL_WIDTH), f32)
        hist_spec = pl.BlockSpec((HIST_ROWS, POOL_WIDTH), const2)
    else:
        u_spec = pl.BlockSpec((tm, POOL_WIDTH), row)
        hist = u
        hist_spec = pl.BlockSpec((HIST_ROWS, POOL_WIDTH),
                                 lambda i: (jnp.maximum(i * (tm // HIST_ROWS) - 1, 0), 0))
    return pl.pallas_call(
        functools.partial(_merge_kernel, seq=seq),
        grid=(n // tm,),
        in_specs=[pl.BlockSpec((tm, D_MODEL), row),
                  pl.BlockSpec((tm, ATTN_WIDTH), row),
                  u_spec, hist_spec,
                  pl.BlockSpec((tm, D_MODEL), row),
                  pl.BlockSpec((tm, D_MODEL), row),
                  pl.BlockSpec((4, POOL_GROUP, POOL_GROUP), const3),
                  pl.BlockSpec((1, POOL_WIDTH), const2),
                  pl.BlockSpec((ATTN_WIDTH, D_MODEL), const2),
                  pl.BlockSpec((POOL_WIDTH, D_MODEL), const2),
                  pl.BlockSpec((D_MODEL, D_MODEL), const2),
                  pl.BlockSpec((1, D_MODEL), const2),
                  pl.BlockSpec((D_MODEL, LANES), const2),
                  pl.BlockSpec((1, LANES), const2)],
        out_specs=[pl.BlockSpec((tm, D_MODEL), row),
                   pl.BlockSpec((tm, D_MODEL), row),
                   pl.BlockSpec((tm, LANES), row)],
        out_shape=[jax.ShapeDtypeStruct((n, D_MODEL), f32),
                   jax.ShapeDtypeStruct((n, D_MODEL), f32),
                   jax.ShapeDtypeStruct((n, LANES), f32)],
        compiler_params=_params("arbitrary"),
        name="merge",
    )(x2d, ao, u, hist, ga, gb, lw['w_pool'], lw['pool_scale'], lw['w_ba'], lw['w_bp'],
      lw['w_out'], lw['ffn_g'], lw['w_router'], lw['b_router'])


RANK_ROWS = 512


def _rank_kernel(route_ref, rank_ref, counts_ref, base_s):
    @pl.when(pl.program_id(0) == 0)
    def _():
        base_s[...] = jnp.zeros(base_s.shape, f32)
    r = route_ref[...]
    tt = r.shape[0]
    lane = lax.broadcasted_iota(i32, r.shape, 1).astype(f32)
    oh1 = lane == r[:, 0:1]
    oh2 = lane == r[:, 1:2]
    oh = jnp.where(oh1, 1.0, 0.0) + jnp.where(oh2, 1.0, 0.0)
    below = (lax.broadcasted_iota(i32, (tt, tt), 0) > lax.broadcasted_iota(i32, (tt, tt), 1))
    before = jnp.dot(jnp.where(below, 1.0, 0.0).astype(bf16), oh.astype(bf16),
                     preferred_element_type=f32) + base_s[...]
    r1 = jnp.sum(jnp.where(oh1, before, 0.0), axis=-1, keepdims=True)
    r2 = jnp.sum(jnp.where(oh2, before, 0.0), axis=-1, keepdims=True)
    lane_i = lax.broadcasted_iota(i32, r.shape, 1)
    rank_ref[...] = jnp.where(lane_i == 0, r1, jnp.where(lane_i == 1, r2, 0.0))
    base_s[...] = base_s[...] + jnp.sum(oh, axis=0, keepdims=True)
    counts_ref[...] = base_s[...]


def _expert_ranks(route):
    n = route.shape[0]
    tt = min(RANK_ROWS, n)
    return pl.pallas_call(
        _rank_kernel,
        grid=(n // tt,),
        in_specs=[pl.BlockSpec((tt, LANES), lambda i: (i, 0))],
        out_specs=[pl.BlockSpec((tt, LANES), lambda i: (i, 0)),
                   pl.BlockSpec((1, LANES), lambda i: (0, 0))],
        out_shape=[jax.ShapeDtypeStruct((n, LANES), f32),
                   jax.ShapeDtypeStruct((1, LANES), f32)],
        scratch_shapes=[pltpu.VMEM((1, LANES), f32)],
        compiler_params=_params("arbitrary"),
        name="expert_ranks",
    )(route)


DISPATCH_ROWS = 256


def _row_copy(src, src_row, dst, dst_row, sem):
    return pltpu.make_async_copy(src.at[pl.ds(src_row, 1)], dst.at[pl.ds(dst_row, 1)], sem)


def _dispatch_kernel(dest_ref, x_hbm, init_hbm, xs_hbm, sem):
    del init_hbm
    tt = dest_ref.shape[1] // TOP_K
    base = pl.program_id(0) * tt

    def start(t, carry):
        for j in range(TOP_K):
            _row_copy(x_hbm, base + t, xs_hbm, dest_ref[0, TOP_K * t + j], sem).start()
        return carry
    lax.fori_loop(0, tt, start, 0)

    def wait(t, carry):
        for j in range(TOP_K):
            _row_copy(x_hbm, base + t, xs_hbm, dest_ref[0, TOP_K * t + j], sem).wait()
        return carry
    lax.fori_loop(0, tt, wait, 0)


def _dispatch(dest, x, n_slots):
    n = x.shape[0]
    tt = DISPATCH_ROWS
    dest3 = dest.reshape(n // tt, 1, tt * TOP_K)
    return pl.pallas_call(
        _dispatch_kernel,
        grid=(n // tt,),
        in_specs=[pl.BlockSpec((None, 1, tt * TOP_K), lambda i: (i, 0, 0),
                               memory_space=pltpu.SMEM),
                  pl.BlockSpec(memory_space=pl.ANY),
                  pl.BlockSpec(memory_space=pl.ANY)],
        out_specs=pl.BlockSpec(memory_space=pl.ANY),
        out_shape=jax.ShapeDtypeStruct((n_slots, D_MODEL), f32),
        scratch_shapes=[pltpu.SemaphoreType.DMA(())],
        input_output_aliases={2: 0},
        compiler_params=_params("arbitrary"),
        name="moe_dispatch",
    )(dest3, x, jnp.zeros((n_slots, D_MODEL), f32))


def _expert_ffn_kernel(be_ref, nact_ref, xs_ref, wg_ref, wu_ref, wd_ref, ys_ref):
    i = pl.program_id(0)

    @pl.when(i < nact_ref[0])
    def _():
        x = xs_ref[...].astype(bf16)
        g = jnp.dot(x, wg_ref[...], preferred_element_type=f32)
        u = jnp.dot(x, wu_ref[...], preferred_element_type=f32)
        hb = (g / (1.0 + jnp.exp(-g))) * u
        ys_ref[...] = jnp.dot(hb.astype(bf16), wd_ref[...], preferred_element_type=f32)

    @pl.when(i >= nact_ref[0])
    def _():
        ys_ref[...] = jnp.zeros(ys_ref.shape, f32)


def _expert_ffn(block_expert, n_active, xs, wg, wu, wd, bm):
    n_blocks = xs.shape[0] // bm
    rows = lambda i, be, na: (i, 0)
    by_expert = lambda i, be, na: (be[i], 0, 0)
    return pl.pallas_call(
        _expert_ffn_kernel,
        grid_spec=pltpu.PrefetchScalarGridSpec(
            num_scalar_prefetch=2,
            grid=(n_blocks,),
            in_specs=[pl.BlockSpec((bm, D_MODEL), rows),
                      pl.BlockSpec((None, D_MODEL, D_EXPERT), by_expert),
                      pl.BlockSpec((None, D_MODEL, D_EXPERT), by_expert),
                      pl.BlockSpec((None, D_EXPERT, D_MODEL), by_expert)],
            out_specs=pl.BlockSpec((bm, D_MODEL), rows),
        ),
        out_shape=jax.ShapeDtypeStruct(xs.shape, f32),
        compiler_params=_params("arbitrary"),
        name="expert_ffn",
    )(block_expert, n_active, xs, wg, wu, wd)


COMBINE_ROWS = 128


def _combine_kernel(dest_ref, h_ref, route_ref, g_ref, ys_hbm, y_ref, buf, sem):
    tt = h_ref.shape[0]

    def copy(t, j):
        return pltpu.make_async_copy(ys_hbm.at[pl.ds(dest_ref[0, TOP_K * t + j], 1)],
                                     buf.at[j, pl.ds(t, 1)], sem)

    def start(t, carry):
        for j in range(TOP_K):
            copy(t, j).start()
        return carry
    lax.fori_loop(0, tt, start, 0)

    def wait(t, carry):
        for j in range(TOP_K):
            copy(t, j).wait()
        return carry
    lax.fori_loop(0, tt, wait, 0)

    r = route_ref[...]
    moe = buf[0] * r[:, 2:3] + buf[1] * r[:, 3:4]
    y_ref[...] = _rms(h_ref[...] + moe, g_ref[...])


def _combine(dest, h, route, final_g, ys):
    n = h.shape[0]
    tt = COMBINE_ROWS
    dest3 = dest.reshape(n // tt, 1, tt * TOP_K)
    row = lambda i: (i, 0)
    return pl.pallas_call(
        _combine_kernel,
        grid=(n // tt,),
        in_specs=[pl.BlockSpec((None, 1, tt * TOP_K), lambda i: (i, 0, 0),
                               memory_space=pltpu.SMEM),
                  pl.BlockSpec((tt, D_MODEL), row),
                  pl.BlockSpec((tt, LANES), row),
                  pl.BlockSpec((1, D_MODEL), lambda i: (0, 0)),
                  pl.BlockSpec(memory_space=pl.ANY)],
        out_specs=pl.BlockSpec((tt, D_MODEL), row),
        out_shape=jax.ShapeDtypeStruct((n, D_MODEL), f32),
        scratch_shapes=[pltpu.VMEM((TOP_K, tt, D_MODEL), f32),
                        pltpu.SemaphoreType.DMA(())],
        compiler_params=_params("arbitrary"),
        name="moe_combine",
    )(dest3, h, route, final_g, ys)


def _moe_and_final_norm(h, hn, route, lw, final_g, bm):
    n = h.shape[0]
    rank, counts = _expert_ranks(route)
    counts = counts[0, :N_EXPERTS].astype(i32)
    padded = (counts + bm - 1) // bm * bm
    pend = jnp.cumsum(padded)
    pstart = pend - padded
    e_id = route[:, 0:TOP_K].astype(i32)
    dest = (pstart[e_id] + rank[:, 0:TOP_K].astype(i32)).reshape(-1)
    n_blocks = (n * TOP_K + N_EXPERTS * (bm - 1)) // bm + 1
    n_active = (pend[-1] // bm).astype(i32)
    blk = jnp.minimum(jnp.arange(n_blocks, dtype=i32), n_active - 1) * bm
    block_expert = jnp.minimum(jnp.searchsorted(pend, blk, side='right'),
                               N_EXPERTS - 1).astype(i32)
    xs = _dispatch(dest, hn, n_blocks * bm)
    ys = _expert_ffn(block_expert, n_active.reshape(1), xs, lw['w_eg'], lw['w_eu'], lw['w_ed'], bm)
    return _combine(dest, h, route, final_g, ys)


def _rope_tables(pos, reps):
    half = HEAD_DIM // 2
    inv_freq = 1.0 / (ROPE_THETA ** (jnp.arange(half, dtype=f32) / half))
    ang = pos.astype(f32)[:, None] * inv_freq[None, :]
    cos, sin = jnp.cos(ang), jnp.sin(ang)
    cos = jnp.concatenate([cos, cos, cos, cos], axis=1)
    sin = jnp.concatenate([-sin, sin, -sin, sin], axis=1)
    return jnp.tile(cos, (reps, 1)), jnp.tile(sin, (reps, 1))


def _layer_weights(l, attn_norm_g, w_in, w_pool, pool_scale, w_branch_attn, w_branch_pool, w_out,
                   ffn_norm_g, w_rg, b_rg, w_re, b_re, w_eg, w_eu, w_ed):
    cuts = [0]
    for s in SEGMENTS:
        cuts.append(cuts[-1] + s)
    w = w_in[l]
    kw_pad = LANES - IDX_DIM - N_IDX_HEADS
    w_packed = jnp.concatenate(
        [w[:, cuts[0]:cuts[4]], w[:, cuts[4]:cuts[6]], jnp.zeros((D_MODEL, kw_pad), f32),
         w[:, cuts[6]:]], axis=1).astype(bf16)
    r_pad = LANES - N_EXPERTS - N_GROUPS
    return dict(
        attn_g=attn_norm_g[l].reshape(1, D_MODEL), w_packed=w_packed,
        w_pool=w_pool[l].astype(bf16), pool_scale=pool_scale[l].reshape(1, POOL_WIDTH),
        w_ba=w_branch_attn[l].astype(bf16), w_bp=w_branch_pool[l].astype(bf16),
        w_out=w_out[l].astype(bf16), ffn_g=ffn_norm_g[l].reshape(1, D_MODEL),
        w_router=jnp.concatenate([w_re[l], w_rg[l], jnp.zeros((D_MODEL, r_pad), f32)], axis=1),
        b_router=jnp.concatenate([b_re[l], b_rg[l], jnp.zeros((r_pad,), f32)]).reshape(1, LANES),
        w_eg=w_eg[l].astype(bf16), w_eu=w_eu[l].astype(bf16), w_ed=w_ed[l].astype(bf16))


def _prompt_layer(x, lw, final_g):
    batch, seq, _ = x.shape
    x2d = x.reshape(batch * seq, D_MODEL)
    cos, sin = _rope_tables(jnp.arange(seq), 1)
    q, k, v, qi, kw, u, ga, gb = _project(x2d, lw['attn_g'], lw['w_packed'], cos, sin)
    ao = _prompt_attention(qi, q, kw, k, v, batch, seq, min(TOPK_MAX, seq // 4))
    h, hn, route = _merge(x2d, ao, u, ga, gb, lw, seq)
    y = _moe_and_final_norm(h, hn, route, lw, final_g, bm=256)
    return (y.reshape(batch, seq, D_MODEL),
            k.reshape(batch, seq, N_KV_HEADS, HEAD_DIM), v.reshape(batch, seq, N_KV_HEADS, HEAD_DIM),
            kw[:, :IDX_DIM].reshape(batch, seq, IDX_DIM),
            u.reshape(batch, seq, POOL_WIDTH)[:, seq - POOL_HIST:])


def _sample_layer(x, l, cache_k, cache_v, cache_idx_k, state_pool, page_table, lw, final_g):
    db, ds, _ = x.shape
    n_phys, page = cache_k.shape[1], cache_k.shape[2]
    past = page_table.shape[1] * page
    x2d = x.reshape(db * ds, D_MODEL)
    cos, sin = _rope_tables(past + jnp.arange(ds), PROJ_ROWS // ds)
    q, k, v, qi, kw, u, ga, gb = _project(x2d, lw['attn_g'], lw['w_packed'], cos, sin)

    qi_s = qi.reshape(db, ds, N_IDX_HEADS, IDX_DIM).transpose(0, 2, 1, 3).reshape(db, 64, IDX_DIM)
    w_s = kw[:, IDX_DIM:IDX_DIM + N_IDX_HEADS].reshape(db, ds, N_IDX_HEADS)
    w_s = w_s.transpose(0, 2, 1).reshape(db, 64, 1)
    pad_rows = lambda a: jnp.pad(a, ((0, 0), (0, page - ds), (0, 0)))
    ki_new = pad_rows(kw[:, :IDX_DIM].reshape(db, ds, IDX_DIM))
    keys = _sample_scores(page_table, qi_s, w_s, ki_new,
                          cache_idx_k[l].reshape(n_phys, page, IDX_DIM))
    width = keys.shape[-1]
    keys, t = _sample_threshold(keys.reshape(db * ds, width), min(TOPK_MAX, (past + ds) // 4))

    q_t = q.reshape(db, ds, N_HEADS, HEAD_DIM).transpose(0, 2, 1, 3)
    kv_of_head = (jnp.arange(N_HEADS)[:, None] // 2 == jnp.arange(N_KV_HEADS)[None, :])
    q_bd = jnp.where(kv_of_head[None, :, None, :, None], q_t[:, :, :, None, :], 0)
    q_bd = q_bd.reshape(db, 64, KV_WIDTH).astype(bf16)
    ao = _sample_attention(page_table, q_bd, keys.reshape(db, ds, width), t.reshape(db, ds, LANES),
                           pad_rows(k.reshape(db, ds, KV_WIDTH)).astype(bf16),
                           pad_rows(v.reshape(db, ds, KV_WIDTH)).astype(bf16),
                           cache_k[l].reshape(n_phys, page, KV_WIDTH),
                           cache_v[l].reshape(n_phys, page, KV_WIDTH))

    u3 = u.reshape(db, ds, POOL_WIDTH)
    u_all = jnp.concatenate([jnp.zeros((db, 1, POOL_WIDTH), f32), state_pool[l], u3], axis=1)
    h, hn, route = _merge(x2d, ao.reshape(db * ds, ATTN_WIDTH), u_all, ga, gb, lw, None)
    y = _moe_and_final_norm(h, hn, route, lw, final_g, bm=128)
    return (y.reshape(db, ds, D_MODEL),
            k.reshape(db, ds, N_KV_HEADS, HEAD_DIM), v.reshape(db, ds, N_KV_HEADS, HEAD_DIM),
            kw[:, :IDX_DIM].reshape(db, ds, IDX_DIM),
            u_all[:, -POOL_HIST:])


def kernel(x_prompt, x_sample, cache_k, cache_v, cache_idx_k, state_pool, page_table, attn_norm_g, w_in, w_pool, pool_scale, w_branch_attn, w_branch_pool, w_out, ffn_norm_g, w_router_group, b_router_group, w_router_expert, b_router_expert, w_exp_gate, w_exp_up, w_exp_down, final_norm_g):
    depth = w_in.shape[0]
    assert depth == 1, "the final norm is fused into the (single) layer's MoE combine"
    final_g = final_norm_g.reshape(1, D_MODEL)
    lw = _layer_weights(0, attn_norm_g, w_in, w_pool, pool_scale, w_branch_attn, w_branch_pool,
                        w_out, ffn_norm_g, w_router_group, b_router_group, w_router_expert,
                        b_router_expert, w_exp_gate, w_exp_up, w_exp_down)
    yp, kp, vp, kip, up = _prompt_layer(x_prompt, lw, final_g)
    ys, ks, vs, kis, us = _sample_layer(x_sample, 0, cache_k, cache_v, cache_idx_k, state_pool,
                                        page_table, lw, final_g)
    stack = lambda a: a[None]
    return (yp, ys, stack(kp), stack(vp), stack(kip), stack(up),
            stack(ks), stack(vs), stack(kis), stack(us))
```

```python
import functools

import jax
import jax.numpy as jnp
from jax import lax
from jax.experimental import pallas as pl
from jax.experimental.pallas import tpu as pltpu

D_MODEL = 1024
HEAD_DIM = 64
N_HEADS = 8
N_KV_HEADS = 4
ATTN_WIDTH = N_HEADS * HEAD_DIM
KV_WIDTH = N_KV_HEADS * HEAD_DIM
N_IDX_HEADS = 8
IDX_DIM = 64
TOPK_MAX = 256
ROPE_THETA = 10000.0
POOL_WINDOWS = (2, 4, 8, 16)
POOL_WIDTH = 512
POOL_GROUP = 128
POOL_HIST = 15
N_GROUPS = 4
EXPERTS_PER_GROUP = 8
N_EXPERTS = 32
TOP_K = 2
D_EXPERT = 512
RMS_EPS = 1e-6
SEGMENTS = (ATTN_WIDTH, KV_WIDTH, KV_WIDTH, N_IDX_HEADS * IDX_DIM, IDX_DIM, N_IDX_HEADS,
            POOL_WIDTH, D_MODEL, D_MODEL)

LANES = 128
VMEM_LIMIT = 56 * 1024 * 1024
KEY_CHUNK = 256
Q_ROWS = 128
PROJ_ROWS = 256
HIST_ROWS = 16
NEG_BIG = -1e30
INT_MIN = -2 ** 31
KEY_NEG_INF = -2139095041
FLT_LOWEST = -3.4028234663852886e38

_SEG = dict(q=(0, 512), k=(512, 768), v=(768, 1024), qi=(1024, 1536), kw=(1536, 1664),
            u=(1664, 2176), ga=(2176, 3200), gb=(3200, 4224))
PROJ_PACKED = 4224

f32 = jnp.float32
bf16 = jnp.bfloat16
i32 = jnp.int32


def _params(*sem):
    return pltpu.CompilerParams(dimension_semantics=sem, vmem_limit_bytes=VMEM_LIMIT)


def _rms(x, g):
    return x * lax.rsqrt(jnp.mean(x * x, axis=-1, keepdims=True) + RMS_EPS) * g


def _key_to_float(k):
    return lax.bitcast_convert_type(k ^ ((k >> 31) & 0x7FFFFFFF), f32)


def _rope(x, cos, sin_signed):
    lane = lax.broadcasted_iota(i32, (x.shape[0], LANES), 1)
    first_half = (lane % HEAD_DIM) < HEAD_DIM // 2
    outs = []
    for c in range(x.shape[1] // LANES):
        xc = x[:, c * LANES:(c + 1) * LANES]
        partner = jnp.where(first_half, pltpu.roll(xc, LANES - HEAD_DIM // 2, 1),
                            pltpu.roll(xc, HEAD_DIM // 2, 1))
        outs.append(xc * cos + partner * sin_signed)
    return outs[0] if len(outs) == 1 else jnp.concatenate(outs, axis=1)


def _proj_kernel(x_ref, g_ref, w_ref, cos_ref, sin_ref,
                 q_ref, k_ref, v_ref, qi_ref, kw_ref, u_ref, ga_ref, gb_ref):
    xb = _rms(x_ref[...], g_ref[...]).astype(bf16)
    cos = cos_ref[...]
    sin = sin_ref[...]

    def seg(name):
        a, b = _SEG[name]
        return jnp.dot(xb, w_ref[:, a:b], preferred_element_type=f32)

    q_ref[...] = (_rope(seg('q'), cos, sin) * (HEAD_DIM ** -0.5)).astype(bf16)
    k_ref[...] = _rope(seg('k'), cos, sin)
    v_ref[...] = seg('v')
    qi_ref[...] = (_rope(seg('qi'), cos, sin) * (IDX_DIM ** -0.5)).astype(bf16)
    kw = seg('kw')
    lane = lax.broadcasted_iota(i32, kw.shape, 1)
    kw_ref[...] = jnp.where(lane < IDX_DIM, _rope(kw, cos, sin), kw * (N_IDX_HEADS ** -0.5))
    u_ref[...] = seg('u')
    ga_ref[...] = seg('ga')
    gb_ref[...] = seg('gb')


def _project(x2d, g, w_packed, cos_tab, sin_tab):
    n = x2d.shape[0]
    tm = PROJ_ROWS
    ntab = cos_tab.shape[0] // tm
    row = lambda i: (i, 0)
    const = lambda i: (0, 0)
    tab = lambda i: (i % ntab, 0)
    widths = (512, 256, 256, 512, 128, 512, 1024, 1024)
    dtypes = (bf16, f32, f32, bf16, f32, f32, f32, f32)
    return pl.pallas_call(
        _proj_kernel,
        grid=(n // tm,),
        in_specs=[pl.BlockSpec((tm, D_MODEL), row),
                  pl.BlockSpec((1, D_MODEL), const),
                  pl.BlockSpec((D_MODEL, PROJ_PACKED), const),
                  pl.BlockSpec((tm, LANES), tab),
                  pl.BlockSpec((tm, LANES), tab)],
        out_specs=[pl.BlockSpec((tm, w), row) for w in widths],
        out_shape=[jax.ShapeDtypeStruct((n, w), d) for w, d in zip(widths, dtypes)],
        compiler_params=_params("arbitrary"),
        name="project",
    )(x2d, g, w_packed, cos_tab, sin_tab)


def _count(load_chunk, nch, rows, pred):
    def body(c, acc):
        ind = jnp.where(pred(load_chunk(c), c), 1.0, 0.0)
        for s in range(KEY_CHUNK // LANES):
            acc = acc + ind[:, s * LANES:(s + 1) * LANES]
        return acc
    acc = jnp.zeros((rows, LANES), f32)
    if isinstance(nch, int):
        for c in range(nch):
            acc = body(c, acc)
    else:
        acc = lax.fori_loop(0, nch, body, acc)
    return jnp.sum(acc, axis=-1, keepdims=True)


def _kth_largest_score(load_chunk, nch, rows, ksel, search):
    def cond(state):
        it, _, cnt = state
        return jnp.logical_and(it < 32, jnp.max(cnt) > ksel)

    def body(state):
        it, key, cnt = state
        cand = key + lax.shift_left(jnp.int32(1), 31 - it)
        cand_f = _key_to_float(cand)
        c = _count(load_chunk, nch, rows, lambda sc, _: sc >= cand_f)
        take = c >= ksel
        return it + 1, jnp.where(take, cand, key), jnp.where(take, c, cnt)

    total = jnp.where(search, nch * KEY_CHUNK, 0).astype(f32)
    _, key, cnt = lax.while_loop(
        cond, body, (jnp.int32(0), jnp.full((rows, 1), INT_MIN, i32), jnp.full((rows, 1), total)))
    lifted = key <= KEY_NEG_INF
    return (jnp.where(lifted, FLT_LOWEST, _key_to_float(key)), jnp.where(lifted, 0.0, cnt))


def _demote_excess_ties(load_chunk, store_chunk, nch, rows, ksel, t, c_ge, idx_bits):
    @pl.when(jnp.max(c_ge) > ksel)
    def _():
        need = ksel - _count(load_chunk, nch, rows, lambda sc, c: sc > t)

        def col_of(c):
            return c * KEY_CHUNK + lax.broadcasted_iota(i32, (rows, KEY_CHUNK), 1)

        def bit_body(it, p):
            cand = p + lax.shift_left(jnp.int32(1), idx_bits - 1 - it)
            cnt = _count(load_chunk, nch, rows,
                         lambda sc, c: jnp.where(sc == t, col_of(c), cand) < cand)
            return jnp.where(cnt < need, cand, p)
        p = lax.fori_loop(0, idx_bits, bit_body, jnp.zeros((rows, 1), i32))

        def rewrite(c, carry):
            sc = load_chunk(c)
            drop = jnp.where(sc == t, col_of(c), p) > p
            store_chunk(c, jnp.where(drop, -jnp.inf, sc))
            return carry
        if isinstance(nch, int):
            for c in range(nch):
                rewrite(c, 0)
        else:
            lax.fori_loop(0, nch, rewrite, 0)


def _prompt_attn_kernel(qi_ref, q_ref, kwq_ref, kw_ref, k_ref, v_ref, o_ref,
                        kit2, kt2, vb, scores, qil, ql, m_s, mx_s, acc_s, *, ksel):
    i = pl.program_id(1)
    seq = kw_ref.shape[0]
    ck = KEY_CHUNK
    half = HEAD_DIM

    @pl.when(i == 0)
    def _prepare_keys():
        def body(c, carry):
            rows = pl.ds(pl.multiple_of(c * ck, ck), ck)
            kit = kw_ref[rows, :].T[0:half].astype(bf16)
            kit2[c, 0:half, :] = kit
            kit2[c, half:2 * half, :] = kit
            kc = k_ref[rows, :]
            for pair in range(2):
                kt = kc[:, pair * LANES:(pair + 1) * LANES].T.astype(bf16)
                for sub in range(2):
                    one = kt[sub * half:(sub + 1) * half]
                    kt2[c, 2 * pair + sub, 0:half, :] = one
                    kt2[c, 2 * pair + sub, half:2 * half, :] = one
            vb[c] = v_ref[rows, :].astype(bf16)
            return carry
        lax.fori_loop(0, seq // ck, body, 0)

    lane = lax.broadcasted_iota(i32, (Q_ROWS, LANES), 1)
    low = lane < half
    for p in range(4):
        c = qi_ref[:, p * LANES:(p + 1) * LANES].astype(f32)
        qil[(2 * p) * Q_ROWS:(2 * p + 1) * Q_ROWS, :] = jnp.where(low, c, 0.0).astype(bf16)
        qil[(2 * p + 1) * Q_ROWS:(2 * p + 2) * Q_ROWS, :] = jnp.where(low, 0.0, c).astype(bf16)
        c = q_ref[:, p * LANES:(p + 1) * LANES].astype(f32)
        ql[p, 0:Q_ROWS, :] = jnp.where(low, c, 0.0).astype(bf16)
        ql[p, Q_ROWS:2 * Q_ROWS, :] = jnp.where(low, 0.0, c).astype(bf16)
    w_head = [kwq_ref[:, IDX_DIM + h:IDX_DIM + h + 1] for h in range(N_IDX_HEADS)]

    nch = (i + 2) // 2
    row_id = i * Q_ROWS + lax.broadcasted_iota(i32, (Q_ROWS, ck), 0)

    def score_chunk(c, carry):
        s = jnp.dot(qil[...], kit2[c], preferred_element_type=f32)
        acc = None
        for h in range(N_IDX_HEADS):
            term = jnp.maximum(s[h * Q_ROWS:(h + 1) * Q_ROWS], 0.0) * w_head[h]
            acc = term if acc is None else acc + term
        col = c * ck + lax.broadcasted_iota(i32, (Q_ROWS, ck), 1)
        scores[c] = jnp.where(col <= row_id, acc, -jnp.inf)
        return carry
    lax.fori_loop(0, nch, score_chunk, 0)

    load = lambda c: scores[c]

    def store(c, val):
        scores[c] = val
    t, c_ge = _kth_largest_score(load, nch, Q_ROWS, ksel, (i + 1) * Q_ROWS > ksel)
    _demote_excess_ties(load, store, nch, Q_ROWS, ksel, t, c_ge, (seq - 1).bit_length())

    def masked_logits(c, p, bias2):
        return jnp.dot(ql[p], kt2[c, p], preferred_element_type=f32) + bias2

    def bias_of(c):
        bias = jnp.where(scores[c] >= t, 0.0, NEG_BIG)
        return jnp.concatenate([bias, bias], axis=0)

    mx_s[...] = jnp.full(mx_s.shape, NEG_BIG, f32)

    def max_chunk(c, carry):
        bias2 = bias_of(c)
        for p in range(N_KV_HEADS):
            mx_s[p] = jnp.maximum(mx_s[p], masked_logits(c, p, bias2))
        return carry
    lax.fori_loop(0, nch, max_chunk, 0)

    for p in range(N_KV_HEADS):
        m_s[p] = jnp.broadcast_to(jnp.max(mx_s[p], axis=-1, keepdims=True), m_s.shape[1:])
    mx_s[...] = jnp.zeros(mx_s.shape, f32)
    acc_s[...] = jnp.zeros(acc_s.shape, f32)

    def attend_chunk(c, carry):
        bias2 = bias_of(c)
        for p in range(N_KV_HEADS):
            m = m_s[p]
            pr = jnp.exp(masked_logits(c, p, bias2) - jnp.concatenate([m, m], axis=1))
            mx_s[p] = mx_s[p] + pr
            acc_s[p] = acc_s[p] + jnp.dot(
                pr.astype(bf16), vb[c, :, (p // 2) * LANES:(p // 2 + 1) * LANES],
                preferred_element_type=f32)
        return carry
    lax.fori_loop(0, nch, attend_chunk, 0)

    for p in range(N_KV_HEADS):
        o = acc_s[p] / jnp.sum(mx_s[p], axis=-1, keepdims=True)
        a, b = o[0:Q_ROWS], o[Q_ROWS:2 * Q_ROWS]
        if p % 2 == 0:
            b = pltpu.roll(b, half, 1)
        else:
            a = pltpu.roll(a, half, 1)
        o_ref[:, p * LANES:(p + 1) * LANES] = jnp.where(low, a, b).astype(bf16)


def _prompt_attention(qi, q, kw, k, v, batch, seq, ksel):
    nqb = seq // Q_ROWS
    nck = seq // KEY_CHUNK
    qrow = lambda b, i: (b * nqb + i, 0)
    whole = lambda b, i: (b, 0)
    return pl.pallas_call(
        functools.partial(_prompt_attn_kernel, ksel=ksel),
        grid=(batch, nqb),
        in_specs=[pl.BlockSpec((Q_ROWS, 512), qrow),
                  pl.BlockSpec((Q_ROWS, 512), qrow),
                  pl.BlockSpec((Q_ROWS, LANES), qrow),
                  pl.BlockSpec((seq, LANES), whole),
                  pl.BlockSpec((seq, KV_WIDTH), whole),
                  pl.BlockSpec((seq, KV_WIDTH), whole)],
        out_specs=pl.BlockSpec((Q_ROWS, ATTN_WIDTH), qrow),
        out_shape=jax.ShapeDtypeStruct((batch * seq, ATTN_WIDTH), bf16),
        scratch_shapes=[pltpu.VMEM((nck, LANES, KEY_CHUNK), bf16),
                        pltpu.VMEM((nck, N_KV_HEADS, LANES, KEY_CHUNK), bf16),
                        pltpu.VMEM((nck, KEY_CHUNK, KV_WIDTH), bf16),
                        pltpu.VMEM((nck, Q_ROWS, KEY_CHUNK), f32),
                        pltpu.VMEM((N_IDX_HEADS * Q_ROWS, LANES), bf16),
                        pltpu.VMEM((N_KV_HEADS, 2 * Q_ROWS, LANES), bf16),
                        pltpu.VMEM((N_KV_HEADS, 2 * Q_ROWS, LANES), f32),
                        pltpu.VMEM((N_KV_HEADS, 2 * Q_ROWS, KEY_CHUNK), f32),
                        pltpu.VMEM((N_KV_HEADS, 2 * Q_ROWS, LANES), f32)],
        compiler_params=_params("arbitrary", "arbitrary"),
        name="prompt_attention",
    )(qi, q, kw, kw, k, v)


PAGES_PER_STEP = 8
PAGE = 128
SAMPLE_Q = 8
SAMPLE_ROWS = N_HEADS * SAMPLE_Q


def _sample_score_kernel(pt_ref, qi_ref, w_ref, kin_ref, *refs):
    pages = refs[:PAGES_PER_STEP]
    out_ref = refs[PAGES_PER_STEP]
    j = pl.program_id(1)
    last = pl.num_programs(1) - 1
    qi = qi_ref[...]
    w = w_ref[...]

    def score(kit):
        s = jnp.dot(qi, kit.astype(bf16), preferred_element_type=f32)
        s = jnp.maximum(s, 0.0) * w
        return jnp.sum(s.reshape(N_IDX_HEADS, SAMPLE_Q, s.shape[-1]), axis=0)

    @pl.when(j < last)
    def _():
        out_ref[...] = score(jnp.concatenate([r[...] for r in pages], axis=1))

    @pl.when(j == last)
    def _():
        sc = score(kin_ref[...])
        col = lax.broadcasted_iota(i32, sc.shape, 1)
        row = lax.broadcasted_iota(i32, sc.shape, 0)
        out_ref[...] = jnp.full(out_ref.shape, -jnp.inf, f32)
        out_ref[:, 0:PAGE] = jnp.where(col <= row, sc, -jnp.inf)


def _page_spec(block, layer, n, n_pages):
    def index(b, j, pt):
        page = pt[b, jnp.minimum(j * PAGES_PER_STEP + n, n_pages - 1)]
        return (layer, page) + (0,) * (len(block) - 2)
    return pl.BlockSpec(block, index)


def _sample_scores(page_table, qi_s, w_s, ki_new_t, idx_cache_t, layer):
    db, n_pages = page_table.shape
    steps = n_pages // PAGES_PER_STEP + 1
    per_b = lambda b, j, pt: (b, 0, 0)
    step_w = PAGES_PER_STEP * PAGE
    return pl.pallas_call(
        _sample_score_kernel,
        grid_spec=pltpu.PrefetchScalarGridSpec(
            num_scalar_prefetch=1,
            grid=(db, steps),
            in_specs=[pl.BlockSpec((None, SAMPLE_ROWS, IDX_DIM), per_b),
                      pl.BlockSpec((None, SAMPLE_ROWS, 1), per_b),
                      pl.BlockSpec((None, IDX_DIM, PAGE), per_b)]
                     + [_page_spec((None, None, IDX_DIM, PAGE), layer, n, n_pages)
                        for n in range(PAGES_PER_STEP)],
            out_specs=pl.BlockSpec((None, SAMPLE_Q, step_w), lambda b, j, pt: (b, 0, j)),
        ),
        out_shape=jax.ShapeDtypeStruct((db, SAMPLE_Q, steps * step_w), f32),
        compiler_params=_params("arbitrary", "arbitrary"),
        name="sample_scores",
    )(page_table, qi_s, w_s, ki_new_t, *([idx_cache_t] * PAGES_PER_STEP))


def _sample_threshold_kernel(keys_in, keys_out, t_out, *, ksel):
    rows, width = keys_in.shape
    nch = width // KEY_CHUNK
    keys_out[...] = keys_in[...]
    load = lambda c: keys_out[:, c * KEY_CHUNK:(c + 1) * KEY_CHUNK]

    def store(c, val):
        keys_out[:, c * KEY_CHUNK:(c + 1) * KEY_CHUNK] = val
    t, c_ge = _kth_largest_score(load, nch, rows, ksel, True)
    _demote_excess_ties(load, store, nch, rows, ksel, t, c_ge, (width - 1).bit_length())
    t_out[...] = jnp.broadcast_to(t, t_out.shape)


def _sample_threshold(keys2d, ksel):
    n, width = keys2d.shape
    rows = 128
    return pl.pallas_call(
        functools.partial(_sample_threshold_kernel, ksel=ksel),
        grid=(n // rows,),
        in_specs=[pl.BlockSpec((rows, width), lambda i: (i, 0))],
        out_specs=[pl.BlockSpec((rows, width), lambda i: (i, 0)),
                   pl.BlockSpec((rows, LANES), lambda i: (i, 0))],
        out_shape=[jax.ShapeDtypeStruct((n, width), f32),
                   jax.ShapeDtypeStruct((n, LANES), f32)],
        compiler_params=_params("arbitrary"),
        name="sample_threshold",
    )(keys2d)


def _sample_attn_kernel(pt_ref, q_ref, sc_ref, t_ref, kn_ref, vn_ref, *refs):
    n = PAGES_PER_STEP
    k_pages, v_pages = refs[:n], refs[n:2 * n]
    o_ref, m_s, l_s, acc_s = refs[2 * n:]
    j = pl.program_id(1)
    last = pl.num_programs(1) - 1
    group_rows = 2 * SAMPLE_Q

    @pl.when(j == 0)
    def _():
        m_s[...] = jnp.full(m_s.shape, NEG_BIG, f32)
        l_s[...] = jnp.zeros(l_s.shape, f32)
        acc_s[...] = jnp.zeros(acc_s.shape, f32)

    t = t_ref[:, 0:1]

    def update(kt, vt, sc):
        bias = jnp.where(sc >= t, 0.0, NEG_BIG)
        bias = jnp.concatenate([bias, bias], axis=0)
        lg = jnp.concatenate(
            [jnp.dot(q_ref[h * group_rows:(h + 1) * group_rows, :], kt[h],
                     preferred_element_type=f32) + bias for h in range(N_KV_HEADS)], axis=0)
        m_old = m_s[...]
        m_new = jnp.maximum(m_old, jnp.max(lg, axis=-1, keepdims=True))
        alpha = jnp.exp(m_old - m_new)
        pr = jnp.exp(lg - m_new[:, 0:1])
        l_s[...] = alpha * l_s[...] + jnp.sum(pr, axis=-1, keepdims=True)
        pr = pr.astype(bf16)
        pv = jnp.concatenate(
            [lax.dot_general(pr[h * group_rows:(h + 1) * group_rows], vt[h],
                             (((1,), (1,)), ((), ())), preferred_element_type=f32)
             for h in range(N_KV_HEADS)], axis=0)
        acc_s[...] = acc_s[...] * alpha[:, 0:HEAD_DIM] + pv
        m_s[...] = m_new

    def lanes_of(page_refs):
        return jnp.concatenate([r[...] for r in page_refs], axis=2).astype(bf16)

    @pl.when(j < last)
    def _():
        update(lanes_of(k_pages), lanes_of(v_pages), sc_ref[...])

    @pl.when(j == last)
    def _():
        update(kn_ref[...], vn_ref[...], sc_ref[:, 0:PAGE])
        o = acc_s[...] / l_s[:, 0:HEAD_DIM]
        for h in range(N_HEADS):
            o_ref[:, h * HEAD_DIM:(h + 1) * HEAD_DIM] = o[h * SAMPLE_Q:(h + 1) * SAMPLE_Q]


def _sample_attention(page_table, q_s, scores3d, t3d, k_new_t, v_new_t, k_cache_t, v_cache_t,
                      layer):
    db, n_pages = page_table.shape
    steps = n_pages // PAGES_PER_STEP + 1
    per_b = lambda b, j, pt: (b, 0, 0)
    per_b4 = lambda b, j, pt: (b, 0, 0, 0)
    step_w = PAGES_PER_STEP * PAGE
    page_block = (None, None, N_KV_HEADS, HEAD_DIM, PAGE)
    return pl.pallas_call(
        _sample_attn_kernel,
        grid_spec=pltpu.PrefetchScalarGridSpec(
            num_scalar_prefetch=1,
            grid=(db, steps),
            in_specs=[pl.BlockSpec((None, SAMPLE_ROWS, HEAD_DIM), per_b),
                      pl.BlockSpec((None, SAMPLE_Q, step_w), lambda b, j, pt: (b, 0, j)),
                      pl.BlockSpec((None, SAMPLE_Q, LANES), per_b),
                      pl.BlockSpec((None, N_KV_HEADS, HEAD_DIM, PAGE), per_b4),
                      pl.BlockSpec((None, N_KV_HEADS, HEAD_DIM, PAGE), per_b4)]
                     + [_page_spec(page_block, layer, n, n_pages)
                        for n in range(PAGES_PER_STEP)] * 2,
            out_specs=pl.BlockSpec((None, SAMPLE_Q, ATTN_WIDTH), per_b),
            scratch_shapes=[pltpu.VMEM((SAMPLE_ROWS, LANES), f32),
                            pltpu.VMEM((SAMPLE_ROWS, LANES), f32),
                            pltpu.VMEM((SAMPLE_ROWS, HEAD_DIM), f32)],
        ),
        out_shape=jax.ShapeDtypeStruct((db, SAMPLE_Q, ATTN_WIDTH), f32),
        compiler_params=_params("arbitrary", "arbitrary"),
        name="sample_attention",
    )(page_table, q_s, scores3d, t3d, k_new_t, v_new_t,
      *([k_cache_t] * PAGES_PER_STEP), *([v_cache_t] * PAGES_PER_STEP))


def _pool_diff_prompt(u_ref, hist_ref, seq):
    tm = u_ref.shape[0]
    start = (pl.program_id(0) * tm) % seq
    u = u_ref[...]
    hist = hist_ref[...] * jnp.where(start == 0, 0.0, 1.0)
    ext = jnp.concatenate([hist, u], axis=0)
    pos = start + lax.broadcasted_iota(i32, (tm, 1), 0)
    outs = []
    for g, win in enumerate(POOL_WINDOWS):
        s = ext[:, g * POOL_GROUP:(g + 1) * POOL_GROUP]
        span = 1
        while span < win:
            s = s + pltpu.roll(s, span, 0)
            span *= 2
        cnt = jnp.minimum(pos + 1, win).astype(f32)
        outs.append(s[HIST_ROWS:] / cnt - u[:, g * POOL_GROUP:(g + 1) * POOL_GROUP])
    return outs


def _pool_diff_sample(u_ref):
    nb = u_ref.shape[0]
    outs = []
    for g, win in enumerate(POOL_WINDOWS):
        lanes = slice(g * POOL_GROUP, (g + 1) * POOL_GROUP)
        cur = u_ref[:, HIST_ROWS:HIST_ROWS + 8, lanes]
        s = cur
        for d in range(1, win):
            s = s + u_ref[:, HIST_ROWS - d:HIST_ROWS + 8 - d, lanes]
        outs.append((s / float(win) - cur).reshape(nb * 8, POOL_GROUP))
    return outs


def _merge_kernel(x_ref, ao_ref, u_ref, hist_ref, ga_ref, gb_ref, wpool_ref, pscale_ref,
                  wba_ref, wbp_ref, wout_ref, fg_ref, wr_ref, br_ref,
                  h_ref, hn_ref, route_ref, *, seq):
    diffs = _pool_diff_sample(u_ref) if seq is None else _pool_diff_prompt(u_ref, hist_ref, seq)
    pool = jnp.concatenate(
        [jnp.dot(d.astype(bf16), wpool_ref[g], preferred_element_type=f32)
         for g, d in enumerate(diffs)], axis=1) * pscale_ref[...]
    br_a = jnp.dot(ao_ref[...].astype(bf16), wba_ref[...], preferred_element_type=f32)
    br_p = jnp.dot(pool.astype(bf16), wbp_ref[...], preferred_element_type=f32)
    sig = lambda z: 1.0 / (1.0 + jnp.exp(-z))
    mix = sig(ga_ref[...]) * br_a + sig(gb_ref[...]) * br_p
    h = x_ref[...] + jnp.dot(mix.astype(bf16), wout_ref[...], preferred_element_type=f32)
    h_ref[...] = h
    hn = _rms(h, fg_ref[...])
    hn_ref[...] = hn

    logit = jnp.dot(hn, wr_ref[...], preferred_element_type=f32,
                    precision=lax.Precision.HIGHEST) + br_ref[...]
    lane_i = lax.broadcasted_iota(i32, logit.shape, 1)
    lane = lane_i.astype(f32)
    big = float(LANES)
    gl = jnp.where(lane_i < N_EXPERTS, -jnp.inf,
                   jnp.where(lane_i < N_EXPERTS + N_GROUPS, logit, -jnp.inf))
    gmax = jnp.max(gl, axis=-1, keepdims=True)
    g_sel = jnp.min(jnp.where(gl == gmax, lane, big), axis=-1, keepdims=True) - N_EXPERTS
    g_w = 1.0 / jnp.sum(jnp.exp(gl - gmax), axis=-1, keepdims=True)
    group_of_lane = (lane_i // EXPERTS_PER_GROUP).astype(f32)
    el = jnp.where(group_of_lane == g_sel, logit, -jnp.inf)
    v1 = jnp.max(el, axis=-1, keepdims=True)
    i1 = jnp.min(jnp.where(el == v1, lane, big), axis=-1, keepdims=True)
    el2 = jnp.where(lane == i1, -jnp.inf, el)
    v2 = jnp.max(el2, axis=-1, keepdims=True)
    i2 = jnp.min(jnp.where(el2 == v2, lane, big), axis=-1, keepdims=True)
    e21 = jnp.exp(v2 - v1)
    w1 = g_w / (1.0 + e21)
    w2 = g_w * e21 / (1.0 + e21)
    route_ref[...] = jnp.where(lane_i == 0, i1,
                     jnp.where(lane_i == 1, i2,
                     jnp.where(lane_i == 2, w1, jnp.where(lane_i == 3, w2, 0.0))))


def _merge(x2d, ao, u, ga, gb, lw, seq):
    n = x2d.shape[0]
    tm = PROJ_ROWS
    row = lambda i: (i, 0)
    const2 = lambda i: (0, 0)
    const3 = lambda i: (0, 0, 0)
    if seq is None:
        u_spec = pl.BlockSpec((tm // 8, HIST_ROWS + 8, POOL_WIDTH), lambda i: (i, 0, 0))
        hist = jnp.zeros((HIST_ROWS, POOL_WIDTH), f32)
        hist_spec = pl.BlockSpec((HIST_ROWS, POOL_WIDTH), const2)
    else:
        u_spec = pl.BlockSpec((tm, POOL_WIDTH), row)
        hist = u
        hist_spec = pl.BlockSpec((HIST_ROWS, POOL_WIDTH),
                                 lambda i: (jnp.maximum(i * (tm // HIST_ROWS) - 1, 0), 0))
    return pl.pallas_call(
        functools.partial(_merge_kernel, seq=seq),
        grid=(n // tm,),
        in_specs=[pl.BlockSpec((tm, D_MODEL), row),
                  pl.BlockSpec((tm, ATTN_WIDTH), row),
                  u_spec, hist_spec,
                  pl.BlockSpec((tm, D_MODEL), row),
                  pl.BlockSpec((tm, D_MODEL), row),
                  pl.BlockSpec((4, POOL_GROUP, POOL_GROUP), const3),
                  pl.BlockSpec((1, POOL_WIDTH), const2),
                  pl.BlockSpec((ATTN_WIDTH, D_MODEL), const2),
                  pl.BlockSpec((POOL_WIDTH, D_MODEL), const2),
                  pl.BlockSpec((D_MODEL, D_MODEL), const2),
                  pl.BlockSpec((1, D_MODEL), const2),
                  pl.BlockSpec((D_MODEL, LANES), const2),
                  pl.BlockSpec((1, LANES), const2)],
        out_specs=[pl.BlockSpec((tm, D_MODEL), row),
                   pl.BlockSpec((tm, D_MODEL), row),
                   pl.BlockSpec((tm, LANES), row)],
        out_shape=[jax.ShapeDtypeStruct((n, D_MODEL), f32),
                   jax.ShapeDtypeStruct((n, D_MODEL), f32),
                   jax.ShapeDtypeStruct((n, LANES), f32)],
        compiler_params=_params("arbitrary"),
        name="merge",
    )(x2d, ao, u, hist, ga, gb, lw['w_pool'], lw['pool_scale'], lw['w_ba'], lw['w_bp'],
      lw['w_out'], lw['ffn_g'], lw['w_router'], lw['b_router'])


RANK_ROWS = 512


def _rank_kernel(route_ref, rank_ref, counts_ref, base_s):
    @pl.when(pl.program_id(0) == 0)
    def _():
        base_s[...] = jnp.zeros(base_s.shape, f32)
    r = route_ref[...]
    tt = r.shape[0]
    lane = lax.broadcasted_iota(i32, r.shape, 1).astype(f32)
    oh1 = lane == r[:, 0:1]
    oh2 = lane == r[:, 1:2]
    oh = jnp.where(oh1, 1.0, 0.0) + jnp.where(oh2, 1.0, 0.0)
    below = (lax.broadcasted_iota(i32, (tt, tt), 0) > lax.broadcasted_iota(i32, (tt, tt), 1))
    before = jnp.dot(jnp.where(below, 1.0, 0.0).astype(bf16), oh.astype(bf16),
                     preferred_element_type=f32) + base_s[...]
    r1 = jnp.sum(jnp.where(oh1, before, 0.0), axis=-1, keepdims=True)
    r2 = jnp.sum(jnp.where(oh2, before, 0.0), axis=-1, keepdims=True)
    lane_i = lax.broadcasted_iota(i32, r.shape, 1)
    rank_ref[...] = jnp.where(lane_i == 0, r1, jnp.where(lane_i == 1, r2, 0.0))
    base_s[...] = base_s[...] + jnp.sum(oh, axis=0, keepdims=True)
    counts_ref[...] = base_s[...]


def _expert_ranks(route):
    n = route.shape[0]
    tt = min(RANK_ROWS, n)
    return pl.pallas_call(
        _rank_kernel,
        grid=(n // tt,),
        in_specs=[pl.BlockSpec((tt, LANES), lambda i: (i, 0))],
        out_specs=[pl.BlockSpec((tt, LANES), lambda i: (i, 0)),
                   pl.BlockSpec((1, LANES), lambda i: (0, 0))],
        out_shape=[jax.ShapeDtypeStruct((n, LANES), f32),
                   jax.ShapeDtypeStruct((1, LANES), f32)],
        scratch_shapes=[pltpu.VMEM((1, LANES), f32)],
        compiler_params=_params("arbitrary"),
        name="expert_ranks",
    )(route)


INVERT_ROWS = 8192


def _slot_tokens_kernel(dest_ref, tok_ref):
    i = pl.program_id(0)
    tile = dest_ref.shape[0]

    @pl.when(i == 0)
    def _():
        def zero(s, carry):
            tok_ref[s] = 0
            return carry
        lax.fori_loop(0, tok_ref.shape[0], zero, 0, unroll=16)

    def body(a, carry):
        tok_ref[dest_ref[a]] = (i * tile + a) // TOP_K
        return carry
    lax.fori_loop(0, tile, body, 0, unroll=16)


def _slot_tokens(dest, n_slots):
    n = dest.shape[0]
    tile = min(INVERT_ROWS, n)
    return pl.pallas_call(
        _slot_tokens_kernel,
        grid=(n // tile,),
        in_specs=[pl.BlockSpec((tile,), lambda i: (i,), memory_space=pltpu.SMEM)],
        out_specs=pl.BlockSpec((n_slots,), lambda i: (0,), memory_space=pltpu.SMEM),
        out_shape=jax.ShapeDtypeStruct((n_slots,), i32),
        compiler_params=_params("arbitrary"),
        name="moe_slot_tokens",
    )(dest)


def _expert_ffn_kernel(be_ref, tok_ref, tok_next_ref, x_hbm, wg_ref, wu_ref, wd_ref,
                       ys_ref, xbuf, sem):
    i = pl.program_id(0)
    last = pl.num_programs(0) - 1
    bm = ys_ref.shape[0]
    slot = i % 2

    def row_copy(tok, r, s):
        return pltpu.make_async_copy(x_hbm.at[pl.ds(tok[0, r], 1)], xbuf.at[s, pl.ds(r, 1)],
                                     sem.at[s])

    def wait_block(s):
        pltpu.make_async_copy(x_hbm.at[pl.ds(0, bm)], xbuf.at[s], sem.at[s]).wait()

    @pl.when(i == 0)
    def _():
        def body(r, carry):
            row_copy(tok_ref, r, 0).start()
            return carry
        lax.fori_loop(0, bm, body, 0, unroll=8)

    wait_block(slot)
    for r in range(bm):
        row_copy(tok_next_ref, r, 1 - slot).start()
    x = xbuf[slot].astype(bf16)
    g = jnp.dot(x, wg_ref[...], preferred_element_type=f32)
    u = jnp.dot(x, wu_ref[...], preferred_element_type=f32)
    hb = (g / (1.0 + jnp.exp(-g))) * u
    ys_ref[...] = jnp.dot(hb.astype(bf16), wd_ref[...], preferred_element_type=f32)

    @pl.when(i == last)
    def _():
        wait_block(1 - slot)


def _expert_ffn(block_expert, slot_tok, x, wg, wu, wd, bm):
    n_blocks = slot_tok.shape[0] // bm
    tok3 = slot_tok.reshape(n_blocks, 1, bm)
    by_expert = lambda i, be: (be[i], 0, 0)
    smem_block = lambda index: pl.BlockSpec((None, 1, bm), index, memory_space=pltpu.SMEM)
    return pl.pallas_call(
        _expert_ffn_kernel,
        grid_spec=pltpu.PrefetchScalarGridSpec(
            num_scalar_prefetch=1,
            grid=(n_blocks,),
            in_specs=[smem_block(lambda i, be: (i, 0, 0)),
                      smem_block(lambda i, be: (jnp.minimum(i + 1, n_blocks - 1), 0, 0)),
                      pl.BlockSpec(memory_space=pl.ANY),
                      pl.BlockSpec((None, D_MODEL, D_EXPERT), by_expert),
                      pl.BlockSpec((None, D_MODEL, D_EXPERT), by_expert),
                      pl.BlockSpec((None, D_EXPERT, D_MODEL), by_expert)],
            out_specs=pl.BlockSpec((bm, D_MODEL), lambda i, be: (i, 0)),
            scratch_shapes=[pltpu.VMEM((2, bm, D_MODEL), f32),
                            pltpu.SemaphoreType.DMA((2,))],
        ),
        out_shape=jax.ShapeDtypeStruct((n_blocks * bm, D_MODEL), f32),
        compiler_params=_params("arbitrary"),
        name="expert_ffn",
    )(block_expert, tok3, tok3, x, wg, wu, wd)


COMBINE_ROWS = 128


def _combine_kernel(dest_ref, dest_next_ref, h_ref, route_ref, g_ref, ys_hbm, y_ref, buf, sem):
    i = pl.program_id(0)
    tt = h_ref.shape[0]
    slot = i % 2

    def gather(dest, s):
        def body(t, carry):
            for j in range(TOP_K):
                pltpu.make_async_copy(ys_hbm.at[pl.ds(dest[0, TOP_K * t + j], 1)],
                                      buf.at[s, j, pl.ds(t, 1)], sem.at[s]).start()
            return carry
        lax.fori_loop(0, tt, body, 0, unroll=4)

    @pl.when(i == 0)
    def _():
        gather(dest_ref, 0)

    @pl.when(i + 1 < pl.num_programs(0))
    def _():
        gather(dest_next_ref, 1 - slot)

    for j in range(TOP_K):
        pltpu.make_async_copy(ys_hbm.at[pl.ds(0, tt)], buf.at[slot, j], sem.at[slot]).wait()
    r = route_ref[...]
    moe = buf[slot, 0] * r[:, 2:3] + buf[slot, 1] * r[:, 3:4]
    y_ref[...] = _rms(h_ref[...] + moe, g_ref[...])


def _combine(dest, h, route, final_g, ys):
    n = h.shape[0]
    tt = COMBINE_ROWS
    n_tiles = n // tt
    dest3 = dest.reshape(n_tiles, 1, tt * TOP_K)
    row = lambda i: (i, 0)
    smem_block = lambda index: pl.BlockSpec((None, 1, tt * TOP_K), index, memory_space=pltpu.SMEM)
    return pl.pallas_call(
        _combine_kernel,
        grid=(n_tiles,),
        in_specs=[smem_block(lambda i: (i, 0, 0)),
                  smem_block(lambda i: (jnp.minimum(i + 1, n_tiles - 1), 0, 0)),
                  pl.BlockSpec((tt, D_MODEL), row),
                  pl.BlockSpec((tt, LANES), row),
                  pl.BlockSpec((1, D_MODEL), lambda i: (0, 0)),
                  pl.BlockSpec(memory_space=pl.ANY)],
        out_specs=pl.BlockSpec((tt, D_MODEL), row),
        out_shape=jax.ShapeDtypeStruct((n, D_MODEL), f32),
        scratch_shapes=[pltpu.VMEM((2, TOP_K, tt, D_MODEL), f32),
                        pltpu.SemaphoreType.DMA((2,))],
        compiler_params=_params("arbitrary"),
        name="moe_combine",
    )(dest3, dest3, h, route, final_g, ys)


def _moe_and_final_norm(h, hn, route, lw, final_g, bm):
    n = h.shape[0]
    rank, counts = _expert_ranks(route)
    counts = counts[0, :N_EXPERTS].astype(i32)
    padded = (counts + bm - 1) // bm * bm
    pend = jnp.cumsum(padded)
    pstart = pend - padded
    e_id = route[:, 0:TOP_K].astype(i32)
    dest = (pstart[e_id] + rank[:, 0:TOP_K].astype(i32)).reshape(-1)
    n_blocks = (n * TOP_K + N_EXPERTS * (bm - 1)) // bm + 1
    n_active = (pend[-1] // bm).astype(i32)
    blk = jnp.minimum(jnp.arange(n_blocks, dtype=i32), n_active - 1) * bm
    block_expert = jnp.minimum(jnp.sum((pend[None, :] <= blk[:, None]).astype(i32), axis=1),
                               N_EXPERTS - 1)
    slot_tok = _slot_tokens(dest, n_blocks * bm)
    ys = _expert_ffn(block_expert, slot_tok, hn, lw['w_eg'], lw['w_eu'], lw['w_ed'], bm)
    return _combine(dest, h, route, final_g, ys)


def _rope_tables(pos, reps):
    half = HEAD_DIM // 2
    inv_freq = 1.0 / (ROPE_THETA ** (jnp.arange(half, dtype=f32) / half))
    ang = pos.astype(f32)[:, None] * inv_freq[None, :]
    cos, sin = jnp.cos(ang), jnp.sin(ang)
    cos = jnp.concatenate([cos, cos, cos, cos], axis=1)
    sin = jnp.concatenate([-sin, sin, -sin, sin], axis=1)
    return jnp.tile(cos, (reps, 1)), jnp.tile(sin, (reps, 1))


def _layer_weights(l, attn_norm_g, w_in, w_pool, pool_scale, w_branch_attn, w_branch_pool, w_out,
                   ffn_norm_g, w_rg, b_rg, w_re, b_re, w_eg, w_eu, w_ed):
    cuts = [0]
    for s in SEGMENTS:
        cuts.append(cuts[-1] + s)
    w = w_in[l]
    kw_pad = LANES - IDX_DIM - N_IDX_HEADS
    w_packed = jnp.concatenate(
        [w[:, cuts[0]:cuts[4]], w[:, cuts[4]:cuts[6]], jnp.zeros((D_MODEL, kw_pad), f32),
         w[:, cuts[6]:]], axis=1).astype(bf16)
    r_pad = LANES - N_EXPERTS - N_GROUPS
    return dict(
        attn_g=attn_norm_g[l].reshape(1, D_MODEL), w_packed=w_packed,
        w_pool=w_pool[l].astype(bf16), pool_scale=pool_scale[l].reshape(1, POOL_WIDTH),
        w_ba=w_branch_attn[l].astype(bf16), w_bp=w_branch_pool[l].astype(bf16),
        w_out=w_out[l].astype(bf16), ffn_g=ffn_norm_g[l].reshape(1, D_MODEL),
        w_router=jnp.concatenate([w_re[l], w_rg[l], jnp.zeros((D_MODEL, r_pad), f32)], axis=1),
        b_router=jnp.concatenate([b_re[l], b_rg[l], jnp.zeros((r_pad,), f32)]).reshape(1, LANES),
        w_eg=w_eg[l].astype(bf16), w_eu=w_eu[l].astype(bf16), w_ed=w_ed[l].astype(bf16))


def _prompt_layer(x, lw, final_g):
    batch, seq, _ = x.shape
    x2d = x.reshape(batch * seq, D_MODEL)
    cos, sin = _rope_tables(jnp.arange(seq), 1)
    q, k, v, qi, kw, u, ga, gb = _project(x2d, lw['attn_g'], lw['w_packed'], cos, sin)
    ao = _prompt_attention(qi, q, kw, k, v, batch, seq, min(TOPK_MAX, seq // 4))
    h, hn, route = _merge(x2d, ao, u, ga, gb, lw, seq)
    y = _moe_and_final_norm(h, hn, route, lw, final_g, bm=256)
    return (y.reshape(batch, seq, D_MODEL),
            k.reshape(batch, seq, N_KV_HEADS, HEAD_DIM), v.reshape(batch, seq, N_KV_HEADS, HEAD_DIM),
            kw[:, :IDX_DIM].reshape(batch, seq, IDX_DIM),
            u.reshape(batch, seq, POOL_WIDTH)[:, seq - POOL_HIST:])


def _sample_layer(x, l, cache_k, cache_v, cache_idx_k, state_pool, page_table, lw, final_g):
    db, ds, _ = x.shape
    page = cache_k.shape[2]
    past = page_table.shape[1] * page
    x2d = x.reshape(db * ds, D_MODEL)
    cos, sin = _rope_tables(past + jnp.arange(ds), PROJ_ROWS // ds)
    q, k, v, qi, kw, u, ga, gb = _project(x2d, lw['attn_g'], lw['w_packed'], cos, sin)

    rows_hq = lambda a, nh, d: a.reshape(db, ds, nh, d).transpose(0, 2, 1, 3).reshape(db, nh * ds, d)
    qi_s = rows_hq(qi, N_IDX_HEADS, IDX_DIM)
    w_s = kw[:, IDX_DIM:IDX_DIM + N_IDX_HEADS].reshape(db, ds, N_IDX_HEADS)
    w_s = w_s.transpose(0, 2, 1).reshape(db, N_IDX_HEADS * ds, 1)
    pad_slots = lambda a: jnp.pad(a, [(0, 0)] * (a.ndim - 1) + [(0, page - ds)])
    ki_new_t = pad_slots(kw[:, :IDX_DIM].reshape(db, ds, IDX_DIM).transpose(0, 2, 1))
    scores = _sample_scores(page_table, qi_s, w_s, ki_new_t,
                            cache_idx_k.transpose(0, 1, 3, 2), l)
    width = scores.shape[-1]
    scores, t = _sample_threshold(scores.reshape(db * ds, width),
                                  min(TOPK_MAX, (past + ds) // 4))

    as_page = lambda a: pad_slots(
        a.reshape(db, ds, N_KV_HEADS, HEAD_DIM).transpose(0, 2, 3, 1)).astype(bf16)
    ao = _sample_attention(page_table, rows_hq(q, N_HEADS, HEAD_DIM),
                           scores.reshape(db, ds, width), t.reshape(db, ds, LANES),
                           as_page(k), as_page(v),
                           cache_k.transpose(0, 1, 3, 4, 2), cache_v.transpose(0, 1, 3, 4, 2), l)

    u3 = u.reshape(db, ds, POOL_WIDTH)
    u_all = jnp.concatenate([jnp.zeros((db, 1, POOL_WIDTH), f32), state_pool[l], u3], axis=1)
    h, hn, route = _merge(x2d, ao.reshape(db * ds, ATTN_WIDTH), u_all, ga, gb, lw, None)
    y = _moe_and_final_norm(h, hn, route, lw, final_g, bm=128)
    return (y.reshape(db, ds, D_MODEL),
            k.reshape(db, ds, N_KV_HEADS, HEAD_DIM), v.reshape(db, ds, N_KV_HEADS, HEAD_DIM),
            kw[:, :IDX_DIM].reshape(db, ds, IDX_DIM),
            u_all[:, -POOL_HIST:])


def kernel(x_prompt, x_sample, cache_k, cache_v, cache_idx_k, state_pool, page_table, attn_norm_g, w_in, w_pool, pool_scale, w_branch_attn, w_branch_pool, w_out, ffn_norm_g, w_router_group, b_router_group, w_router_expert, b_router_expert, w_exp_gate, w_exp_up, w_exp_down, final_norm_g):
    depth = w_in.shape[0]
    assert depth == 1, "the final norm is fused into the (single) layer's MoE combine"
    final_g = final_norm_g.reshape(1, D_MODEL)
    lw = _layer_weights(0, attn_norm_g, w_in, w_pool, pool_scale, w_branch_attn, w_branch_pool,
                        w_out, ffn_norm_g, w_router_group, b_router_group, w_router_expert,
                        b_router_expert, w_exp_gate, w_exp_up, w_exp_down)
    yp, kp, vp, kip, up = _prompt_layer(x_prompt, lw, final_g)
    ys, ks, vs, kis, us = _sample_layer(x_sample, 0, cache_k, cache_v, cache_idx_k, state_pool,
                                        page_table, lw, final_g)
    stack = lambda a: a[None]
    return (yp, ys, stack(kp), stack(vp), stack(kip), stack(up),
            stack(ks), stack(vs), stack(kis), stack(us))
```

```python
import functools

import jax
import jax.numpy as jnp
from jax import lax
from jax.experimental import pallas as pl
from jax.experimental.pallas import tpu as pltpu

D_MODEL = 1024
HEAD_DIM = 64
N_HEADS = 8
N_KV_HEADS = 4
ATTN_WIDTH = N_HEADS * HEAD_DIM
KV_WIDTH = N_KV_HEADS * HEAD_DIM
N_IDX_HEADS = 8
IDX_DIM = 64
TOPK_MAX = 256
ROPE_THETA = 10000.0
POOL_WINDOWS = (2, 4, 8, 16)
POOL_WIDTH = 512
POOL_GROUP = 128
POOL_HIST = 15
N_GROUPS = 4
EXPERTS_PER_GROUP = 8
N_EXPERTS = 32
TOP_K = 2
D_EXPERT = 512
RMS_EPS = 1e-6
SEGMENTS = (ATTN_WIDTH, KV_WIDTH, KV_WIDTH, N_IDX_HEADS * IDX_DIM, IDX_DIM, N_IDX_HEADS,
            POOL_WIDTH, D_MODEL, D_MODEL)

LANES = 128
VMEM_LIMIT = 56 * 1024 * 1024
KEY_CHUNK = 256
Q_ROWS = 128
PROJ_ROWS = 256
HIST_ROWS = 16
NEG_BIG = -1e30
INT_MIN = -2 ** 31
KEY_NEG_INF = -2139095041
FLT_LOWEST = -3.4028234663852886e38
SUM_EXP_FLOOR = 1e-22

_SEG = dict(q=(0, 512), k=(512, 768), v=(768, 1024), qi=(1024, 1536), kw=(1536, 1664),
            u=(1664, 2176), ga=(2176, 3200), gb=(3200, 4224))
PROJ_PACKED = 4224

f32 = jnp.float32
bf16 = jnp.bfloat16
i32 = jnp.int32


def _params(*sem):
    return pltpu.CompilerParams(dimension_semantics=sem, vmem_limit_bytes=VMEM_LIMIT)


def _rms(x, g):
    return x * lax.rsqrt(jnp.mean(x * x, axis=-1, keepdims=True) + RMS_EPS) * g


def _key_to_float(k):
    return lax.bitcast_convert_type(k ^ ((k >> 31) & 0x7FFFFFFF), f32)


def _rope(x, cos, sin_signed):
    lane = lax.broadcasted_iota(i32, (x.shape[0], LANES), 1)
    first_half = (lane % HEAD_DIM) < HEAD_DIM // 2
    outs = []
    for c in range(x.shape[1] // LANES):
        xc = x[:, c * LANES:(c + 1) * LANES]
        partner = jnp.where(first_half, pltpu.roll(xc, LANES - HEAD_DIM // 2, 1),
                            pltpu.roll(xc, HEAD_DIM // 2, 1))
        outs.append(xc * cos + partner * sin_signed)
    return outs[0] if len(outs) == 1 else jnp.concatenate(outs, axis=1)


def _proj_kernel(x_ref, g_ref, w_ref, cos_ref, sin_ref,
                 q_ref, k_ref, v_ref, qi_ref, kw_ref, u_ref, ga_ref, gb_ref):
    xb = _rms(x_ref[...], g_ref[...]).astype(bf16)
    cos = cos_ref[...]
    sin = sin_ref[...]

    def seg(name):
        a, b = _SEG[name]
        return jnp.dot(xb, w_ref[:, a:b], preferred_element_type=f32)

    q_ref[...] = (_rope(seg('q'), cos, sin) * (HEAD_DIM ** -0.5)).astype(bf16)
    k_ref[...] = _rope(seg('k'), cos, sin)
    v_ref[...] = seg('v')
    qi_ref[...] = (_rope(seg('qi'), cos, sin) * (IDX_DIM ** -0.5)).astype(bf16)
    kw = seg('kw')
    lane = lax.broadcasted_iota(i32, kw.shape, 1)
    kw_ref[...] = jnp.where(lane < IDX_DIM, _rope(kw, cos, sin), kw * (N_IDX_HEADS ** -0.5))
    u_ref[...] = seg('u')
    ga_ref[...] = seg('ga')
    gb_ref[...] = seg('gb')


def _project(x2d, g, w_packed, cos_tab, sin_tab):
    n = x2d.shape[0]
    tm = PROJ_ROWS
    ntab = cos_tab.shape[0] // tm
    row = lambda i: (i, 0)
    const = lambda i: (0, 0)
    tab = lambda i: (i % ntab, 0)
    widths = (512, 256, 256, 512, 128, 512, 1024, 1024)
    dtypes = (bf16, f32, f32, bf16, f32, f32, f32, f32)
    return pl.pallas_call(
        _proj_kernel,
        grid=(n // tm,),
        in_specs=[pl.BlockSpec((tm, D_MODEL), row),
                  pl.BlockSpec((1, D_MODEL), const),
                  pl.BlockSpec((D_MODEL, PROJ_PACKED), const),
                  pl.BlockSpec((tm, LANES), tab),
                  pl.BlockSpec((tm, LANES), tab)],
        out_specs=[pl.BlockSpec((tm, w), row) for w in widths],
        out_shape=[jax.ShapeDtypeStruct((n, w), d) for w, d in zip(widths, dtypes)],
        compiler_params=_params("arbitrary"),
        name="project",
    )(x2d, g, w_packed, cos_tab, sin_tab)


def _count(load_chunk, nch, rows, pred):
    def body(c, acc):
        ind = jnp.where(pred(load_chunk(c), c), 1.0, 0.0)
        for s in range(KEY_CHUNK // LANES):
            acc = acc + ind[:, s * LANES:(s + 1) * LANES]
        return acc
    acc = jnp.zeros((rows, LANES), f32)
    if isinstance(nch, int):
        for c in range(nch):
            acc = body(c, acc)
    else:
        acc = lax.fori_loop(0, nch, body, acc)
    return jnp.sum(acc, axis=-1, keepdims=True)


def _kth_largest_score(load_chunk, nch, rows, ksel, search):
    def cond(state):
        it, _, cnt = state
        return jnp.logical_and(it < 32, jnp.max(cnt) > ksel)

    def body(state):
        it, key, cnt = state
        cand = key + lax.shift_left(jnp.int32(1), 31 - it)
        cand_f = _key_to_float(cand)
        c = _count(load_chunk, nch, rows, lambda sc, _: sc >= cand_f)
        take = c >= ksel
        return it + 1, jnp.where(take, cand, key), jnp.where(take, c, cnt)

    total = jnp.where(search, nch * KEY_CHUNK, 0).astype(f32)
    _, key, cnt = lax.while_loop(
        cond, body, (jnp.int32(0), jnp.full((rows, 1), INT_MIN, i32), jnp.full((rows, 1), total)))
    lifted = key <= KEY_NEG_INF
    return (jnp.where(lifted, FLT_LOWEST, _key_to_float(key)), jnp.where(lifted, 0.0, cnt))


def _demote_excess_ties(load_chunk, store_chunk, nch, rows, ksel, t, c_ge, idx_bits):
    @pl.when(jnp.max(c_ge) > ksel)
    def _():
        need = ksel - _count(load_chunk, nch, rows, lambda sc, c: sc > t)

        def col_of(c):
            return c * KEY_CHUNK + lax.broadcasted_iota(i32, (rows, KEY_CHUNK), 1)

        def bit_body(it, p):
            cand = p + lax.shift_left(jnp.int32(1), idx_bits - 1 - it)
            cnt = _count(load_chunk, nch, rows,
                         lambda sc, c: jnp.where(sc == t, col_of(c), cand) < cand)
            return jnp.where(cnt < need, cand, p)
        p = lax.fori_loop(0, idx_bits, bit_body, jnp.zeros((rows, 1), i32))

        def rewrite(c, carry):
            sc = load_chunk(c)
            drop = jnp.where(sc == t, col_of(c), p) > p
            store_chunk(c, jnp.where(drop, -jnp.inf, sc))
            return carry
        if isinstance(nch, int):
            for c in range(nch):
                rewrite(c, 0)
        else:
            lax.fori_loop(0, nch, rewrite, 0)


def _prompt_attn_kernel(qi_ref, q_ref, kwq_ref, kw_ref, k_ref, v_ref, o_ref,
                        kit2, kt2, vb, kn2, scores, qil, ql, m_s, mx_s, acc_s, *, ksel):
    i = pl.program_id(1)
    seq = kw_ref.shape[0]
    ck = KEY_CHUNK
    half = HEAD_DIM

    @pl.when(i == 0)
    def _prepare_keys():
        def body(c, carry):
            rows = pl.ds(pl.multiple_of(c * ck, ck), ck)
            kit = kw_ref[rows, :].T[0:half].astype(bf16)
            kit2[c, 0:half, :] = kit
            kit2[c, half:2 * half, :] = kit
            kc = k_ref[rows, :]
            for pair in range(2):
                kt = kc[:, pair * LANES:(pair + 1) * LANES].T.astype(bf16)
                for sub in range(2):
                    one = kt[sub * half:(sub + 1) * half]
                    kt2[c, 2 * pair + sub, 0:half, :] = one
                    kt2[c, 2 * pair + sub, half:2 * half, :] = one
                    sq = one.astype(f32)
                    sq = jnp.sum(sq * sq, axis=0, keepdims=True)
                    kn2[2 * pair + sub] = jnp.maximum(kn2[2 * pair + sub], sq)
            vb[c] = v_ref[rows, :].astype(bf16)
            return carry
        kn2[...] = jnp.zeros(kn2.shape, f32)
        lax.fori_loop(0, seq // ck, body, 0)

    lane = lax.broadcasted_iota(i32, (Q_ROWS, LANES), 1)
    low = lane < half
    for p in range(4):
        c = qi_ref[:, p * LANES:(p + 1) * LANES].astype(f32)
        qil[(2 * p) * Q_ROWS:(2 * p + 1) * Q_ROWS, :] = jnp.where(low, c, 0.0).astype(bf16)
        qil[(2 * p + 1) * Q_ROWS:(2 * p + 2) * Q_ROWS, :] = jnp.where(low, 0.0, c).astype(bf16)
        c = q_ref[:, p * LANES:(p + 1) * LANES].astype(f32)
        ql[p, 0:Q_ROWS, :] = jnp.where(low, c, 0.0).astype(bf16)
        ql[p, Q_ROWS:2 * Q_ROWS, :] = jnp.where(low, 0.0, c).astype(bf16)
    w_head = [kwq_ref[:, IDX_DIM + h:IDX_DIM + h + 1] for h in range(N_IDX_HEADS)]

    nch = (i + 2) // 2
    row_id = i * Q_ROWS + lax.broadcasted_iota(i32, (Q_ROWS, ck), 0)

    def score_chunk(c, carry):
        s = jnp.dot(qil[...], kit2[c], preferred_element_type=f32)
        acc = None
        for h in range(N_IDX_HEADS):
            term = jnp.maximum(s[h * Q_ROWS:(h + 1) * Q_ROWS], 0.0) * w_head[h]
            acc = term if acc is None else acc + term
        col = c * ck + lax.broadcasted_iota(i32, (Q_ROWS, ck), 1)
        scores[c] = jnp.where(col <= row_id, acc, -jnp.inf)
        return carry
    lax.fori_loop(0, nch, score_chunk, 0)

    load = lambda c: scores[c]

    def store(c, val):
        scores[c] = val
    t, c_ge = _kth_largest_score(load, nch, Q_ROWS, ksel, (i + 1) * Q_ROWS > ksel)
    _demote_excess_ties(load, store, nch, Q_ROWS, ksel, t, c_ge, (seq - 1).bit_length())

    def masked_logits(c, p, bias2):
        return jnp.dot(ql[p], kt2[c, p], preferred_element_type=f32) + bias2

    def bias_of(c):
        bias = jnp.where(scores[c] >= t, 0.0, NEG_BIG)
        return jnp.concatenate([bias, bias], axis=0)

    def attend_sweep():
        mx_s[...] = jnp.zeros(mx_s.shape, f32)
        acc_s[...] = jnp.zeros(acc_s.shape, f32)

        def attend_chunk(c, carry):
            bias2 = bias_of(c)
            for p in range(N_KV_HEADS):
                m = m_s[p]
                pr = jnp.exp(masked_logits(c, p, bias2) - jnp.concatenate([m, m], axis=1))
                mx_s[p] = mx_s[p] + pr
                acc_s[p] = acc_s[p] + jnp.dot(
                    pr.astype(bf16), vb[c, :, (p // 2) * LANES:(p // 2 + 1) * LANES],
                    preferred_element_type=f32)
            return carry
        lax.fori_loop(0, nch, attend_chunk, 0)

    smallest = None
    for p in range(N_KV_HEADS):
        qf = ql[p].astype(f32)
        qn2 = jnp.sum(qf * qf, axis=-1, keepdims=True)
        bound = jnp.sqrt(qn2 * jnp.max(kn2[p], axis=-1, keepdims=True))
        m_s[p] = jnp.broadcast_to(bound, m_s.shape[1:])
    attend_sweep()
    for p in range(N_KV_HEADS):
        low_p = jnp.min(jnp.sum(mx_s[p], axis=-1, keepdims=True))
        smallest = low_p if smallest is None else jnp.minimum(smallest, low_p)

    @pl.when(smallest < SUM_EXP_FLOOR)
    def _exact_maxima():
        mx_s[...] = jnp.full(mx_s.shape, NEG_BIG, f32)

        def max_chunk(c, carry):
            bias2 = bias_of(c)
            for p in range(N_KV_HEADS):
                mx_s[p] = jnp.maximum(mx_s[p], masked_logits(c, p, bias2))
            return carry
        lax.fori_loop(0, nch, max_chunk, 0)
        for p in range(N_KV_HEADS):
            m_s[p] = jnp.broadcast_to(jnp.max(mx_s[p], axis=-1, keepdims=True), m_s.shape[1:])
        attend_sweep()

    for p in range(N_KV_HEADS):
        o = acc_s[p] / jnp.sum(mx_s[p], axis=-1, keepdims=True)
        a, b = o[0:Q_ROWS], o[Q_ROWS:2 * Q_ROWS]
        if p % 2 == 0:
            b = pltpu.roll(b, half, 1)
        else:
            a = pltpu.roll(a, half, 1)
        o_ref[:, p * LANES:(p + 1) * LANES] = jnp.where(low, a, b).astype(bf16)


def _prompt_attention(qi, q, kw, k, v, batch, seq, ksel):
    nqb = seq // Q_ROWS
    nck = seq // KEY_CHUNK
    qrow = lambda b, i: (b * nqb + i, 0)
    whole = lambda b, i: (b, 0)
    return pl.pallas_call(
        functools.partial(_prompt_attn_kernel, ksel=ksel),
        grid=(batch, nqb),
        in_specs=[pl.BlockSpec((Q_ROWS, 512), qrow),
                  pl.BlockSpec((Q_ROWS, 512), qrow),
                  pl.BlockSpec((Q_ROWS, LANES), qrow),
                  pl.BlockSpec((seq, LANES), whole),
                  pl.BlockSpec((seq, KV_WIDTH), whole),
                  pl.BlockSpec((seq, KV_WIDTH), whole)],
        out_specs=pl.BlockSpec((Q_ROWS, ATTN_WIDTH), qrow),
        out_shape=jax.ShapeDtypeStruct((batch * seq, ATTN_WIDTH), bf16),
        scratch_shapes=[pltpu.VMEM((nck, LANES, KEY_CHUNK), bf16),
                        pltpu.VMEM((nck, N_KV_HEADS, LANES, KEY_CHUNK), bf16),
                        pltpu.VMEM((nck, KEY_CHUNK, KV_WIDTH), bf16),
                        pltpu.VMEM((N_KV_HEADS, 1, KEY_CHUNK), f32),
                        pltpu.VMEM((nck, Q_ROWS, KEY_CHUNK), f32),
                        pltpu.VMEM((N_IDX_HEADS * Q_ROWS, LANES), bf16),
                        pltpu.VMEM((N_KV_HEADS, 2 * Q_ROWS, LANES), bf16),
                        pltpu.VMEM((N_KV_HEADS, 2 * Q_ROWS, LANES), f32),
                        pltpu.VMEM((N_KV_HEADS, 2 * Q_ROWS, KEY_CHUNK), f32),
                        pltpu.VMEM((N_KV_HEADS, 2 * Q_ROWS, LANES), f32)],
        compiler_params=_params("arbitrary", "arbitrary"),
        name="prompt_attention",
    )(qi, q, kw, kw, k, v)


PAGES_PER_STEP = 16
PAGE = 128
SAMPLE_Q = 8
SAMPLE_ROWS = N_HEADS * SAMPLE_Q


def _sample_score_kernel(pt_ref, qi_ref, w_ref, kin_ref, *refs):
    pages = refs[:PAGES_PER_STEP]
    out_ref = refs[PAGES_PER_STEP]
    j = pl.program_id(1)
    last = pl.num_programs(1) - 1
    qi = qi_ref[...]
    w = w_ref[...]

    def score(kit):
        s = jnp.dot(qi, kit.astype(bf16), preferred_element_type=f32)
        s = jnp.maximum(s, 0.0) * w
        return jnp.sum(s.reshape(N_IDX_HEADS, SAMPLE_Q, s.shape[-1]), axis=0)

    @pl.when(j < last)
    def _():
        out_ref[...] = score(jnp.concatenate([r[...] for r in pages], axis=1))

    @pl.when(j == last)
    def _():
        sc = score(kin_ref[...])
        col = lax.broadcasted_iota(i32, sc.shape, 1)
        row = lax.broadcasted_iota(i32, sc.shape, 0)
        out_ref[...] = jnp.full(out_ref.shape, -jnp.inf, f32)
        out_ref[:, 0:PAGE] = jnp.where(col <= row, sc, -jnp.inf)


def _page_spec(block, layer, n, n_pages):
    def index(b, j, pt):
        page = pt[b, jnp.minimum(j * PAGES_PER_STEP + n, n_pages - 1)]
        return (layer, page) + (0,) * (len(block) - 2)
    return pl.BlockSpec(block, index)


def _sample_scores(page_table, qi_s, w_s, ki_new_t, idx_cache_t, layer):
    db, n_pages = page_table.shape
    steps = n_pages // PAGES_PER_STEP + 1
    per_b = lambda b, j, pt: (b, 0, 0)
    step_w = PAGES_PER_STEP * PAGE
    return pl.pallas_call(
        _sample_score_kernel,
        grid_spec=pltpu.PrefetchScalarGridSpec(
            num_scalar_prefetch=1,
            grid=(db, steps),
            in_specs=[pl.BlockSpec((None, SAMPLE_ROWS, IDX_DIM), per_b),
                      pl.BlockSpec((None, SAMPLE_ROWS, 1), per_b),
                      pl.BlockSpec((None, IDX_DIM, PAGE), per_b)]
                     + [_page_spec((None, None, IDX_DIM, PAGE), layer, n, n_pages)
                        for n in range(PAGES_PER_STEP)],
            out_specs=pl.BlockSpec((None, SAMPLE_Q, step_w), lambda b, j, pt: (b, 0, j)),
        ),
        out_shape=jax.ShapeDtypeStruct((db, SAMPLE_Q, steps * step_w), f32),
        compiler_params=_params("arbitrary", "arbitrary"),
        name="sample_scores",
    )(page_table, qi_s, w_s, ki_new_t, *([idx_cache_t] * PAGES_PER_STEP))


def _sample_threshold_kernel(keys_in, keys_out, t_out, *, ksel):
    rows, width = keys_in.shape
    nch = width // KEY_CHUNK
    keys_out[...] = keys_in[...]
    load = lambda c: keys_out[:, c * KEY_CHUNK:(c + 1) * KEY_CHUNK]

    def store(c, val):
        keys_out[:, c * KEY_CHUNK:(c + 1) * KEY_CHUNK] = val
    t, c_ge = _kth_largest_score(load, nch, rows, ksel, True)
    _demote_excess_ties(load, store, nch, rows, ksel, t, c_ge, (width - 1).bit_length())
    t_out[...] = jnp.broadcast_to(t, t_out.shape)


def _sample_threshold(keys2d, ksel):
    n, width = keys2d.shape
    rows = 128
    return pl.pallas_call(
        functools.partial(_sample_threshold_kernel, ksel=ksel),
        grid=(n // rows,),
        in_specs=[pl.BlockSpec((rows, width), lambda i: (i, 0))],
        out_specs=[pl.BlockSpec((rows, width), lambda i: (i, 0)),
                   pl.BlockSpec((rows, LANES), lambda i: (i, 0))],
        out_shape=[jax.ShapeDtypeStruct((n, width), f32),
                   jax.ShapeDtypeStruct((n, LANES), f32)],
        compiler_params=_params("arbitrary"),
        name="sample_threshold",
    )(keys2d)


def _sample_attn_kernel(pt_ref, q_ref, sc_ref, t_ref, kn_ref, vn_ref, *refs):
    n = PAGES_PER_STEP
    k_pages, v_pages = refs[:n], refs[n:2 * n]
    o_ref, m_s, l_s, acc_s = refs[2 * n:]
    j = pl.program_id(1)
    last = pl.num_programs(1) - 1
    group_rows = 2 * SAMPLE_Q

    @pl.when(j == 0)
    def _():
        m_s[...] = jnp.full(m_s.shape, NEG_BIG, f32)
        l_s[...] = jnp.zeros(l_s.shape, f32)
        acc_s[...] = jnp.zeros(acc_s.shape, f32)

    t = t_ref[:, 0:1]

    def update(kt, vt, sc):
        bias = jnp.where(sc >= t, 0.0, NEG_BIG)
        bias = jnp.concatenate([bias, bias], axis=0)
        lg = jnp.concatenate(
            [jnp.dot(q_ref[h * group_rows:(h + 1) * group_rows, :], kt[h],
                     preferred_element_type=f32) + bias for h in range(N_KV_HEADS)], axis=0)
        m_old = m_s[...]
        m_new = jnp.maximum(m_old, jnp.max(lg, axis=-1, keepdims=True))
        alpha = jnp.exp(m_old - m_new)
        pr = jnp.exp(lg - m_new[:, 0:1])
        l_s[...] = alpha * l_s[...] + jnp.sum(pr, axis=-1, keepdims=True)
        pr = pr.astype(bf16)
        pv = jnp.concatenate(
            [lax.dot_general(pr[h * group_rows:(h + 1) * group_rows], vt[h],
                             (((1,), (1,)), ((), ())), preferred_element_type=f32)
             for h in range(N_KV_HEADS)], axis=0)
        acc_s[...] = acc_s[...] * alpha[:, 0:HEAD_DIM] + pv
        m_s[...] = m_new

    def lanes_of(page_refs):
        return jnp.concatenate([r[...] for r in page_refs], axis=2).astype(bf16)

    @pl.when(j < last)
    def _():
        update(lanes_of(k_pages), lanes_of(v_pages), sc_ref[...])

    @pl.when(j == last)
    def _():
        update(kn_ref[...], vn_ref[...], sc_ref[:, 0:PAGE])
        o = acc_s[...] / l_s[:, 0:HEAD_DIM]
        for h in range(N_HEADS):
            o_ref[:, h * HEAD_DIM:(h + 1) * HEAD_DIM] = o[h * SAMPLE_Q:(h + 1) * SAMPLE_Q]


def _sample_attention(page_table, q_s, scores3d, t3d, k_new_t, v_new_t, k_cache_t, v_cache_t,
                      layer):
    db, n_pages = page_table.shape
    steps = n_pages // PAGES_PER_STEP + 1
    per_b = lambda b, j, pt: (b, 0, 0)
    per_b4 = lambda b, j, pt: (b, 0, 0, 0)
    step_w = PAGES_PER_STEP * PAGE
    page_block = (None, None, N_KV_HEADS, HEAD_DIM, PAGE)
    return pl.pallas_call(
        _sample_attn_kernel,
        grid_spec=pltpu.PrefetchScalarGridSpec(
            num_scalar_prefetch=1,
            grid=(db, steps),
            in_specs=[pl.BlockSpec((None, SAMPLE_ROWS, HEAD_DIM), per_b),
                      pl.BlockSpec((None, SAMPLE_Q, step_w), lambda b, j, pt: (b, 0, j)),
                      pl.BlockSpec((None, SAMPLE_Q, LANES), per_b),
                      pl.BlockSpec((None, N_KV_HEADS, HEAD_DIM, PAGE), per_b4),
                      pl.BlockSpec((None, N_KV_HEADS, HEAD_DIM, PAGE), per_b4)]
                     + [_page_spec(page_block, layer, n, n_pages)
                        for n in range(PAGES_PER_STEP)] * 2,
            out_specs=pl.BlockSpec((None, SAMPLE_Q, ATTN_WIDTH), per_b),
            scratch_shapes=[pltpu.VMEM((SAMPLE_ROWS, LANES), f32),
                            pltpu.VMEM((SAMPLE_ROWS, LANES), f32),
                            pltpu.VMEM((SAMPLE_ROWS, HEAD_DIM), f32)],
        ),
        out_shape=jax.ShapeDtypeStruct((db, SAMPLE_Q, ATTN_WIDTH), f32),
        compiler_params=_params("arbitrary", "arbitrary"),
        name="sample_attention",
    )(page_table, q_s, scores3d, t3d, k_new_t, v_new_t,
      *([k_cache_t] * PAGES_PER_STEP), *([v_cache_t] * PAGES_PER_STEP))


def _pool_diff_prompt(u_ref, hist_ref, seq):
    tm = u_ref.shape[0]
    start = (pl.program_id(0) * tm) % seq
    u = u_ref[...]
    hist = hist_ref[...] * jnp.where(start == 0, 0.0, 1.0)
    ext = jnp.concatenate([hist, u], axis=0)
    pos = start + lax.broadcasted_iota(i32, (tm, 1), 0)
    outs = []
    for g, win in enumerate(POOL_WINDOWS):
        s = ext[:, g * POOL_GROUP:(g + 1) * POOL_GROUP]
        span = 1
        while span < win:
            s = s + pltpu.roll(s, span, 0)
            span *= 2
        cnt = jnp.minimum(pos + 1, win).astype(f32)
        outs.append(s[HIST_ROWS:] / cnt - u[:, g * POOL_GROUP:(g + 1) * POOL_GROUP])
    return outs


def _pool_diff_sample(u_ref):
    nb = u_ref.shape[0]
    outs = []
    for g, win in enumerate(POOL_WINDOWS):
        lanes = slice(g * POOL_GROUP, (g + 1) * POOL_GROUP)
        cur = u_ref[:, HIST_ROWS:HIST_ROWS + 8, lanes]
        s = cur
        for d in range(1, win):
            s = s + u_ref[:, HIST_ROWS - d:HIST_ROWS + 8 - d, lanes]
        outs.append((s / float(win) - cur).reshape(nb * 8, POOL_GROUP))
    return outs


def _merge_kernel(x_ref, ao_ref, u_ref, hist_ref, ga_ref, gb_ref, wpool_ref, pscale_ref,
                  wba_ref, wbp_ref, wout_ref, fg_ref, wr_ref, br_ref,
                  h_ref, hn_ref, route_ref, *, seq):
    diffs = _pool_diff_sample(u_ref) if seq is None else _pool_diff_prompt(u_ref, hist_ref, seq)
    pool = jnp.concatenate(
        [jnp.dot(d.astype(bf16), wpool_ref[g], preferred_element_type=f32)
         for g, d in enumerate(diffs)], axis=1) * pscale_ref[...]
    br_a = jnp.dot(ao_ref[...].astype(bf16), wba_ref[...], preferred_element_type=f32)
    br_p = jnp.dot(pool.astype(bf16), wbp_ref[...], preferred_element_type=f32)
    sig = lambda z: 1.0 / (1.0 + jnp.exp(-z))
    mix = sig(ga_ref[...]) * br_a + sig(gb_ref[...]) * br_p
    h = x_ref[...] + jnp.dot(mix.astype(bf16), wout_ref[...], preferred_element_type=f32)
    h_ref[...] = h
    hn = _rms(h, fg_ref[...])
    hn_ref[...] = hn

    logit = jnp.dot(hn, wr_ref[...], preferred_element_type=f32,
                    precision=lax.Precision.HIGHEST) + br_ref[...]
    lane_i = lax.broadcasted_iota(i32, logit.shape, 1)
    lane = lane_i.astype(f32)
    big = float(LANES)
    gl = jnp.where(lane_i < N_EXPERTS, -jnp.inf,
                   jnp.where(lane_i < N_EXPERTS + N_GROUPS, logit, -jnp.inf))
    gmax = jnp.max(gl, axis=-1, keepdims=True)
    g_sel = jnp.min(jnp.where(gl == gmax, lane, big), axis=-1, keepdims=True) - N_EXPERTS
    g_w = 1.0 / jnp.sum(jnp.exp(gl - gmax), axis=-1, keepdims=True)
    group_of_lane = (lane_i // EXPERTS_PER_GROUP).astype(f32)
    el = jnp.where(group_of_lane == g_sel, logit, -jnp.inf)
    v1 = jnp.max(el, axis=-1, keepdims=True)
    i1 = jnp.min(jnp.where(el == v1, lane, big), axis=-1, keepdims=True)
    el2 = jnp.where(lane == i1, -jnp.inf, el)
    v2 = jnp.max(el2, axis=-1, keepdims=True)
    i2 = jnp.min(jnp.where(el2 == v2, lane, big), axis=-1, keepdims=True)
    e21 = jnp.exp(v2 - v1)
    w1 = g_w / (1.0 + e21)
    w2 = g_w * e21 / (1.0 + e21)
    route_ref[...] = jnp.where(lane_i == 0, i1,
                     jnp.where(lane_i == 1, i2,
                     jnp.where(lane_i == 2, w1, jnp.where(lane_i == 3, w2, 0.0))))


def _merge(x2d, ao, u, ga, gb, lw, seq):
    n = x2d.shape[0]
    tm = PROJ_ROWS
    row = lambda i: (i, 0)
    const2 = lambda i: (0, 0)
    const3 = lambda i: (0, 0, 0)
    if seq is None:
        u_spec = pl.BlockSpec((tm // 8, HIST_ROWS + 8, POOL_WIDTH), lambda i: (i, 0, 0))
        hist = jnp.zeros((HIST_ROWS, POOL_WIDTH), f32)
        hist_spec = pl.BlockSpec((HIST_ROWS, POOL_WIDTH), const2)
    else:
        u_spec = pl.BlockSpec((tm, POOL_WIDTH), row)
        hist = u
        hist_spec = pl.BlockSpec((HIST_ROWS, POOL_WIDTH),
                                 lambda i: (jnp.maximum(i * (tm // HIST_ROWS) - 1, 0), 0))
    return pl.pallas_call(
        functools.partial(_merge_kernel, seq=seq),
        grid=(n // tm,),
        in_specs=[pl.BlockSpec((tm, D_MODEL), row),
                  pl.BlockSpec((tm, ATTN_WIDTH), row),
                  u_spec, hist_spec,
                  pl.BlockSpec((tm, D_MODEL), row),
                  pl.BlockSpec((tm, D_MODEL), row),
                  pl.BlockSpec((4, POOL_GROUP, POOL_GROUP), const3),
                  pl.BlockSpec((1, POOL_WIDTH), const2),
                  pl.BlockSpec((ATTN_WIDTH, D_MODEL), const2),
                  pl.BlockSpec((POOL_WIDTH, D_MODEL), const2),
                  pl.BlockSpec((D_MODEL, D_MODEL), const2),
                  pl.BlockSpec((1, D_MODEL), const2),
                  pl.BlockSpec((D_MODEL, LANES), const2),
                  pl.BlockSpec((1, LANES), const2)],
        out_specs=[pl.BlockSpec((tm, D_MODEL), row),
                   pl.BlockSpec((tm, D_MODEL), row),
                   pl.BlockSpec((tm, LANES), row)],
        out_shape=[jax.ShapeDtypeStruct((n, D_MODEL), f32),
                   jax.ShapeDtypeStruct((n, D_MODEL), f32),
                   jax.ShapeDtypeStruct((n, LANES), f32)],
        compiler_params=_params("arbitrary"),
        name="merge",
    )(x2d, ao, u, hist, ga, gb, lw['w_pool'], lw['pool_scale'], lw['w_ba'], lw['w_bp'],
      lw['w_out'], lw['ffn_g'], lw['w_router'], lw['b_router'])


RANK_ROWS = 512


def _rank_kernel(route_ref, rank_ref, counts_ref, base_s):
    @pl.when(pl.program_id(0) == 0)
    def _():
        base_s[...] = jnp.zeros(base_s.shape, f32)
    r = route_ref[...]
    tt = r.shape[0]
    lane = lax.broadcasted_iota(i32, r.shape, 1).astype(f32)
    oh1 = lane == r[:, 0:1]
    oh2 = lane == r[:, 1:2]
    oh = jnp.where(oh1, 1.0, 0.0) + jnp.where(oh2, 1.0, 0.0)
    below = (lax.broadcasted_iota(i32, (tt, tt), 0) > lax.broadcasted_iota(i32, (tt, tt), 1))
    before = jnp.dot(jnp.where(below, 1.0, 0.0).astype(bf16), oh.astype(bf16),
                     preferred_element_type=f32) + base_s[...]
    r1 = jnp.sum(jnp.where(oh1, before, 0.0), axis=-1, keepdims=True)
    r2 = jnp.sum(jnp.where(oh2, before, 0.0), axis=-1, keepdims=True)
    lane_i = lax.broadcasted_iota(i32, r.shape, 1)
    rank_ref[...] = jnp.where(lane_i == 0, r1, jnp.where(lane_i == 1, r2, 0.0))
    base_s[...] = base_s[...] + jnp.sum(oh, axis=0, keepdims=True)
    counts_ref[...] = base_s[...]


def _expert_ranks(route):
    n = route.shape[0]
    tt = min(RANK_ROWS, n)
    return pl.pallas_call(
        _rank_kernel,
        grid=(n // tt,),
        in_specs=[pl.BlockSpec((tt, LANES), lambda i: (i, 0))],
        out_specs=[pl.BlockSpec((tt, LANES), lambda i: (i, 0)),
                   pl.BlockSpec((1, LANES), lambda i: (0, 0))],
        out_shape=[jax.ShapeDtypeStruct((n, LANES), f32),
                   jax.ShapeDtypeStruct((1, LANES), f32)],
        scratch_shapes=[pltpu.VMEM((1, LANES), f32)],
        compiler_params=_params("arbitrary"),
        name="expert_ranks",
    )(route)


INVERT_ROWS = 8192


def _slot_tokens_kernel(dest_ref, tok_ref):
    i = pl.program_id(0)
    tile = dest_ref.shape[0]

    @pl.when(i == 0)
    def _():
        def zero(s, carry):
            tok_ref[s] = 0
            return carry
        lax.fori_loop(0, tok_ref.shape[0], zero, 0, unroll=16)

    def body(a, carry):
        tok_ref[dest_ref[a]] = (i * tile + a) // TOP_K
        return carry
    lax.fori_loop(0, tile, body, 0, unroll=16)


def _slot_tokens(dest, n_slots):
    n = dest.shape[0]
    tile = min(INVERT_ROWS, n)
    return pl.pallas_call(
        _slot_tokens_kernel,
        grid=(n // tile,),
        in_specs=[pl.BlockSpec((tile,), lambda i: (i,), memory_space=pltpu.SMEM)],
        out_specs=pl.BlockSpec((n_slots,), lambda i: (0,), memory_space=pltpu.SMEM),
        out_shape=jax.ShapeDtypeStruct((n_slots,), i32),
        compiler_params=_params("arbitrary"),
        name="moe_slot_tokens",
    )(dest)


def _expert_ffn_kernel(be_ref, nact_ref, tok_ref, tok_next_ref, x_hbm, wg_ref, wu_ref, wd_ref,
                       ys_ref, xbuf, sem):
    i = pl.program_id(0)
    n_active = nact_ref[0]
    bm = ys_ref.shape[0]
    slot = i % 2

    def gather(tok, s):
        def body(r, carry):
            pltpu.make_async_copy(x_hbm.at[pl.ds(tok[0, r], 1)], xbuf.at[s, pl.ds(r, 1)],
                                  sem.at[s]).start()
            return carry
        lax.fori_loop(0, bm, body, 0, unroll=8)

    @pl.when(i == 0)
    def _():
        gather(tok_ref, 0)

    @pl.when(i + 1 < n_active)
    def _():
        gather(tok_next_ref, 1 - slot)

    @pl.when(i < n_active)
    def _():
        pltpu.make_async_copy(x_hbm.at[pl.ds(0, bm)], xbuf.at[slot], sem.at[slot]).wait()
        x = xbuf[slot].astype(bf16)
        g = jnp.dot(x, wg_ref[...], preferred_element_type=f32)
        u = jnp.dot(x, wu_ref[...], preferred_element_type=f32)
        hb = (g / (1.0 + jnp.exp(-g))) * u
        ys_ref[...] = jnp.dot(hb.astype(bf16), wd_ref[...], preferred_element_type=f32)

    @pl.when(i >= n_active)
    def _():
        ys_ref[...] = jnp.zeros(ys_ref.shape, f32)


def _expert_ffn(block_expert, n_active, slot_tok, x, wg, wu, wd, bm):
    n_blocks = slot_tok.shape[0] // bm
    tok3 = slot_tok.reshape(n_blocks, 1, bm)
    by_expert = lambda i, be, na: (be[i], 0, 0)
    smem_block = lambda index: pl.BlockSpec((None, 1, bm), index, memory_space=pltpu.SMEM)
    return pl.pallas_call(
        _expert_ffn_kernel,
        grid_spec=pltpu.PrefetchScalarGridSpec(
            num_scalar_prefetch=2,
            grid=(n_blocks,),
            in_specs=[smem_block(lambda i, be, na: (i, 0, 0)),
                      smem_block(lambda i, be, na: (jnp.minimum(i + 1, n_blocks - 1), 0, 0)),
                      pl.BlockSpec(memory_space=pl.ANY),
                      pl.BlockSpec((None, D_MODEL, D_EXPERT), by_expert),
                      pl.BlockSpec((None, D_MODEL, D_EXPERT), by_expert),
                      pl.BlockSpec((None, D_EXPERT, D_MODEL), by_expert)],
            out_specs=pl.BlockSpec((bm, D_MODEL), lambda i, be, na: (i, 0)),
            scratch_shapes=[pltpu.VMEM((2, bm, D_MODEL), f32),
                            pltpu.SemaphoreType.DMA((2,))],
        ),
        out_shape=jax.ShapeDtypeStruct((n_blocks * bm, D_MODEL), f32),
        compiler_params=_params("arbitrary"),
        name="expert_ffn",
    )(block_expert, n_active, tok3, tok3, x, wg, wu, wd)


COMBINE_ROWS = 128


def _combine_kernel(dest_ref, dest_next_ref, h_ref, route_ref, g_ref, ys_hbm, y_ref, buf, sem):
    i = pl.program_id(0)
    tt = h_ref.shape[0]
    slot = i % 2

    def gather(dest, s):
        def body(t, carry):
            for j in range(TOP_K):
                pltpu.make_async_copy(ys_hbm.at[pl.ds(dest[0, TOP_K * t + j], 1)],
                                      buf.at[s, j, pl.ds(t, 1)], sem.at[s]).start()
            return carry
        lax.fori_loop(0, tt, body, 0, unroll=4)

    @pl.when(i == 0)
    def _():
        gather(dest_ref, 0)

    @pl.when(i + 1 < pl.num_programs(0))
    def _():
        gather(dest_next_ref, 1 - slot)

    for j in range(TOP_K):
        pltpu.make_async_copy(ys_hbm.at[pl.ds(0, tt)], buf.at[slot, j], sem.at[slot]).wait()
    r = route_ref[...]
    moe = buf[slot, 0] * r[:, 2:3] + buf[slot, 1] * r[:, 3:4]
    y_ref[...] = _rms(h_ref[...] + moe, g_ref[...])


def _combine(dest, h, route, final_g, ys):
    n = h.shape[0]
    tt = COMBINE_ROWS
    n_tiles = n // tt
    dest3 = dest.reshape(n_tiles, 1, tt * TOP_K)
    row = lambda i: (i, 0)
    smem_block = lambda index: pl.BlockSpec((None, 1, tt * TOP_K), index, memory_space=pltpu.SMEM)
    return pl.pallas_call(
        _combine_kernel,
        grid=(n_tiles,),
        in_specs=[smem_block(lambda i: (i, 0, 0)),
                  smem_block(lambda i: (jnp.minimum(i + 1, n_tiles - 1), 0, 0)),
                  pl.BlockSpec((tt, D_MODEL), row),
                  pl.BlockSpec((tt, LANES), row),
                  pl.BlockSpec((1, D_MODEL), lambda i: (0, 0)),
                  pl.BlockSpec(memory_space=pl.ANY)],
        out_specs=pl.BlockSpec((tt, D_MODEL), row),
        out_shape=jax.ShapeDtypeStruct((n, D_MODEL), f32),
        scratch_shapes=[pltpu.VMEM((2, TOP_K, tt, D_MODEL), f32),
                        pltpu.SemaphoreType.DMA((2,))],
        compiler_params=_params("arbitrary"),
        name="moe_combine",
    )(dest3, dest3, h, route, final_g, ys)


def _moe_and_final_norm(h, hn, route, lw, final_g, bm):
    n = h.shape[0]
    rank, counts = _expert_ranks(route)
    counts = counts[0, :N_EXPERTS].astype(i32)
    padded = (counts + bm - 1) // bm * bm
    pend = jnp.cumsum(padded)
    pstart = pend - padded
    e_id = route[:, 0:TOP_K].astype(i32)
    dest = (pstart[e_id] + rank[:, 0:TOP_K].astype(i32)).reshape(-1)
    n_blocks = (n * TOP_K + N_EXPERTS * (bm - 1)) // bm + 1
    n_active = (pend[-1] // bm).astype(i32)
    blk = jnp.minimum(jnp.arange(n_blocks, dtype=i32), n_active - 1) * bm
    block_expert = jnp.minimum(jnp.sum((pend[None, :] <= blk[:, None]).astype(i32), axis=1),
                               N_EXPERTS - 1)
    slot_tok = _slot_tokens(dest, n_blocks * bm)
    ys = _expert_ffn(block_expert, n_active.reshape(1), slot_tok, hn,
                     lw['w_eg'], lw['w_eu'], lw['w_ed'], bm)
    return _combine(dest, h, route, final_g, ys)


def _rope_tables(pos, reps):
    half = HEAD_DIM // 2
    inv_freq = 1.0 / (ROPE_THETA ** (jnp.arange(half, dtype=f32) / half))
    ang = pos.astype(f32)[:, None] * inv_freq[None, :]
    cos, sin = jnp.cos(ang), jnp.sin(ang)
    cos = jnp.concatenate([cos, cos, cos, cos], axis=1)
    sin = jnp.concatenate([-sin, sin, -sin, sin], axis=1)
    return jnp.tile(cos, (reps, 1)), jnp.tile(sin, (reps, 1))


def _layer_weights(l, attn_norm_g, w_in, w_pool, pool_scale, w_branch_attn, w_branch_pool, w_out,
                   ffn_norm_g, w_rg, b_rg, w_re, b_re, w_eg, w_eu, w_ed):
    cuts = [0]
    for s in SEGMENTS:
        cuts.append(cuts[-1] + s)
    w = w_in[l]
    kw_pad = LANES - IDX_DIM - N_IDX_HEADS
    w_packed = jnp.concatenate(
        [w[:, cuts[0]:cuts[4]], w[:, cuts[4]:cuts[6]], jnp.zeros((D_MODEL, kw_pad), f32),
         w[:, cuts[6]:]], axis=1).astype(bf16)
    r_pad = LANES - N_EXPERTS - N_GROUPS
    return dict(
        attn_g=attn_norm_g[l].reshape(1, D_MODEL), w_packed=w_packed,
        w_pool=w_pool[l].astype(bf16), pool_scale=pool_scale[l].reshape(1, POOL_WIDTH),
        w_ba=w_branch_attn[l].astype(bf16), w_bp=w_branch_pool[l].astype(bf16),
        w_out=w_out[l].astype(bf16), ffn_g=ffn_norm_g[l].reshape(1, D_MODEL),
        w_router=jnp.concatenate([w_re[l], w_rg[l], jnp.zeros((D_MODEL, r_pad), f32)], axis=1),
        b_router=jnp.concatenate([b_re[l], b_rg[l], jnp.zeros((r_pad,), f32)]).reshape(1, LANES),
        w_eg=w_eg[l].astype(bf16), w_eu=w_eu[l].astype(bf16), w_ed=w_ed[l].astype(bf16))


def _prompt_layer(x, lw, final_g):
    batch, seq, _ = x.shape
    x2d = x.reshape(batch * seq, D_MODEL)
    cos, sin = _rope_tables(jnp.arange(seq), 1)
    q, k, v, qi, kw, u, ga, gb = _project(x2d, lw['attn_g'], lw['w_packed'], cos, sin)
    ao = _prompt_attention(qi, q, kw, k, v, batch, seq, min(TOPK_MAX, seq // 4))
    h, hn, route = _merge(x2d, ao, u, ga, gb, lw, seq)
    y = _moe_and_final_norm(h, hn, route, lw, final_g, bm=256)
    return (y.reshape(batch, seq, D_MODEL),
            k.reshape(batch, seq, N_KV_HEADS, HEAD_DIM), v.reshape(batch, seq, N_KV_HEADS, HEAD_DIM),
            kw[:, :IDX_DIM].reshape(batch, seq, IDX_DIM),
            u.reshape(batch, seq, POOL_WIDTH)[:, seq - POOL_HIST:])


def _sample_layer(x, l, cache_k, cache_v, cache_idx_k, state_pool, page_table, lw, final_g):
    db, ds, _ = x.shape
    page = cache_k.shape[2]
    past = page_table.shape[1] * page
    x2d = x.reshape(db * ds, D_MODEL)
    cos, sin = _rope_tables(past + jnp.arange(ds), PROJ_ROWS // ds)
    q, k, v, qi, kw, u, ga, gb = _project(x2d, lw['attn_g'], lw['w_packed'], cos, sin)

    rows_hq = lambda a, nh, d: a.reshape(db, ds, nh, d).transpose(0, 2, 1, 3).reshape(db, nh * ds, d)
    qi_s = rows_hq(qi, N_IDX_HEADS, IDX_DIM)
    w_s = kw[:, IDX_DIM:IDX_DIM + N_IDX_HEADS].reshape(db, ds, N_IDX_HEADS)
    w_s = w_s.transpose(0, 2, 1).reshape(db, N_IDX_HEADS * ds, 1)
    pad_slots = lambda a: jnp.pad(a, [(0, 0)] * (a.ndim - 1) + [(0, page - ds)])
    ki_new_t = pad_slots(kw[:, :IDX_DIM].reshape(db, ds, IDX_DIM).transpose(0, 2, 1))
    scores = _sample_scores(page_table, qi_s, w_s, ki_new_t,
                            cache_idx_k.transpose(0, 1, 3, 2), l)
    width = scores.shape[-1]
    scores, t = _sample_threshold(scores.reshape(db * ds, width),
                                  min(TOPK_MAX, (past + ds) // 4))

    as_page = lambda a: pad_slots(
        a.reshape(db, ds, N_KV_HEADS, HEAD_DIM).transpose(0, 2, 3, 1)).astype(bf16)
    ao = _sample_attention(page_table, rows_hq(q, N_HEADS, HEAD_DIM),
                           scores.reshape(db, ds, width), t.reshape(db, ds, LANES),
                           as_page(k), as_page(v),
                           cache_k.transpose(0, 1, 3, 4, 2), cache_v.transpose(0, 1, 3, 4, 2), l)

    u3 = u.reshape(db, ds, POOL_WIDTH)
    u_all = jnp.concatenate([jnp.zeros((db, 1, POOL_WIDTH), f32), state_pool[l], u3], axis=1)
    h, hn, route = _merge(x2d, ao.reshape(db * ds, ATTN_WIDTH), u_all, ga, gb, lw, None)
    y = _moe_and_final_norm(h, hn, route, lw, final_g, bm=128)
    return (y.reshape(db, ds, D_MODEL),
            k.reshape(db, ds, N_KV_HEADS, HEAD_DIM), v.reshape(db, ds, N_KV_HEADS, HEAD_DIM),
            kw[:, :IDX_DIM].reshape(db, ds, IDX_DIM),
            u_all[:, -POOL_HIST:])


def kernel(x_prompt, x_sample, cache_k, cache_v, cache_idx_k, state_pool, page_table, attn_norm_g, w_in, w_pool, pool_scale, w_branch_attn, w_branch_pool, w_out, ffn_norm_g, w_router_group, b_router_group, w_router_expert, b_router_expert, w_exp_gate, w_exp_up, w_exp_down, final_norm_g):
    depth = w_in.shape[0]
    assert depth == 1, "the final norm is fused into the (single) layer's MoE combine"
    final_g = final_norm_g.reshape(1, D_MODEL)
    lw = _layer_weights(0, attn_norm_g, w_in, w_pool, pool_scale, w_branch_attn, w_branch_pool,
                        w_out, ffn_norm_g, w_router_group, b_router_group, w_router_expert,
                        b_router_expert, w_exp_gate, w_exp_up, w_exp_down)
    yp, kp, vp, kip, up = _prompt_layer(x_prompt, lw, final_g)
    ys, ks, vs, kis, us = _sample_layer(x_sample, 0, cache_k, cache_v, cache_idx_k, state_pool,
                                        page_table, lw, final_g)
    stack = lambda a: a[None]
    return (yp, ys, stack(kp), stack(vp), stack(kip), stack(up),
            stack(ks), stack(vs), stack(kis), stack(us))
```

```python
import functools

import jax
import jax.numpy as jnp
from jax import lax
from jax.experimental import pallas as pl
from jax.experimental.pallas import tpu as pltpu

D_MODEL = 1024
HEAD_DIM = 64
N_HEADS = 8
N_KV_HEADS = 4
ATTN_WIDTH = N_HEADS * HEAD_DIM
KV_WIDTH = N_KV_HEADS * HEAD_DIM
N_IDX_HEADS = 8
IDX_DIM = 64
TOPK_MAX = 256
ROPE_THETA = 10000.0
POOL_WINDOWS = (2, 4, 8, 16)
POOL_WIDTH = 512
POOL_GROUP = 128
POOL_HIST = 15
N_GROUPS = 4
EXPERTS_PER_GROUP = 8
N_EXPERTS = 32
TOP_K = 2
D_EXPERT = 512
RMS_EPS = 1e-6
SEGMENTS = (ATTN_WIDTH, KV_WIDTH, KV_WIDTH, N_IDX_HEADS * IDX_DIM, IDX_DIM, N_IDX_HEADS,
            POOL_WIDTH, D_MODEL, D_MODEL)

LANES = 128
VMEM_LIMIT = 56 * 1024 * 1024
KEY_CHUNK = 256
Q_ROWS = 128
PROJ_ROWS = 256
HIST_ROWS = 16
NEG_BIG = -1e30
INT_MIN = -2 ** 31
KEY_NEG_INF = -2139095041
FLT_LOWEST = -3.4028234663852886e38
SUM_EXP_FLOOR = 1e-22

_SEG = dict(q=(0, 512), k=(512, 768), v=(768, 1024), qi=(1024, 1536), kw=(1536, 1664),
            u=(1664, 2176), ga=(2176, 3200), gb=(3200, 4224))
PROJ_PACKED = 4224

f32 = jnp.float32
bf16 = jnp.bfloat16
i32 = jnp.int32


def _params(*sem):
    return pltpu.CompilerParams(dimension_semantics=sem, vmem_limit_bytes=VMEM_LIMIT)


def _rms(x, g):
    return x * lax.rsqrt(jnp.mean(x * x, axis=-1, keepdims=True) + RMS_EPS) * g


def _key_to_float(k):
    return lax.bitcast_convert_type(k ^ ((k >> 31) & 0x7FFFFFFF), f32)


def _rope(x, cos, sin_signed):
    lane = lax.broadcasted_iota(i32, (x.shape[0], LANES), 1)
    first_half = (lane % HEAD_DIM) < HEAD_DIM // 2
    outs = []
    for c in range(x.shape[1] // LANES):
        xc = x[:, c * LANES:(c + 1) * LANES]
        partner = jnp.where(first_half, pltpu.roll(xc, LANES - HEAD_DIM // 2, 1),
                            pltpu.roll(xc, HEAD_DIM // 2, 1))
        outs.append(xc * cos + partner * sin_signed)
    return outs[0] if len(outs) == 1 else jnp.concatenate(outs, axis=1)


def _proj_kernel(x_ref, g_ref, w_ref, cos_ref, sin_ref,
                 q_ref, k_ref, v_ref, qi_ref, kw_ref, u_ref, ga_ref, gb_ref):
    xb = _rms(x_ref[...], g_ref[...]).astype(bf16)
    cos = cos_ref[...]
    sin = sin_ref[...]

    def seg(name):
        a, b = _SEG[name]
        return jnp.dot(xb, w_ref[:, a:b], preferred_element_type=f32)

    q_ref[...] = (_rope(seg('q'), cos, sin) * (HEAD_DIM ** -0.5)).astype(bf16)
    k_ref[...] = _rope(seg('k'), cos, sin)
    v_ref[...] = seg('v')
    qi_ref[...] = (_rope(seg('qi'), cos, sin) * (IDX_DIM ** -0.5)).astype(bf16)
    kw = seg('kw')
    lane = lax.broadcasted_iota(i32, kw.shape, 1)
    kw_ref[...] = jnp.where(lane < IDX_DIM, _rope(kw, cos, sin), kw * (N_IDX_HEADS ** -0.5))
    u_ref[...] = seg('u')
    ga_ref[...] = seg('ga')
    gb_ref[...] = seg('gb')


def _project(x2d, g, w_packed, cos_tab, sin_tab):
    n = x2d.shape[0]
    tm = PROJ_ROWS
    ntab = cos_tab.shape[0] // tm
    row = lambda i: (i, 0)
    const = lambda i: (0, 0)
    tab = lambda i: (i % ntab, 0)
    widths = (512, 256, 256, 512, 128, 512, 1024, 1024)
    dtypes = (bf16, f32, f32, bf16, f32, f32, f32, f32)
    return pl.pallas_call(
        _proj_kernel,
        grid=(n // tm,),
        in_specs=[pl.BlockSpec((tm, D_MODEL), row),
                  pl.BlockSpec((1, D_MODEL), const),
                  pl.BlockSpec((D_MODEL, PROJ_PACKED), const),
                  pl.BlockSpec((tm, LANES), tab),
                  pl.BlockSpec((tm, LANES), tab)],
        out_specs=[pl.BlockSpec((tm, w), row) for w in widths],
        out_shape=[jax.ShapeDtypeStruct((n, w), d) for w, d in zip(widths, dtypes)],
        compiler_params=_params("arbitrary"),
        name="project",
    )(x2d, g, w_packed, cos_tab, sin_tab)


def _count(load_chunk, nch, rows, pred):
    def body(c, acc):
        ind = jnp.where(pred(load_chunk(c), c), 1.0, 0.0)
        for s in range(KEY_CHUNK // LANES):
            acc = acc + ind[:, s * LANES:(s + 1) * LANES]
        return acc
    acc = jnp.zeros((rows, LANES), f32)
    if isinstance(nch, int):
        for c in range(nch):
            acc = body(c, acc)
    else:
        acc = lax.fori_loop(0, nch, body, acc)
    return jnp.sum(acc, axis=-1, keepdims=True)


def _kth_largest_score(load_chunk, nch, rows, ksel, search, zero_counts=None):
    if zero_counts is None:
        zero_counts = (_count(load_chunk, nch, rows, lambda sc, _: sc > 0.0),
                       _count(load_chunk, nch, rows, lambda sc, _: sc >= 0.0))
    c_pos, c_nonneg = zero_counts
    at_zero = jnp.logical_and(c_pos < ksel, c_nonneg >= ksel)

    def cond(state):
        it, _, cnt = state
        return jnp.logical_and(it < 32, jnp.max(jnp.where(at_zero, 0.0, cnt)) > ksel)

    def body(state):
        it, key, cnt = state
        cand = key + lax.shift_left(jnp.int32(1), 31 - it)
        cand_f = _key_to_float(cand)
        c = _count(load_chunk, nch, rows, lambda sc, _: sc >= cand_f)
        take = c >= ksel
        return it + 1, jnp.where(take, cand, key), jnp.where(take, c, cnt)

    total = jnp.where(search, nch * KEY_CHUNK, 0).astype(f32)
    _, key, cnt = lax.while_loop(
        cond, body, (jnp.int32(0), jnp.full((rows, 1), INT_MIN, i32), jnp.full((rows, 1), total)))
    lifted = key <= KEY_NEG_INF
    t = jnp.where(lifted, FLT_LOWEST, _key_to_float(key))
    cnt = jnp.where(lifted, 0.0, cnt)
    at_zero = jnp.logical_and(at_zero, search)
    return jnp.where(at_zero, 0.0, t), jnp.where(at_zero, c_nonneg, cnt)


def _demote_excess_ties(load_chunk, store_chunk, nch, rows, ksel, t, c_ge):
    @pl.when(jnp.max(c_ge) > ksel)
    def _():
        need = ksel - _count(load_chunk, nch, rows, lambda sc, c: sc > t)
        upto = (lax.broadcasted_iota(i32, (KEY_CHUNK, KEY_CHUNK), 0)
                <= lax.broadcasted_iota(i32, (KEY_CHUNK, KEY_CHUNK), 1))
        upto = jnp.where(upto, 1.0, 0.0).astype(bf16)

        def rewrite(c, seen):
            sc = load_chunk(c)
            tie = sc == t
            rank = seen + jnp.dot(jnp.where(tie, 1.0, 0.0).astype(bf16), upto,
                                  preferred_element_type=f32)
            store_chunk(c, jnp.where(jnp.where(tie, rank, 0.0) > need, -jnp.inf, sc))
            return rank[:, KEY_CHUNK - 1:KEY_CHUNK]
        seen = jnp.zeros((rows, 1), f32)
        if isinstance(nch, int):
            for c in range(nch):
                seen = rewrite(c, seen)
        else:
            lax.fori_loop(0, nch, rewrite, seen)


def _prompt_attn_kernel(qi_ref, q_ref, kwq_ref, kw_ref, k_ref, v_ref, o_ref,
                        kit2, kt2, vb, kn2, scores, qil, ql, m_s, mx_s, acc_s, *, ksel):
    i = pl.program_id(1)
    seq = kw_ref.shape[0]
    ck = KEY_CHUNK
    half = HEAD_DIM

    @pl.when(i == 0)
    def _prepare_keys():
        def body(c, carry):
            rows = pl.ds(pl.multiple_of(c * ck, ck), ck)
            kit = kw_ref[rows, :].T[0:half].astype(bf16)
            kit2[c, 0:half, :] = kit
            kit2[c, half:2 * half, :] = kit
            kc = k_ref[rows, :]
            for pair in range(2):
                kt = kc[:, pair * LANES:(pair + 1) * LANES].T.astype(bf16)
                for sub in range(2):
                    one = kt[sub * half:(sub + 1) * half]
                    kt2[c, 2 * pair + sub, 0:half, :] = one
                    kt2[c, 2 * pair + sub, half:2 * half, :] = one
                    sq = one.astype(f32)
                    sq = jnp.sum(sq * sq, axis=0, keepdims=True)
                    kn2[2 * pair + sub] = jnp.maximum(kn2[2 * pair + sub], sq)
            vb[c] = v_ref[rows, :].astype(bf16)
            return carry
        kn2[...] = jnp.zeros(kn2.shape, f32)
        lax.fori_loop(0, seq // ck, body, 0)

    lane = lax.broadcasted_iota(i32, (Q_ROWS, LANES), 1)
    low = lane < half
    for p in range(4):
        c = qi_ref[:, p * LANES:(p + 1) * LANES].astype(f32)
        qil[(2 * p) * Q_ROWS:(2 * p + 1) * Q_ROWS, :] = jnp.where(low, c, 0.0).astype(bf16)
        qil[(2 * p + 1) * Q_ROWS:(2 * p + 2) * Q_ROWS, :] = jnp.where(low, 0.0, c).astype(bf16)
        c = q_ref[:, p * LANES:(p + 1) * LANES].astype(f32)
        ql[p, 0:Q_ROWS, :] = jnp.where(low, c, 0.0).astype(bf16)
        ql[p, Q_ROWS:2 * Q_ROWS, :] = jnp.where(low, 0.0, c).astype(bf16)
    w_head = [kwq_ref[:, IDX_DIM + h:IDX_DIM + h + 1] for h in range(N_IDX_HEADS)]

    nch = (i + 2) // 2
    row_id = i * Q_ROWS + lax.broadcasted_iota(i32, (Q_ROWS, ck), 0)

    def lane_fold(x):
        return sum(x[:, j * LANES:(j + 1) * LANES] for j in range(ck // LANES))

    def score_chunk(c, carry):
        n_pos, n_nonneg = carry
        s = jnp.dot(qil[...], kit2[c], preferred_element_type=f32)
        acc = None
        for h in range(N_IDX_HEADS):
            term = jnp.maximum(s[h * Q_ROWS:(h + 1) * Q_ROWS], 0.0) * w_head[h]
            acc = term if acc is None else acc + term
        col = c * ck + lax.broadcasted_iota(i32, (Q_ROWS, ck), 1)
        sc = jnp.where(col <= row_id, acc, -jnp.inf)
        scores[c] = sc
        return (n_pos + lane_fold(jnp.where(sc > 0.0, 1.0, 0.0)),
                n_nonneg + lane_fold(jnp.where(sc >= 0.0, 1.0, 0.0)))
    zeros = jnp.zeros((Q_ROWS, LANES), f32)
    n_pos, n_nonneg = lax.fori_loop(0, nch, score_chunk, (zeros, zeros))
    zero_counts = (jnp.sum(n_pos, axis=-1, keepdims=True),
                   jnp.sum(n_nonneg, axis=-1, keepdims=True))

    load = lambda c: scores[c]

    def store(c, val):
        scores[c] = val
    t, c_ge = _kth_largest_score(load, nch, Q_ROWS, ksel, (i + 1) * Q_ROWS > ksel, zero_counts)
    _demote_excess_ties(load, store, nch, Q_ROWS, ksel, t, c_ge)

    def masked_logits(c, p, bias2):
        return jnp.dot(ql[p], kt2[c, p], preferred_element_type=f32) + bias2

    def bias_of(c):
        bias = jnp.where(scores[c] >= t, 0.0, NEG_BIG)
        return jnp.concatenate([bias, bias], axis=0)

    def attend_sweep():
        mx_s[...] = jnp.zeros(mx_s.shape, f32)
        acc_s[...] = jnp.zeros(acc_s.shape, f32)

        def attend_chunk(c, carry):
            bias2 = bias_of(c)
            for p in range(N_KV_HEADS):
                m = m_s[p]
                pr = jnp.exp(masked_logits(c, p, bias2) - jnp.concatenate([m, m], axis=1))
                mx_s[p] = mx_s[p] + pr
                acc_s[p] = acc_s[p] + jnp.dot(
                    pr.astype(bf16), vb[c, :, (p // 2) * LANES:(p // 2 + 1) * LANES],
                    preferred_element_type=f32)
            return carry
        lax.fori_loop(0, nch, attend_chunk, 0)

    smallest = None
    for p in range(N_KV_HEADS):
        qf = ql[p].astype(f32)
        qn2 = jnp.sum(qf * qf, axis=-1, keepdims=True)
        bound = jnp.sqrt(qn2 * jnp.max(kn2[p], axis=-1, keepdims=True))
        m_s[p] = jnp.broadcast_to(bound, m_s.shape[1:])
    attend_sweep()
    for p in range(N_KV_HEADS):
        low_p = jnp.min(jnp.sum(mx_s[p], axis=-1, keepdims=True))
        smallest = low_p if smallest is None else jnp.minimum(smallest, low_p)

    @pl.when(smallest < SUM_EXP_FLOOR)
    def _exact_maxima():
        mx_s[...] = jnp.full(mx_s.shape, NEG_BIG, f32)

        def max_chunk(c, carry):
            bias2 = bias_of(c)
            for p in range(N_KV_HEADS):
                mx_s[p] = jnp.maximum(mx_s[p], masked_logits(c, p, bias2))
            return carry
        lax.fori_loop(0, nch, max_chunk, 0)
        for p in range(N_KV_HEADS):
            m_s[p] = jnp.broadcast_to(jnp.max(mx_s[p], axis=-1, keepdims=True), m_s.shape[1:])
        attend_sweep()

    for p in range(N_KV_HEADS):
        o = acc_s[p] / jnp.sum(mx_s[p], axis=-1, keepdims=True)
        a, b = o[0:Q_ROWS], o[Q_ROWS:2 * Q_ROWS]
        if p % 2 == 0:
            b = pltpu.roll(b, half, 1)
        else:
            a = pltpu.roll(a, half, 1)
        o_ref[:, p * LANES:(p + 1) * LANES] = jnp.where(low, a, b).astype(bf16)


def _prompt_attention(qi, q, kw, k, v, batch, seq, ksel):
    nqb = seq // Q_ROWS
    nck = seq // KEY_CHUNK
    qrow = lambda b, i: (b * nqb + i, 0)
    whole = lambda b, i: (b, 0)
    return pl.pallas_call(
        functools.partial(_prompt_attn_kernel, ksel=ksel),
        grid=(batch, nqb),
        in_specs=[pl.BlockSpec((Q_ROWS, 512), qrow),
                  pl.BlockSpec((Q_ROWS, 512), qrow),
                  pl.BlockSpec((Q_ROWS, LANES), qrow),
                  pl.BlockSpec((seq, LANES), whole),
                  pl.BlockSpec((seq, KV_WIDTH), whole),
                  pl.BlockSpec((seq, KV_WIDTH), whole)],
        out_specs=pl.BlockSpec((Q_ROWS, ATTN_WIDTH), qrow),
        out_shape=jax.ShapeDtypeStruct((batch * seq, ATTN_WIDTH), bf16),
        scratch_shapes=[pltpu.VMEM((nck, LANES, KEY_CHUNK), bf16),
                        pltpu.VMEM((nck, N_KV_HEADS, LANES, KEY_CHUNK), bf16),
                        pltpu.VMEM((nck, KEY_CHUNK, KV_WIDTH), bf16),
                        pltpu.VMEM((N_KV_HEADS, 1, KEY_CHUNK), f32),
                        pltpu.VMEM((nck, Q_ROWS, KEY_CHUNK), f32),
                        pltpu.VMEM((N_IDX_HEADS * Q_ROWS, LANES), bf16),
                        pltpu.VMEM((N_KV_HEADS, 2 * Q_ROWS, LANES), bf16),
                        pltpu.VMEM((N_KV_HEADS, 2 * Q_ROWS, LANES), f32),
                        pltpu.VMEM((N_KV_HEADS, 2 * Q_ROWS, KEY_CHUNK), f32),
                        pltpu.VMEM((N_KV_HEADS, 2 * Q_ROWS, LANES), f32)],
        compiler_params=_params("arbitrary", "arbitrary"),
        name="prompt_attention",
    )(qi, q, kw, kw, k, v)


PAGES_PER_STEP = 16
PAGE = 128
SAMPLE_Q = 8
SAMPLE_ROWS = N_HEADS * SAMPLE_Q


def _sample_score_kernel(pt_ref, qi_ref, w_ref, kin_ref, *refs):
    pages = refs[:PAGES_PER_STEP]
    out_ref = refs[PAGES_PER_STEP]
    j = pl.program_id(1)
    last = pl.num_programs(1) - 1
    qi = qi_ref[...]
    w = w_ref[...]

    def score(kit):
        s = jnp.dot(qi, kit.astype(bf16), preferred_element_type=f32)
        s = jnp.maximum(s, 0.0) * w
        return jnp.sum(s.reshape(N_IDX_HEADS, SAMPLE_Q, s.shape[-1]), axis=0)

    @pl.when(j < last)
    def _():
        out_ref[...] = score(jnp.concatenate([r[...] for r in pages], axis=1))

    @pl.when(j == last)
    def _():
        sc = score(kin_ref[...])
        col = lax.broadcasted_iota(i32, sc.shape, 1)
        row = lax.broadcasted_iota(i32, sc.shape, 0)
        out_ref[...] = jnp.full(out_ref.shape, -jnp.inf, f32)
        out_ref[:, 0:PAGE] = jnp.where(col <= row, sc, -jnp.inf)


def _page_spec(block, layer, n, n_pages):
    def index(b, j, pt):
        page = pt[b, jnp.minimum(j * PAGES_PER_STEP + n, n_pages - 1)]
        return (layer, page) + (0,) * (len(block) - 2)
    return pl.BlockSpec(block, index)


def _sample_scores(page_table, qi_s, w_s, ki_new_t, idx_cache_t, layer):
    db, n_pages = page_table.shape
    steps = n_pages // PAGES_PER_STEP + 1
    per_b = lambda b, j, pt: (b, 0, 0)
    step_w = PAGES_PER_STEP * PAGE
    return pl.pallas_call(
        _sample_score_kernel,
        grid_spec=pltpu.PrefetchScalarGridSpec(
            num_scalar_prefetch=1,
            grid=(db, steps),
            in_specs=[pl.BlockSpec((None, SAMPLE_ROWS, IDX_DIM), per_b),
                      pl.BlockSpec((None, SAMPLE_ROWS, 1), per_b),
                      pl.BlockSpec((None, IDX_DIM, PAGE), per_b)]
                     + [_page_spec((None, None, IDX_DIM, PAGE), layer, n, n_pages)
                        for n in range(PAGES_PER_STEP)],
            out_specs=pl.BlockSpec((None, SAMPLE_Q, step_w), lambda b, j, pt: (b, 0, j)),
        ),
        out_shape=jax.ShapeDtypeStruct((db, SAMPLE_Q, steps * step_w), f32),
        compiler_params=_params("arbitrary", "arbitrary"),
        name="sample_scores",
    )(page_table, qi_s, w_s, ki_new_t, *([idx_cache_t] * PAGES_PER_STEP))


def _sample_threshold_kernel(keys_in, keys_out, t_out, *, ksel):
    rows, width = keys_in.shape
    nch = width // KEY_CHUNK
    keys_out[...] = keys_in[...]
    load = lambda c: keys_out[:, c * KEY_CHUNK:(c + 1) * KEY_CHUNK]

    def store(c, val):
        keys_out[:, c * KEY_CHUNK:(c + 1) * KEY_CHUNK] = val
    t, c_ge = _kth_largest_score(load, nch, rows, ksel, True)
    _demote_excess_ties(load, store, nch, rows, ksel, t, c_ge)
    t_out[...] = jnp.broadcast_to(t, t_out.shape)


def _sample_threshold(keys2d, ksel):
    n, width = keys2d.shape
    rows = 128
    return pl.pallas_call(
        functools.partial(_sample_threshold_kernel, ksel=ksel),
        grid=(n // rows,),
        in_specs=[pl.BlockSpec((rows, width), lambda i: (i, 0))],
        out_specs=[pl.BlockSpec((rows, width), lambda i: (i, 0)),
                   pl.BlockSpec((rows, LANES), lambda i: (i, 0))],
        out_shape=[jax.ShapeDtypeStruct((n, width), f32),
                   jax.ShapeDtypeStruct((n, LANES), f32)],
        compiler_params=_params("arbitrary"),
        name="sample_threshold",
    )(keys2d)


def _sample_attn_kernel(pt_ref, q_ref, sc_ref, t_ref, kn_ref, vn_ref, *refs):
    n = PAGES_PER_STEP
    k_pages, v_pages = refs[:n], refs[n:2 * n]
    o_ref, m_s, l_s, acc_s = refs[2 * n:]
    j = pl.program_id(1)
    last = pl.num_programs(1) - 1
    group_rows = 2 * SAMPLE_Q

    @pl.when(j == 0)
    def _():
        m_s[...] = jnp.full(m_s.shape, NEG_BIG, f32)
        l_s[...] = jnp.zeros(l_s.shape, f32)
        acc_s[...] = jnp.zeros(acc_s.shape, f32)

    t = t_ref[:, 0:1]

    def update(kt, vt, sc):
        bias = jnp.where(sc >= t, 0.0, NEG_BIG)
        bias = jnp.concatenate([bias, bias], axis=0)
        lg = jnp.concatenate(
            [jnp.dot(q_ref[h * group_rows:(h + 1) * group_rows, :], kt[h],
                     preferred_element_type=f32) + bias for h in range(N_KV_HEADS)], axis=0)
        m_old = m_s[...]
        m_new = jnp.maximum(m_old, jnp.max(lg, axis=-1, keepdims=True))
        alpha = jnp.exp(m_old - m_new)
        pr = jnp.exp(lg - m_new[:, 0:1])
        l_s[...] = alpha * l_s[...] + jnp.sum(pr, axis=-1, keepdims=True)
        pr = pr.astype(bf16)
        pv = jnp.concatenate(
            [lax.dot_general(pr[h * group_rows:(h + 1) * group_rows], vt[h],
                             (((1,), (1,)), ((), ())), preferred_element_type=f32)
             for h in range(N_KV_HEADS)], axis=0)
        acc_s[...] = acc_s[...] * alpha[:, 0:HEAD_DIM] + pv
        m_s[...] = m_new

    def lanes_of(page_refs):
        return jnp.concatenate([r[...] for r in page_refs], axis=2).astype(bf16)

    @pl.when(j < last)
    def _():
        update(lanes_of(k_pages), lanes_of(v_pages), sc_ref[...])

    @pl.when(j == last)
    def _():
        update(kn_ref[...], vn_ref[...], sc_ref[:, 0:PAGE])
        o = acc_s[...] / l_s[:, 0:HEAD_DIM]
        for h in range(N_HEADS):
            o_ref[:, h * HEAD_DIM:(h + 1) * HEAD_DIM] = o[h * SAMPLE_Q:(h + 1) * SAMPLE_Q]


def _sample_attention(page_table, q_s, scores3d, t3d, k_new_t, v_new_t, k_cache_t, v_cache_t,
                      layer):
    db, n_pages = page_table.shape
    steps = n_pages // PAGES_PER_STEP + 1
    per_b = lambda b, j, pt: (b, 0, 0)
    per_b4 = lambda b, j, pt: (b, 0, 0, 0)
    step_w = PAGES_PER_STEP * PAGE
    page_block = (None, None, N_KV_HEADS, HEAD_DIM, PAGE)
    return pl.pallas_call(
        _sample_attn_kernel,
        grid_spec=pltpu.PrefetchScalarGridSpec(
            num_scalar_prefetch=1,
            grid=(db, steps),
            in_specs=[pl.BlockSpec((None, SAMPLE_ROWS, HEAD_DIM), per_b),
                      pl.BlockSpec((None, SAMPLE_Q, step_w), lambda b, j, pt: (b, 0, j)),
                      pl.BlockSpec((None, SAMPLE_Q, LANES), per_b),
                      pl.BlockSpec((None, N_KV_HEADS, HEAD_DIM, PAGE), per_b4),
                      pl.BlockSpec((None, N_KV_HEADS, HEAD_DIM, PAGE), per_b4)]
                     + [_page_spec(page_block, layer, n, n_pages)
                        for n in range(PAGES_PER_STEP)] * 2,
            out_specs=pl.BlockSpec((None, SAMPLE_Q, ATTN_WIDTH), per_b),
            scratch_shapes=[pltpu.VMEM((SAMPLE_ROWS, LANES), f32),
                            pltpu.VMEM((SAMPLE_ROWS, LANES), f32),
                            pltpu.VMEM((SAMPLE_ROWS, HEAD_DIM), f32)],
        ),
        out_shape=jax.ShapeDtypeStruct((db, SAMPLE_Q, ATTN_WIDTH), f32),
        compiler_params=_params("arbitrary", "arbitrary"),
        name="sample_attention",
    )(page_table, q_s, scores3d, t3d, k_new_t, v_new_t,
      *([k_cache_t] * PAGES_PER_STEP), *([v_cache_t] * PAGES_PER_STEP))


def _pool_diff_prompt(u_ref, hist_ref, seq):
    tm = u_ref.shape[0]
    start = (pl.program_id(0) * tm) % seq
    u = u_ref[...]
    hist = hist_ref[...] * jnp.where(start == 0, 0.0, 1.0)
    ext = jnp.concatenate([hist, u], axis=0)
    pos = start + lax.broadcasted_iota(i32, (tm, 1), 0)
    outs = []
    for g, win in enumerate(POOL_WINDOWS):
        s = ext[:, g * POOL_GROUP:(g + 1) * POOL_GROUP]
        span = 1
        while span < win:
            s = s + pltpu.roll(s, span, 0)
            span *= 2
        cnt = jnp.minimum(pos + 1, win).astype(f32)
        outs.append(s[HIST_ROWS:] / cnt - u[:, g * POOL_GROUP:(g + 1) * POOL_GROUP])
    return outs


def _pool_diff_sample(u_ref):
    nb = u_ref.shape[0]
    outs = []
    for g, win in enumerate(POOL_WINDOWS):
        lanes = slice(g * POOL_GROUP, (g + 1) * POOL_GROUP)
        cur = u_ref[:, HIST_ROWS:HIST_ROWS + 8, lanes]
        s = cur
        for d in range(1, win):
            s = s + u_ref[:, HIST_ROWS - d:HIST_ROWS + 8 - d, lanes]
        outs.append((s / float(win) - cur).reshape(nb * 8, POOL_GROUP))
    return outs


def _merge_kernel(x_ref, ao_ref, u_ref, hist_ref, ga_ref, gb_ref, wpool_ref, pscale_ref,
                  wba_ref, wbp_ref, wout_ref, fg_ref, wr_ref, br_ref,
                  h_ref, hn_ref, route_ref, *, seq):
    diffs = _pool_diff_sample(u_ref) if seq is None else _pool_diff_prompt(u_ref, hist_ref, seq)
    pool = jnp.concatenate(
        [jnp.dot(d.astype(bf16), wpool_ref[g], preferred_element_type=f32)
         for g, d in enumerate(diffs)], axis=1) * pscale_ref[...]
    br_a = jnp.dot(ao_ref[...].astype(bf16), wba_ref[...], preferred_element_type=f32)
    br_p = jnp.dot(pool.astype(bf16), wbp_ref[...], preferred_element_type=f32)
    sig = lambda z: 1.0 / (1.0 + jnp.exp(-z))
    mix = sig(ga_ref[...]) * br_a + sig(gb_ref[...]) * br_p
    h = x_ref[...] + jnp.dot(mix.astype(bf16), wout_ref[...], preferred_element_type=f32)
    h_ref[...] = h
    hn = _rms(h, fg_ref[...])
    hn_ref[...] = hn

    logit = jnp.dot(hn, wr_ref[...], preferred_element_type=f32,
                    precision=lax.Precision.HIGHEST) + br_ref[...]
    lane_i = lax.broadcasted_iota(i32, logit.shape, 1)
    lane = lane_i.astype(f32)
    big = float(LANES)
    gl = jnp.where(lane_i < N_EXPERTS, -jnp.inf,
                   jnp.where(lane_i < N_EXPERTS + N_GROUPS, logit, -jnp.inf))
    gmax = jnp.max(gl, axis=-1, keepdims=True)
    g_sel = jnp.min(jnp.where(gl == gmax, lane, big), axis=-1, keepdims=True) - N_EXPERTS
    g_w = 1.0 / jnp.sum(jnp.exp(gl - gmax), axis=-1, keepdims=True)
    group_of_lane = (lane_i // EXPERTS_PER_GROUP).astype(f32)
    el = jnp.where(group_of_lane == g_sel, logit, -jnp.inf)
    v1 = jnp.max(el, axis=-1, keepdims=True)
    i1 = jnp.min(jnp.where(el == v1, lane, big), axis=-1, keepdims=True)
    el2 = jnp.where(lane == i1, -jnp.inf, el)
    v2 = jnp.max(el2, axis=-1, keepdims=True)
    i2 = jnp.min(jnp.where(el2 == v2, lane, big), axis=-1, keepdims=True)
    e21 = jnp.exp(v2 - v1)
    w1 = g_w / (1.0 + e21)
    w2 = g_w * e21 / (1.0 + e21)
    route_ref[...] = jnp.where(lane_i == 0, i1,
                     jnp.where(lane_i == 1, i2,
                     jnp.where(lane_i == 2, w1, jnp.where(lane_i == 3, w2, 0.0))))


def _merge(x2d, ao, u, ga, gb, lw, seq):
    n = x2d.shape[0]
    tm = PROJ_ROWS
    row = lambda i: (i, 0)
    const2 = lambda i: (0, 0)
    const3 = lambda i: (0, 0, 0)
    if seq is None:
        u_spec = pl.BlockSpec((tm // 8, HIST_ROWS + 8, POOL_WIDTH), lambda i: (i, 0, 0))
        hist = jnp.zeros((HIST_ROWS, POOL_WIDTH), f32)
        hist_spec = pl.BlockSpec((HIST_ROWS, POOL_WIDTH), const2)
    else:
        u_spec = pl.BlockSpec((tm, POOL_WIDTH), row)
        hist = u
        hist_spec = pl.BlockSpec((HIST_ROWS, POOL_WIDTH),
                                 lambda i: (jnp.maximum(i * (tm // HIST_ROWS) - 1, 0), 0))
    return pl.pallas_call(
        functools.partial(_merge_kernel, seq=seq),
        grid=(n // tm,),
        in_specs=[pl.BlockSpec((tm, D_MODEL), row),
                  pl.BlockSpec((tm, ATTN_WIDTH), row),
                  u_spec, hist_spec,
                  pl.BlockSpec((tm, D_MODEL), row),
                  pl.BlockSpec((tm, D_MODEL), row),
                  pl.BlockSpec((4, POOL_GROUP, POOL_GROUP), const3),
                  pl.BlockSpec((1, POOL_WIDTH), const2),
                  pl.BlockSpec((ATTN_WIDTH, D_MODEL), const2),
                  pl.BlockSpec((POOL_WIDTH, D_MODEL), const2),
                  pl.BlockSpec((D_MODEL, D_MODEL), const2),
                  pl.BlockSpec((1, D_MODEL), const2),
                  pl.BlockSpec((D_MODEL, LANES), const2),
                  pl.BlockSpec((1, LANES), const2)],
        out_specs=[pl.BlockSpec((tm, D_MODEL), row),
                   pl.BlockSpec((tm, D_MODEL), row),
                   pl.BlockSpec((tm, LANES), row)],
        out_shape=[jax.ShapeDtypeStruct((n, D_MODEL), f32),
                   jax.ShapeDtypeStruct((n, D_MODEL), f32),
                   jax.ShapeDtypeStruct((n, LANES), f32)],
        compiler_params=_params("arbitrary"),
        name="merge",
    )(x2d, ao, u, hist, ga, gb, lw['w_pool'], lw['pool_scale'], lw['w_ba'], lw['w_bp'],
      lw['w_out'], lw['ffn_g'], lw['w_router'], lw['b_router'])


RANK_ROWS = 512
GATHER_UNROLL = 8


def _rank_kernel(route_ref, rank_ref, counts_ref, base_s):
    @pl.when(pl.program_id(0) == 0)
    def _():
        base_s[...] = jnp.zeros(base_s.shape, f32)
    r = route_ref[...]
    tt = r.shape[0]
    lane = lax.broadcasted_iota(i32, r.shape, 1).astype(f32)
    oh1 = lane == r[:, 0:1]
    oh2 = lane == r[:, 1:2]
    oh = jnp.where(oh1, 1.0, 0.0) + jnp.where(oh2, 1.0, 0.0)
    below = (lax.broadcasted_iota(i32, (tt, tt), 0) > lax.broadcasted_iota(i32, (tt, tt), 1))
    before = jnp.dot(jnp.where(below, 1.0, 0.0).astype(bf16), oh.astype(bf16),
                     preferred_element_type=f32) + base_s[...]
    r1 = jnp.sum(jnp.where(oh1, before, 0.0), axis=-1, keepdims=True)
    r2 = jnp.sum(jnp.where(oh2, before, 0.0), axis=-1, keepdims=True)
    lane_i = lax.broadcasted_iota(i32, r.shape, 1)
    rank_ref[...] = jnp.where(lane_i == 0, r1, jnp.where(lane_i == 1, r2, 0.0))
    base_s[...] = base_s[...] + jnp.sum(oh, axis=0, keepdims=True)
    counts_ref[...] = base_s[...]


def _expert_ranks(route):
    n = route.shape[0]
    tt = min(RANK_ROWS, n)
    return pl.pallas_call(
        _rank_kernel,
        grid=(n // tt,),
        in_specs=[pl.BlockSpec((tt, LANES), lambda i: (i, 0))],
        out_specs=[pl.BlockSpec((tt, LANES), lambda i: (i, 0)),
                   pl.BlockSpec((1, LANES), lambda i: (0, 0))],
        out_shape=[jax.ShapeDtypeStruct((n, LANES), f32),
                   jax.ShapeDtypeStruct((1, LANES), f32)],
        scratch_shapes=[pltpu.VMEM((1, LANES), f32)],
        compiler_params=_params("arbitrary"),
        name="expert_ranks",
    )(route)


INVERT_ROWS = 8192


def _slot_tokens_kernel(dest_ref, zeros_hbm, tok_ref, sem):
    i = pl.program_id(0)
    tile = dest_ref.shape[0]

    @pl.when(i == 0)
    def _():
        fill = pltpu.make_async_copy(zeros_hbm, tok_ref, sem)
        fill.start()
        fill.wait()

    def body(a, carry):
        tok_ref[dest_ref[a]] = (i * tile + a) // TOP_K
        return carry
    lax.fori_loop(0, tile, body, 0, unroll=16)


def _slot_tokens(dest, n_slots):
    n = dest.shape[0]
    tile = min(INVERT_ROWS, n)
    return pl.pallas_call(
        _slot_tokens_kernel,
        grid=(n // tile,),
        in_specs=[pl.BlockSpec((tile,), lambda i: (i,), memory_space=pltpu.SMEM),
                  pl.BlockSpec(memory_space=pl.ANY)],
        out_specs=pl.BlockSpec((n_slots,), lambda i: (0,), memory_space=pltpu.SMEM),
        out_shape=jax.ShapeDtypeStruct((n_slots,), i32),
        scratch_shapes=[pltpu.SemaphoreType.DMA(())],
        compiler_params=_params("arbitrary"),
        name="moe_slot_tokens",
    )(dest, jnp.zeros((n_slots,), i32))


def _expert_ffn_kernel(be_ref, nact_ref, tok_ref, tok_next_ref, x_hbm, wg_ref, wu_ref, wd_ref,
                       ys_ref, xbuf, sem):
    i = pl.program_id(0)
    n_active = nact_ref[0]
    bm = ys_ref.shape[0]
    slot = i % 2

    def gather(tok, s):
        def body(g, carry):
            for j in range(GATHER_UNROLL):
                r = g * GATHER_UNROLL + j
                pltpu.make_async_copy(x_hbm.at[pl.ds(tok[0, r], 1)], xbuf.at[s, pl.ds(r, 1)],
                                      sem.at[s]).start(priority=j % 2)
            return carry
        lax.fori_loop(0, bm // GATHER_UNROLL, body, 0)

    @pl.when(i == 0)
    def _():
        gather(tok_ref, 0)

    @pl.when(i + 1 < n_active)
    def _():
        gather(tok_next_ref, 1 - slot)

    @pl.when(i < n_active)
    def _():
        pltpu.make_async_copy(x_hbm.at[pl.ds(0, bm)], xbuf.at[slot], sem.at[slot]).wait()
        x = xbuf[slot].astype(bf16)
        g = jnp.dot(x, wg_ref[...], preferred_element_type=f32)
        u = jnp.dot(x, wu_ref[...], preferred_element_type=f32)
        hb = (g / (1.0 + jnp.exp(-g))) * u
        ys_ref[...] = jnp.dot(hb.astype(bf16), wd_ref[...], preferred_element_type=f32)

    @pl.when(i >= n_active)
    def _():
        ys_ref[...] = jnp.zeros(ys_ref.shape, f32)


def _expert_ffn(block_expert, n_active, slot_tok, x, wg, wu, wd, bm):
    n_blocks = slot_tok.shape[0] // bm
    tok3 = slot_tok.reshape(n_blocks, 1, bm)
    by_expert = lambda i, be, na: (be[i], 0, 0)
    smem_block = lambda index: pl.BlockSpec((None, 1, bm), index, memory_space=pltpu.SMEM)
    return pl.pallas_call(
        _expert_ffn_kernel,
        grid_spec=pltpu.PrefetchScalarGridSpec(
            num_scalar_prefetch=2,
            grid=(n_blocks,),
            in_specs=[smem_block(lambda i, be, na: (i, 0, 0)),
                      smem_block(lambda i, be, na: (jnp.minimum(i + 1, n_blocks - 1), 0, 0)),
                      pl.BlockSpec(memory_space=pl.ANY),
                      pl.BlockSpec((None, D_MODEL, D_EXPERT), by_expert),
                      pl.BlockSpec((None, D_MODEL, D_EXPERT), by_expert),
                      pl.BlockSpec((None, D_EXPERT, D_MODEL), by_expert)],
            out_specs=pl.BlockSpec((bm, D_MODEL), lambda i, be, na: (i, 0)),
            scratch_shapes=[pltpu.VMEM((2, bm, D_MODEL), f32),
                            pltpu.SemaphoreType.DMA((2,))],
        ),
        out_shape=jax.ShapeDtypeStruct((n_blocks * bm, D_MODEL), f32),
        compiler_params=_params("arbitrary"),
        name="expert_ffn",
    )(block_expert, n_active, tok3, tok3, x, wg, wu, wd)


COMBINE_ROWS = 128


def _combine_kernel(dest_ref, dest_next_ref, h_ref, route_ref, g_ref, ys_hbm, y_ref, buf, sem):
    i = pl.program_id(0)
    tt = h_ref.shape[0]
    slot = i % 2

    def gather(dest, s):
        def body(g, carry):
            for k in range(GATHER_UNROLL // TOP_K):
                t = g * (GATHER_UNROLL // TOP_K) + k
                for j in range(TOP_K):
                    pltpu.make_async_copy(ys_hbm.at[pl.ds(dest[0, TOP_K * t + j], 1)],
                                          buf.at[s, j, pl.ds(t, 1)], sem.at[s]).start(priority=j)
            return carry
        lax.fori_loop(0, tt * TOP_K // GATHER_UNROLL, body, 0)

    @pl.when(i == 0)
    def _():
        gather(dest_ref, 0)

    @pl.when(i + 1 < pl.num_programs(0))
    def _():
        gather(dest_next_ref, 1 - slot)

    for j in range(TOP_K):
        pltpu.make_async_copy(ys_hbm.at[pl.ds(0, tt)], buf.at[slot, j], sem.at[slot]).wait()
    r = route_ref[...]
    moe = buf[slot, 0] * r[:, 2:3] + buf[slot, 1] * r[:, 3:4]
    y_ref[...] = _rms(h_ref[...] + moe, g_ref[...])


def _combine(dest, h, route, final_g, ys):
    n = h.shape[0]
    tt = COMBINE_ROWS
    n_tiles = n // tt
    dest3 = dest.reshape(n_tiles, 1, tt * TOP_K)
    row = lambda i: (i, 0)
    smem_block = lambda index: pl.BlockSpec((None, 1, tt * TOP_K), index, memory_space=pltpu.SMEM)
    return pl.pallas_call(
        _combine_kernel,
        grid=(n_tiles,),
        in_specs=[smem_block(lambda i: (i, 0, 0)),
                  smem_block(lambda i: (jnp.minimum(i + 1, n_tiles - 1), 0, 0)),
                  pl.BlockSpec((tt, D_MODEL), row),
                  pl.BlockSpec((tt, LANES), row),
                  pl.BlockSpec((1, D_MODEL), lambda i: (0, 0)),
                  pl.BlockSpec(memory_space=pl.ANY)],
        out_specs=pl.BlockSpec((tt, D_MODEL), row),
        out_shape=jax.ShapeDtypeStruct((n, D_MODEL), f32),
        scratch_shapes=[pltpu.VMEM((2, TOP_K, tt, D_MODEL), f32),
                        pltpu.SemaphoreType.DMA((2,))],
        compiler_params=_params("arbitrary"),
        name="moe_combine",
    )(dest3, dest3, h, route, final_g, ys)


def _moe_and_final_norm(h, hn, route, lw, final_g, bm):
    n = h.shape[0]
    rank, counts = _expert_ranks(route)
    counts = counts[0, :N_EXPERTS].astype(i32)
    padded = (counts + bm - 1) // bm * bm
    pend = jnp.cumsum(padded)
    pstart = pend - padded
    e_id = route[:, 0:TOP_K].astype(i32)
    dest = (pstart[e_id] + rank[:, 0:TOP_K].astype(i32)).reshape(-1)
    n_blocks = (n * TOP_K + N_EXPERTS * (bm - 1)) // bm + 1
    n_active = (pend[-1] // bm).astype(i32)
    blk = jnp.minimum(jnp.arange(n_blocks, dtype=i32), n_active - 1) * bm
    block_expert = jnp.minimum(jnp.sum((pend[None, :] <= blk[:, None]).astype(i32), axis=1),
                               N_EXPERTS - 1)
    slot_tok = _slot_tokens(dest, n_blocks * bm)
    ys = _expert_ffn(block_expert, n_active.reshape(1), slot_tok, hn,
                     lw['w_eg'], lw['w_eu'], lw['w_ed'], bm)
    return _combine(dest, h, route, final_g, ys)


def _rope_tables(pos, reps):
    half = HEAD_DIM // 2
    inv_freq = 1.0 / (ROPE_THETA ** (jnp.arange(half, dtype=f32) / half))
    ang = pos.astype(f32)[:, None] * inv_freq[None, :]
    cos, sin = jnp.cos(ang), jnp.sin(ang)
    cos = jnp.concatenate([cos, cos, cos, cos], axis=1)
    sin = jnp.concatenate([-sin, sin, -sin, sin], axis=1)
    return jnp.tile(cos, (reps, 1)), jnp.tile(sin, (reps, 1))


def _layer_weights(l, attn_norm_g, w_in, w_pool, pool_scale, w_branch_attn, w_branch_pool, w_out,
                   ffn_norm_g, w_rg, b_rg, w_re, b_re, w_eg, w_eu, w_ed):
    cuts = [0]
    for s in SEGMENTS:
        cuts.append(cuts[-1] + s)
    w = w_in[l]
    kw_pad = LANES - IDX_DIM - N_IDX_HEADS
    w_packed = jnp.concatenate(
        [w[:, cuts[0]:cuts[4]], w[:, cuts[4]:cuts[6]], jnp.zeros((D_MODEL, kw_pad), f32),
         w[:, cuts[6]:]], axis=1).astype(bf16)
    r_pad = LANES - N_EXPERTS - N_GROUPS
    return dict(
        attn_g=attn_norm_g[l].reshape(1, D_MODEL), w_packed=w_packed,
        w_pool=w_pool[l].astype(bf16), pool_scale=pool_scale[l].reshape(1, POOL_WIDTH),
        w_ba=w_branch_attn[l].astype(bf16), w_bp=w_branch_pool[l].astype(bf16),
        w_out=w_out[l].astype(bf16), ffn_g=ffn_norm_g[l].reshape(1, D_MODEL),
        w_router=jnp.concatenate([w_re[l], w_rg[l], jnp.zeros((D_MODEL, r_pad), f32)], axis=1),
        b_router=jnp.concatenate([b_re[l], b_rg[l], jnp.zeros((r_pad,), f32)]).reshape(1, LANES),
        w_eg=w_eg[l].astype(bf16), w_eu=w_eu[l].astype(bf16), w_ed=w_ed[l].astype(bf16))


def _prompt_layer(x, lw, final_g):
    batch, seq, _ = x.shape
    x2d = x.reshape(batch * seq, D_MODEL)
    cos, sin = _rope_tables(jnp.arange(seq), 1)
    q, k, v, qi, kw, u, ga, gb = _project(x2d, lw['attn_g'], lw['w_packed'], cos, sin)
    ao = _prompt_attention(qi, q, kw, k, v, batch, seq, min(TOPK_MAX, seq // 4))
    h, hn, route = _merge(x2d, ao, u, ga, gb, lw, seq)
    y = _moe_and_final_norm(h, hn, route, lw, final_g, bm=256)
    return (y.reshape(batch, seq, D_MODEL),
            k.reshape(batch, seq, N_KV_HEADS, HEAD_DIM), v.reshape(batch, seq, N_KV_HEADS, HEAD_DIM),
            kw[:, :IDX_DIM].reshape(batch, seq, IDX_DIM),
            u.reshape(batch, seq, POOL_WIDTH)[:, seq - POOL_HIST:])


def _sample_layer(x, l, cache_k, cache_v, cache_idx_k, state_pool, page_table, lw, final_g):
    db, ds, _ = x.shape
    page = cache_k.shape[2]
    past = page_table.shape[1] * page
    x2d = x.reshape(db * ds, D_MODEL)
    cos, sin = _rope_tables(past + jnp.arange(ds), PROJ_ROWS // ds)
    q, k, v, qi, kw, u, ga, gb = _project(x2d, lw['attn_g'], lw['w_packed'], cos, sin)

    rows_hq = lambda a, nh, d: a.reshape(db, ds, nh, d).transpose(0, 2, 1, 3).reshape(db, nh * ds, d)
    qi_s = rows_hq(qi, N_IDX_HEADS, IDX_DIM)
    w_s = kw[:, IDX_DIM:IDX_DIM + N_IDX_HEADS].reshape(db, ds, N_IDX_HEADS)
    w_s = w_s.transpose(0, 2, 1).reshape(db, N_IDX_HEADS * ds, 1)
    pad_slots = lambda a: jnp.pad(a, [(0, 0)] * (a.ndim - 1) + [(0, page - ds)])
    ki_new_t = pad_slots(kw[:, :IDX_DIM].reshape(db, ds, IDX_DIM).transpose(0, 2, 1))
    scores = _sample_scores(page_table, qi_s, w_s, ki_new_t,
                            cache_idx_k.transpose(0, 1, 3, 2), l)
    width = scores.shape[-1]
    scores, t = _sample_threshold(scores.reshape(db * ds, width),
                                  min(TOPK_MAX, (past + ds) // 4))

    as_page = lambda a: pad_slots(
        a.reshape(db, ds, N_KV_HEADS, HEAD_DIM).transpose(0, 2, 3, 1)).astype(bf16)
    ao = _sample_attention(page_table, rows_hq(q, N_HEADS, HEAD_DIM),
                           scores.reshape(db, ds, width), t.reshape(db, ds, LANES),
                           as_page(k), as_page(v),
                           cache_k.transpose(0, 1, 3, 4, 2), cache_v.transpose(0, 1, 3, 4, 2), l)

    u3 = u.reshape(db, ds, POOL_WIDTH)
    u_all = jnp.concatenate([jnp.zeros((db, 1, POOL_WIDTH), f32), state_pool[l], u3], axis=1)
    h, hn, route = _merge(x2d, ao.reshape(db * ds, ATTN_WIDTH), u_all, ga, gb, lw, None)
    y = _moe_and_final_norm(h, hn, route, lw, final_g, bm=128)
    return (y.reshape(db, ds, D_MODEL),
            k.reshape(db, ds, N_KV_HEADS, HEAD_DIM), v.reshape(db, ds, N_KV_HEADS, HEAD_DIM),
            kw[:, :IDX_DIM].reshape(db, ds, IDX_DIM),
            u_all[:, -POOL_HIST:])


def kernel(x_prompt, x_sample, cache_k, cache_v, cache_idx_k, state_pool, page_table, attn_norm_g, w_in, w_pool, pool_scale, w_branch_attn, w_branch_pool, w_out, ffn_norm_g, w_router_group, b_router_group, w_router_expert, b_router_expert, w_exp_gate, w_exp_up, w_exp_down, final_norm_g):
    depth = w_in.shape[0]
    assert depth == 1, "the final norm is fused into the (single) layer's MoE combine"
    final_g = final_norm_g.reshape(1, D_MODEL)
    lw = _layer_weights(0, attn_norm_g, w_in, w_pool, pool_scale, w_branch_attn, w_branch_pool,
                        w_out, ffn_norm_g, w_router_group, b_router_group, w_router_expert,
                        b_router_expert, w_exp_gate, w_exp_up, w_exp_down)
    yp, kp, vp, kip, up = _prompt_layer(x_prompt, lw, final_g)
    ys, ks, vs, kis, us = _sample_layer(x_sample, 0, cache_k, cache_v, cache_idx_k, state_pool,
                                        page_table, lw, final_g)
    stack = lambda a: a[None]
    return (yp, ys, stack(kp), stack(vp), stack(kip), stack(up),
            stack(ks), stack(vs), stack(kis), stack(us))
```

```python
import functools

import jax
import jax.numpy as jnp
from jax import lax
from jax.experimental import pallas as pl
from jax.experimental.pallas import tpu as pltpu

D_MODEL = 1024
HEAD_DIM = 64
N_HEADS = 8
N_KV_HEADS = 4
ATTN_WIDTH = N_HEADS * HEAD_DIM
KV_WIDTH = N_KV_HEADS * HEAD_DIM
N_IDX_HEADS = 8
IDX_DIM = 64
TOPK_MAX = 256
ROPE_THETA = 10000.0
POOL_WINDOWS = (2, 4, 8, 16)
POOL_WIDTH = 512
POOL_GROUP = 128
POOL_HIST = 15
N_GROUPS = 4
EXPERTS_PER_GROUP = 8
N_EXPERTS = 32
TOP_K = 2
D_EXPERT = 512
RMS_EPS = 1e-6
SEGMENTS = (ATTN_WIDTH, KV_WIDTH, KV_WIDTH, N_IDX_HEADS * IDX_DIM, IDX_DIM, N_IDX_HEADS,
            POOL_WIDTH, D_MODEL, D_MODEL)

LANES = 128
VMEM_LIMIT = 56 * 1024 * 1024
KEY_CHUNK = 256
Q_ROWS = 256
PROJ_ROWS = 256
HIST_ROWS = 16
NEG_BIG = -1e30
INT_MIN = -2 ** 31
KEY_NEG_INF = -2139095041
FLT_LOWEST = -3.4028234663852886e38
SUM_EXP_FLOOR = 1e-22

_SEG = dict(q=(0, 512), k=(512, 768), v=(768, 1024), qi=(1024, 1536), kw=(1536, 1664),
            u=(1664, 2176), ga=(2176, 3200), gb=(3200, 4224))
PROJ_PACKED = 4224

f32 = jnp.float32
bf16 = jnp.bfloat16
i32 = jnp.int32


def _params(*sem):
    return pltpu.CompilerParams(dimension_semantics=sem, vmem_limit_bytes=VMEM_LIMIT)


def _rms(x, g):
    return x * lax.rsqrt(jnp.mean(x * x, axis=-1, keepdims=True) + RMS_EPS) * g


def _key_to_float(k):
    return lax.bitcast_convert_type(k ^ ((k >> 31) & 0x7FFFFFFF), f32)


def _rope(x, cos, sin_signed):
    lane = lax.broadcasted_iota(i32, (x.shape[0], LANES), 1)
    first_half = (lane % HEAD_DIM) < HEAD_DIM // 2
    outs = []
    for c in range(x.shape[1] // LANES):
        xc = x[:, c * LANES:(c + 1) * LANES]
        partner = jnp.where(first_half, pltpu.roll(xc, LANES - HEAD_DIM // 2, 1),
                            pltpu.roll(xc, HEAD_DIM // 2, 1))
        outs.append(xc * cos + partner * sin_signed)
    return outs[0] if len(outs) == 1 else jnp.concatenate(outs, axis=1)


def _proj_kernel(x_ref, g_ref, w_ref, cos_ref, sin_ref,
                 q_ref, k_ref, v_ref, qi_ref, kw_ref, u_ref, ga_ref, gb_ref):
    xb = _rms(x_ref[...], g_ref[...]).astype(bf16)
    cos = cos_ref[...]
    sin = sin_ref[...]

    def seg(name):
        a, b = _SEG[name]
        return jnp.dot(xb, w_ref[:, a:b], preferred_element_type=f32)

    q_ref[...] = (_rope(seg('q'), cos, sin) * (HEAD_DIM ** -0.5)).astype(bf16)
    k_ref[...] = _rope(seg('k'), cos, sin)
    v_ref[...] = seg('v')
    qi_ref[...] = (_rope(seg('qi'), cos, sin) * (IDX_DIM ** -0.5)).astype(bf16)
    kw = seg('kw')
    lane = lax.broadcasted_iota(i32, kw.shape, 1)
    kw_ref[...] = jnp.where(lane < IDX_DIM, _rope(kw, cos, sin), kw * (N_IDX_HEADS ** -0.5))
    u_ref[...] = seg('u')
    ga_ref[...] = seg('ga')
    gb_ref[...] = seg('gb')


def _project(x2d, g, w_packed, cos_tab, sin_tab):
    n = x2d.shape[0]
    tm = PROJ_ROWS
    ntab = cos_tab.shape[0] // tm
    row = lambda i: (i, 0)
    const = lambda i: (0, 0)
    tab = lambda i: (i % ntab, 0)
    widths = (512, 256, 256, 512, 128, 512, 1024, 1024)
    dtypes = (bf16, f32, f32, bf16, f32, f32, f32, f32)
    return pl.pallas_call(
        _proj_kernel,
        grid=(n // tm,),
        in_specs=[pl.BlockSpec((tm, D_MODEL), row),
                  pl.BlockSpec((1, D_MODEL), const),
                  pl.BlockSpec((D_MODEL, PROJ_PACKED), const),
                  pl.BlockSpec((tm, LANES), tab),
                  pl.BlockSpec((tm, LANES), tab)],
        out_specs=[pl.BlockSpec((tm, w), row) for w in widths],
        out_shape=[jax.ShapeDtypeStruct((n, w), d) for w, d in zip(widths, dtypes)],
        compiler_params=_params("arbitrary"),
        name="project",
    )(x2d, g, w_packed, cos_tab, sin_tab)


def _count(load_chunk, nch, rows, pred):
    def body(c, acc):
        ind = jnp.where(pred(load_chunk(c), c), 1.0, 0.0)
        for s in range(KEY_CHUNK // LANES):
            acc = acc + ind[:, s * LANES:(s + 1) * LANES]
        return acc
    acc = jnp.zeros((rows, LANES), f32)
    if isinstance(nch, int):
        for c in range(nch):
            acc = body(c, acc)
    else:
        acc = lax.fori_loop(0, nch, body, acc)
    return jnp.sum(acc, axis=-1, keepdims=True)


def _kth_largest_score(load_chunk, nch, rows, ksel, search, zero_counts=None):
    if zero_counts is None:
        zero_counts = (_count(load_chunk, nch, rows, lambda sc, _: sc > 0.0),
                       _count(load_chunk, nch, rows, lambda sc, _: sc >= 0.0))
    c_pos, c_nonneg = zero_counts
    at_zero = jnp.logical_and(c_pos < ksel, c_nonneg >= ksel)

    def cond(state):
        it, _, cnt = state
        return jnp.logical_and(it < 32, jnp.max(jnp.where(at_zero, 0.0, cnt)) > ksel)

    def body(state):
        it, key, cnt = state
        cand = key + lax.shift_left(jnp.int32(1), 31 - it)
        cand_f = _key_to_float(cand)
        c = _count(load_chunk, nch, rows, lambda sc, _: sc >= cand_f)
        take = c >= ksel
        return it + 1, jnp.where(take, cand, key), jnp.where(take, c, cnt)

    total = jnp.where(search, nch * KEY_CHUNK, 0).astype(f32)
    _, key, cnt = lax.while_loop(
        cond, body, (jnp.int32(0), jnp.full((rows, 1), INT_MIN, i32), jnp.full((rows, 1), total)))
    lifted = key <= KEY_NEG_INF
    t = jnp.where(lifted, FLT_LOWEST, _key_to_float(key))
    cnt = jnp.where(lifted, 0.0, cnt)
    at_zero = jnp.logical_and(at_zero, search)
    return jnp.where(at_zero, 0.0, t), jnp.where(at_zero, c_nonneg, cnt)


def _demote_excess_ties(load_chunk, store_chunk, nch, rows, ksel, t, c_ge):
    @pl.when(jnp.max(c_ge) > ksel)
    def _():
        need = ksel - _count(load_chunk, nch, rows, lambda sc, c: sc > t)
        upto = (lax.broadcasted_iota(i32, (KEY_CHUNK, KEY_CHUNK), 0)
                <= lax.broadcasted_iota(i32, (KEY_CHUNK, KEY_CHUNK), 1))
        upto = jnp.where(upto, 1.0, 0.0).astype(bf16)

        def rewrite(c, seen):
            sc = load_chunk(c)
            tie = sc == t
            rank = seen + jnp.dot(jnp.where(tie, 1.0, 0.0).astype(bf16), upto,
                                  preferred_element_type=f32)
            store_chunk(c, jnp.where(jnp.where(tie, rank, 0.0) > need, -jnp.inf, sc))
            return rank[:, KEY_CHUNK - 1:KEY_CHUNK]
        seen = jnp.zeros((rows, 1), f32)
        if isinstance(nch, int):
            for c in range(nch):
                seen = rewrite(c, seen)
        else:
            lax.fori_loop(0, nch, rewrite, seen)


def _prompt_attn_kernel(qi_ref, q_ref, kwq_ref, kw_ref, k_ref, v_ref, o_ref,
                        kit2, kt2, vb, kn2, scores, qil, ql, m_s, mx_s, acc_s, *, ksel):
    i = pl.program_id(1)
    seq = kw_ref.shape[0]
    ck = KEY_CHUNK
    half = HEAD_DIM

    @pl.when(i == 0)
    def _prepare_keys():
        def body(c, carry):
            rows = pl.ds(pl.multiple_of(c * ck, ck), ck)
            kit = kw_ref[rows, :].T[0:half].astype(bf16)
            kit2[c, 0:half, :] = kit
            kit2[c, half:2 * half, :] = kit
            kc = k_ref[rows, :]
            for pair in range(2):
                kt = kc[:, pair * LANES:(pair + 1) * LANES].T.astype(bf16)
                for sub in range(2):
                    one = kt[sub * half:(sub + 1) * half]
                    kt2[c, 2 * pair + sub, 0:half, :] = one
                    kt2[c, 2 * pair + sub, half:2 * half, :] = one
                    sq = one.astype(f32)
                    sq = jnp.sum(sq * sq, axis=0, keepdims=True)
                    kn2[2 * pair + sub] = jnp.maximum(kn2[2 * pair + sub], sq)
            vb[c] = v_ref[rows, :].astype(bf16)
            return carry
        kn2[...] = jnp.zeros(kn2.shape, f32)
        lax.fori_loop(0, seq // ck, body, 0)

    lane = lax.broadcasted_iota(i32, (Q_ROWS, LANES), 1)
    low = lane < half
    for p in range(4):
        c = qi_ref[:, p * LANES:(p + 1) * LANES].astype(f32)
        qil[(2 * p) * Q_ROWS:(2 * p + 1) * Q_ROWS, :] = jnp.where(low, c, 0.0).astype(bf16)
        qil[(2 * p + 1) * Q_ROWS:(2 * p + 2) * Q_ROWS, :] = jnp.where(low, 0.0, c).astype(bf16)
        c = q_ref[:, p * LANES:(p + 1) * LANES].astype(f32)
        ql[p, 0:Q_ROWS, :] = jnp.where(low, c, 0.0).astype(bf16)
        ql[p, Q_ROWS:2 * Q_ROWS, :] = jnp.where(low, 0.0, c).astype(bf16)
    w_head = [kwq_ref[:, IDX_DIM + h:IDX_DIM + h + 1] for h in range(N_IDX_HEADS)]

    nch = ((i + 1) * Q_ROWS + ck - 1) // ck
    row_id = i * Q_ROWS + lax.broadcasted_iota(i32, (Q_ROWS, ck), 0)

    def lane_fold(x):
        return sum(x[:, j * LANES:(j + 1) * LANES] for j in range(ck // LANES))

    def score_chunk(c, carry):
        n_pos, n_nonneg = carry
        s = jnp.dot(qil[...], kit2[c], preferred_element_type=f32)
        acc = None
        for h in range(N_IDX_HEADS):
            term = jnp.maximum(s[h * Q_ROWS:(h + 1) * Q_ROWS], 0.0) * w_head[h]
            acc = term if acc is None else acc + term
        col = c * ck + lax.broadcasted_iota(i32, (Q_ROWS, ck), 1)
        sc = jnp.where(col <= row_id, acc, -jnp.inf)
        scores[c] = sc
        return (n_pos + lane_fold(jnp.where(sc > 0.0, 1.0, 0.0)),
                n_nonneg + lane_fold(jnp.where(sc >= 0.0, 1.0, 0.0)))
    zeros = jnp.zeros((Q_ROWS, LANES), f32)
    n_pos, n_nonneg = lax.fori_loop(0, nch, score_chunk, (zeros, zeros))
    zero_counts = (jnp.sum(n_pos, axis=-1, keepdims=True),
                   jnp.sum(n_nonneg, axis=-1, keepdims=True))

    load = lambda c: scores[c]

    def store(c, val):
        scores[c] = val
    t, c_ge = _kth_largest_score(load, nch, Q_ROWS, ksel, (i + 1) * Q_ROWS > ksel, zero_counts)
    _demote_excess_ties(load, store, nch, Q_ROWS, ksel, t, c_ge)

    def masked_logits(c, p, bias2):
        return jnp.dot(ql[p], kt2[c, p], preferred_element_type=f32) + bias2

    def bias_of(c):
        bias = jnp.where(scores[c] >= t, 0.0, NEG_BIG)
        return jnp.concatenate([bias, bias], axis=0)

    def attend_sweep():
        mx_s[...] = jnp.zeros(mx_s.shape, f32)
        acc_s[...] = jnp.zeros(acc_s.shape, f32)

        def attend_chunk(c, carry):
            bias2 = bias_of(c)
            for p in range(N_KV_HEADS):
                m = m_s[p]
                pr = jnp.exp(masked_logits(c, p, bias2) - jnp.concatenate([m, m], axis=1))
                mx_s[p] = mx_s[p] + pr
                acc_s[p] = acc_s[p] + jnp.dot(
                    pr.astype(bf16), vb[c, :, (p // 2) * LANES:(p // 2 + 1) * LANES],
                    preferred_element_type=f32)
            return carry
        lax.fori_loop(0, nch, attend_chunk, 0)

    smallest = None
    for p in range(N_KV_HEADS):
        qf = ql[p].astype(f32)
        qn2 = jnp.sum(qf * qf, axis=-1, keepdims=True)
        bound = jnp.sqrt(qn2 * jnp.max(kn2[p], axis=-1, keepdims=True))
        m_s[p] = jnp.broadcast_to(bound, m_s.shape[1:])
    attend_sweep()
    for p in range(N_KV_HEADS):
        low_p = jnp.min(jnp.sum(mx_s[p], axis=-1, keepdims=True))
        smallest = low_p if smallest is None else jnp.minimum(smallest, low_p)

    @pl.when(smallest < SUM_EXP_FLOOR)
    def _exact_maxima():
        mx_s[...] = jnp.full(mx_s.shape, NEG_BIG, f32)

        def max_chunk(c, carry):
            bias2 = bias_of(c)
            for p in range(N_KV_HEADS):
                mx_s[p] = jnp.maximum(mx_s[p], masked_logits(c, p, bias2))
            return carry
        lax.fori_loop(0, nch, max_chunk, 0)
        for p in range(N_KV_HEADS):
            m_s[p] = jnp.broadcast_to(jnp.max(mx_s[p], axis=-1, keepdims=True), m_s.shape[1:])
        attend_sweep()

    for p in range(N_KV_HEADS):
        o = acc_s[p] / jnp.sum(mx_s[p], axis=-1, keepdims=True)
        a, b = o[0:Q_ROWS], o[Q_ROWS:2 * Q_ROWS]
        if p % 2 == 0:
            b = pltpu.roll(b, half, 1)
        else:
            a = pltpu.roll(a, half, 1)
        o_ref[:, p * LANES:(p + 1) * LANES] = jnp.where(low, a, b).astype(bf16)


def _prompt_attention(qi, q, kw, k, v, batch, seq, ksel):
    nqb = seq // Q_ROWS
    nck = seq // KEY_CHUNK
    qrow = lambda b, i: (b * nqb + i, 0)
    whole = lambda b, i: (b, 0)
    return pl.pallas_call(
        functools.partial(_prompt_attn_kernel, ksel=ksel),
        grid=(batch, nqb),
        in_specs=[pl.BlockSpec((Q_ROWS, 512), qrow),
                  pl.BlockSpec((Q_ROWS, 512), qrow),
                  pl.BlockSpec((Q_ROWS, LANES), qrow),
                  pl.BlockSpec((seq, LANES), whole),
                  pl.BlockSpec((seq, KV_WIDTH), whole),
                  pl.BlockSpec((seq, KV_WIDTH), whole)],
        out_specs=pl.BlockSpec((Q_ROWS, ATTN_WIDTH), qrow),
        out_shape=jax.ShapeDtypeStruct((batch * seq, ATTN_WIDTH), bf16),
        scratch_shapes=[pltpu.VMEM((nck, LANES, KEY_CHUNK), bf16),
                        pltpu.VMEM((nck, N_KV_HEADS, LANES, KEY_CHUNK), bf16),
                        pltpu.VMEM((nck, KEY_CHUNK, KV_WIDTH), bf16),
                        pltpu.VMEM((N_KV_HEADS, 1, KEY_CHUNK), f32),
                        pltpu.VMEM((nck, Q_ROWS, KEY_CHUNK), f32),
                        pltpu.VMEM((N_IDX_HEADS * Q_ROWS, LANES), bf16),
                        pltpu.VMEM((N_KV_HEADS, 2 * Q_ROWS, LANES), bf16),
                        pltpu.VMEM((N_KV_HEADS, 2 * Q_ROWS, LANES), f32),
                        pltpu.VMEM((N_KV_HEADS, 2 * Q_ROWS, KEY_CHUNK), f32),
                        pltpu.VMEM((N_KV_HEADS, 2 * Q_ROWS, LANES), f32)],
        compiler_params=_params("arbitrary", "arbitrary"),
        name="prompt_attention",
    )(qi, q, kw, kw, k, v)


PAGES_PER_STEP = 16
PAGE = 128
SAMPLE_Q = 8
SAMPLE_ROWS = N_HEADS * SAMPLE_Q


def _sample_score_kernel(pt_ref, qi_ref, w_ref, kin_ref, *refs):
    pages = refs[:PAGES_PER_STEP]
    out_ref = refs[PAGES_PER_STEP]
    j = pl.program_id(1)
    last = pl.num_programs(1) - 1
    qi = qi_ref[...]
    w = w_ref[...]

    def score(kit):
        s = jnp.dot(qi, kit.astype(bf16), preferred_element_type=f32)
        s = jnp.maximum(s, 0.0) * w
        return jnp.sum(s.reshape(N_IDX_HEADS, SAMPLE_Q, s.shape[-1]), axis=0)

    @pl.when(j < last)
    def _():
        out_ref[...] = score(jnp.concatenate([r[...] for r in pages], axis=1))

    @pl.when(j == last)
    def _():
        sc = score(kin_ref[...])
        col = lax.broadcasted_iota(i32, sc.shape, 1)
        row = lax.broadcasted_iota(i32, sc.shape, 0)
        out_ref[...] = jnp.full(out_ref.shape, -jnp.inf, f32)
        out_ref[:, 0:PAGE] = jnp.where(col <= row, sc, -jnp.inf)


def _page_spec(block, layer, n, n_pages):
    def index(b, j, pt):
        page = pt[b, jnp.minimum(j * PAGES_PER_STEP + n, n_pages - 1)]
        return (layer, page) + (0,) * (len(block) - 2)
    return pl.BlockSpec(block, index)


def _sample_scores(page_table, qi_s, w_s, ki_new_t, idx_cache_t, layer):
    db, n_pages = page_table.shape
    steps = n_pages // PAGES_PER_STEP + 1
    per_b = lambda b, j, pt: (b, 0, 0)
    step_w = PAGES_PER_STEP * PAGE
    return pl.pallas_call(
        _sample_score_kernel,
        grid_spec=pltpu.PrefetchScalarGridSpec(
            num_scalar_prefetch=1,
            grid=(db, steps),
            in_specs=[pl.BlockSpec((None, SAMPLE_ROWS, IDX_DIM), per_b),
                      pl.BlockSpec((None, SAMPLE_ROWS, 1), per_b),
                      pl.BlockSpec((None, IDX_DIM, PAGE), per_b)]
                     + [_page_spec((None, None, IDX_DIM, PAGE), layer, n, n_pages)
                        for n in range(PAGES_PER_STEP)],
            out_specs=pl.BlockSpec((None, SAMPLE_Q, step_w), lambda b, j, pt: (b, 0, j)),
        ),
        out_shape=jax.ShapeDtypeStruct((db, SAMPLE_Q, steps * step_w), f32),
        compiler_params=_params("arbitrary", "arbitrary"),
        name="sample_scores",
    )(page_table, qi_s, w_s, ki_new_t, *([idx_cache_t] * PAGES_PER_STEP))


def _sample_threshold_kernel(keys_in, keys_out, t_out, *, ksel):
    rows, width = keys_in.shape
    nch = width // KEY_CHUNK
    keys_out[...] = keys_in[...]
    load = lambda c: keys_out[:, c * KEY_CHUNK:(c + 1) * KEY_CHUNK]

    def store(c, val):
        keys_out[:, c * KEY_CHUNK:(c + 1) * KEY_CHUNK] = val
    t, c_ge = _kth_largest_score(load, nch, rows, ksel, True)
    _demote_excess_ties(load, store, nch, rows, ksel, t, c_ge)
    t_out[...] = jnp.broadcast_to(t, t_out.shape)


def _sample_threshold(keys2d, ksel):
    n, width = keys2d.shape
    rows = 128
    return pl.pallas_call(
        functools.partial(_sample_threshold_kernel, ksel=ksel),
        grid=(n // rows,),
        in_specs=[pl.BlockSpec((rows, width), lambda i: (i, 0))],
        out_specs=[pl.BlockSpec((rows, width), lambda i: (i, 0)),
                   pl.BlockSpec((rows, LANES), lambda i: (i, 0))],
        out_shape=[jax.ShapeDtypeStruct((n, width), f32),
                   jax.ShapeDtypeStruct((n, LANES), f32)],
        compiler_params=_params("arbitrary"),
        name="sample_threshold",
    )(keys2d)


def _sample_attn_kernel(pt_ref, q_ref, sc_ref, t_ref, kn_ref, vn_ref, *refs):
    n = PAGES_PER_STEP
    k_pages, v_pages = refs[:n], refs[n:2 * n]
    o_ref, m_s, l_s, acc_s = refs[2 * n:]
    j = pl.program_id(1)
    last = pl.num_programs(1) - 1
    group_rows = 2 * SAMPLE_Q

    @pl.when(j == 0)
    def _():
        m_s[...] = jnp.full(m_s.shape, NEG_BIG, f32)
        l_s[...] = jnp.zeros(l_s.shape, f32)
        acc_s[...] = jnp.zeros(acc_s.shape, f32)

    t = t_ref[:, 0:1]

    def update(kt, vt, sc):
        bias = jnp.where(sc >= t, 0.0, NEG_BIG)
        bias = jnp.concatenate([bias, bias], axis=0)
        lg = jnp.concatenate(
            [jnp.dot(q_ref[h * group_rows:(h + 1) * group_rows, :], kt[h],
                     preferred_element_type=f32) + bias for h in range(N_KV_HEADS)], axis=0)
        m_old = m_s[...]
        m_new = jnp.maximum(m_old, jnp.max(lg, axis=-1, keepdims=True))
        alpha = jnp.exp(m_old - m_new)
        pr = jnp.exp(lg - m_new[:, 0:1])
        l_s[...] = alpha * l_s[...] + jnp.sum(pr, axis=-1, keepdims=True)
        pr = pr.astype(bf16)
        pv = jnp.concatenate(
            [lax.dot_general(pr[h * group_rows:(h + 1) * group_rows], vt[h],
                             (((1,), (1,)), ((), ())), preferred_element_type=f32)
             for h in range(N_KV_HEADS)], axis=0)
        acc_s[...] = acc_s[...] * alpha[:, 0:HEAD_DIM] + pv
        m_s[...] = m_new

    def lanes_of(page_refs):
        return jnp.concatenate([r[...] for r in page_refs], axis=2).astype(bf16)

    @pl.when(j < last)
    def _():
        update(lanes_of(k_pages), lanes_of(v_pages), sc_ref[...])

    @pl.when(j == last)
    def _():
        update(kn_ref[...], vn_ref[...], sc_ref[:, 0:PAGE])
        o = acc_s[...] / l_s[:, 0:HEAD_DIM]
        for h in range(N_HEADS):
            o_ref[:, h * HEAD_DIM:(h + 1) * HEAD_DIM] = o[h * SAMPLE_Q:(h + 1) * SAMPLE_Q]


def _sample_attention(page_table, q_s, scores3d, t3d, k_new_t, v_new_t, k_cache_t, v_cache_t,
                      layer):
    db, n_pages = page_table.shape
    steps = n_pages // PAGES_PER_STEP + 1
    per_b = lambda b, j, pt: (b, 0, 0)
    per_b4 = lambda b, j, pt: (b, 0, 0, 0)
    step_w = PAGES_PER_STEP * PAGE
    page_block = (None, None, N_KV_HEADS, HEAD_DIM, PAGE)
    return pl.pallas_call(
        _sample_attn_kernel,
        grid_spec=pltpu.PrefetchScalarGridSpec(
            num_scalar_prefetch=1,
            grid=(db, steps),
            in_specs=[pl.BlockSpec((None, SAMPLE_ROWS, HEAD_DIM), per_b),
                      pl.BlockSpec((None, SAMPLE_Q, step_w), lambda b, j, pt: (b, 0, j)),
                      pl.BlockSpec((None, SAMPLE_Q, LANES), per_b),
                      pl.BlockSpec((None, N_KV_HEADS, HEAD_DIM, PAGE), per_b4),
                      pl.BlockSpec((None, N_KV_HEADS, HEAD_DIM, PAGE), per_b4)]
                     + [_page_spec(page_block, layer, n, n_pages)
                        for n in range(PAGES_PER_STEP)] * 2,
            out_specs=pl.BlockSpec((None, SAMPLE_Q, ATTN_WIDTH), per_b),
            scratch_shapes=[pltpu.VMEM((SAMPLE_ROWS, LANES), f32),
                            pltpu.VMEM((SAMPLE_ROWS, LANES), f32),
                            pltpu.VMEM((SAMPLE_ROWS, HEAD_DIM), f32)],
        ),
        out_shape=jax.ShapeDtypeStruct((db, SAMPLE_Q, ATTN_WIDTH), f32),
        compiler_params=_params("arbitrary", "arbitrary"),
        name="sample_attention",
    )(page_table, q_s, scores3d, t3d, k_new_t, v_new_t,
      *([k_cache_t] * PAGES_PER_STEP), *([v_cache_t] * PAGES_PER_STEP))


def _pool_diff_prompt(u_ref, hist_ref, seq):
    tm = u_ref.shape[0]
    start = (pl.program_id(0) * tm) % seq
    u = u_ref[...]
    hist = hist_ref[...] * jnp.where(start == 0, 0.0, 1.0)
    ext = jnp.concatenate([hist, u], axis=0)
    pos = start + lax.broadcasted_iota(i32, (tm, 1), 0)
    outs = []
    for g, win in enumerate(POOL_WINDOWS):
        s = ext[:, g * POOL_GROUP:(g + 1) * POOL_GROUP]
        span = 1
        while span < win:
            s = s + pltpu.roll(s, span, 0)
            span *= 2
        cnt = jnp.minimum(pos + 1, win).astype(f32)
        outs.append(s[HIST_ROWS:] / cnt - u[:, g * POOL_GROUP:(g + 1) * POOL_GROUP])
    return outs


def _pool_diff_sample(u_ref):
    nb = u_ref.shape[0]
    outs = []
    for g, win in enumerate(POOL_WINDOWS):
        lanes = slice(g * POOL_GROUP, (g + 1) * POOL_GROUP)
        cur = u_ref[:, HIST_ROWS:HIST_ROWS + 8, lanes]
        s = cur
        for d in range(1, win):
            s = s + u_ref[:, HIST_ROWS - d:HIST_ROWS + 8 - d, lanes]
        outs.append((s / float(win) - cur).reshape(nb * 8, POOL_GROUP))
    return outs


def _merge_kernel(x_ref, ao_ref, u_ref, hist_ref, ga_ref, gb_ref, wpool_ref, pscale_ref,
                  wba_ref, wbp_ref, wout_ref, fg_ref, wr_ref, br_ref,
                  h_ref, hn_ref, route_ref, *, seq):
    diffs = _pool_diff_sample(u_ref) if seq is None else _pool_diff_prompt(u_ref, hist_ref, seq)
    pool = jnp.concatenate(
        [jnp.dot(d.astype(bf16), wpool_ref[g], preferred_element_type=f32)
         for g, d in enumerate(diffs)], axis=1) * pscale_ref[...]
    br_a = jnp.dot(ao_ref[...].astype(bf16), wba_ref[...], preferred_element_type=f32)
    br_p = jnp.dot(pool.astype(bf16), wbp_ref[...], preferred_element_type=f32)
    sig = lambda z: 1.0 / (1.0 + jnp.exp(-z))
    mix = sig(ga_ref[...]) * br_a + sig(gb_ref[...]) * br_p
    h = x_ref[...] + jnp.dot(mix.astype(bf16), wout_ref[...], preferred_element_type=f32)
    h_ref[...] = h
    hn = _rms(h, fg_ref[...])
    hn_ref[...] = hn

    logit = jnp.dot(hn, wr_ref[...], preferred_element_type=f32,
                    precision=lax.Precision.HIGHEST) + br_ref[...]
    lane_i = lax.broadcasted_iota(i32, logit.shape, 1)
    lane = lane_i.astype(f32)
    big = float(LANES)
    gl = jnp.where(lane_i < N_EXPERTS, -jnp.inf,
                   jnp.where(lane_i < N_EXPERTS + N_GROUPS, logit, -jnp.inf))
    gmax = jnp.max(gl, axis=-1, keepdims=True)
    g_sel = jnp.min(jnp.where(gl == gmax, lane, big), axis=-1, keepdims=True) - N_EXPERTS
    g_w = 1.0 / jnp.sum(jnp.exp(gl - gmax), axis=-1, keepdims=True)
    group_of_lane = (lane_i // EXPERTS_PER_GROUP).astype(f32)
    el = jnp.where(group_of_lane == g_sel, logit, -jnp.inf)
    v1 = jnp.max(el, axis=-1, keepdims=True)
    i1 = jnp.min(jnp.where(el == v1, lane, big), axis=-1, keepdims=True)
    el2 = jnp.where(lane == i1, -jnp.inf, el)
    v2 = jnp.max(el2, axis=-1, keepdims=True)
    i2 = jnp.min(jnp.where(el2 == v2, lane, big), axis=-1, keepdims=True)
    e21 = jnp.exp(v2 - v1)
    w1 = g_w / (1.0 + e21)
    w2 = g_w * e21 / (1.0 + e21)
    route_ref[...] = jnp.where(lane_i == 0, i1,
                     jnp.where(lane_i == 1, i2,
                     jnp.where(lane_i == 2, w1, jnp.where(lane_i == 3, w2, 0.0))))


def _merge(x2d, ao, u, ga, gb, lw, seq):
    n = x2d.shape[0]
    tm = PROJ_ROWS
    row = lambda i: (i, 0)
    const2 = lambda i: (0, 0)
    const3 = lambda i: (0, 0, 0)
    if seq is None:
        u_spec = pl.BlockSpec((tm // 8, HIST_ROWS + 8, POOL_WIDTH), lambda i: (i, 0, 0))
        hist = jnp.zeros((HIST_ROWS, POOL_WIDTH), f32)
        hist_spec = pl.BlockSpec((HIST_ROWS, POOL_WIDTH), const2)
    else:
        u_spec = pl.BlockSpec((tm, POOL_WIDTH), row)
        hist = u
        hist_spec = pl.BlockSpec((HIST_ROWS, POOL_WIDTH),
                                 lambda i: (jnp.maximum(i * (tm // HIST_ROWS) - 1, 0), 0))
    return pl.pallas_call(
        functools.partial(_merge_kernel, seq=seq),
        grid=(n // tm,),
        in_specs=[pl.BlockSpec((tm, D_MODEL), row),
                  pl.BlockSpec((tm, ATTN_WIDTH), row),
                  u_spec, hist_spec,
                  pl.BlockSpec((tm, D_MODEL), row),
                  pl.BlockSpec((tm, D_MODEL), row),
                  pl.BlockSpec((4, POOL_GROUP, POOL_GROUP), const3),
                  pl.BlockSpec((1, POOL_WIDTH), const2),
                  pl.BlockSpec((ATTN_WIDTH, D_MODEL), const2),
                  pl.BlockSpec((POOL_WIDTH, D_MODEL), const2),
                  pl.BlockSpec((D_MODEL, D_MODEL), const2),
                  pl.BlockSpec((1, D_MODEL), const2),
                  pl.BlockSpec((D_MODEL, LANES), const2),
                  pl.BlockSpec((1, LANES), const2)],
        out_specs=[pl.BlockSpec((tm, D_MODEL), row),
                   pl.BlockSpec((tm, D_MODEL), row),
                   pl.BlockSpec((tm, LANES), row)],
        out_shape=[jax.ShapeDtypeStruct((n, D_MODEL), f32),
                   jax.ShapeDtypeStruct((n, D_MODEL), f32),
                   jax.ShapeDtypeStruct((n, LANES), f32)],
        compiler_params=_params("arbitrary"),
        name="merge",
    )(x2d, ao, u, hist, ga, gb, lw['w_pool'], lw['pool_scale'], lw['w_ba'], lw['w_bp'],
      lw['w_out'], lw['ffn_g'], lw['w_router'], lw['b_router'])


RANK_ROWS = 512
GATHER_UNROLL = 8


def _rank_kernel(route_ref, rank_ref, counts_ref, base_s):
    @pl.when(pl.program_id(0) == 0)
    def _():
        base_s[...] = jnp.zeros(base_s.shape, f32)
    r = route_ref[...]
    tt = r.shape[0]
    lane = lax.broadcasted_iota(i32, r.shape, 1).astype(f32)
    oh1 = lane == r[:, 0:1]
    oh2 = lane == r[:, 1:2]
    oh = jnp.where(oh1, 1.0, 0.0) + jnp.where(oh2, 1.0, 0.0)
    below = (lax.broadcasted_iota(i32, (tt, tt), 0) > lax.broadcasted_iota(i32, (tt, tt), 1))
    before = jnp.dot(jnp.where(below, 1.0, 0.0).astype(bf16), oh.astype(bf16),
                     preferred_element_type=f32) + base_s[...]
    r1 = jnp.sum(jnp.where(oh1, before, 0.0), axis=-1, keepdims=True)
    r2 = jnp.sum(jnp.where(oh2, before, 0.0), axis=-1, keepdims=True)
    lane_i = lax.broadcasted_iota(i32, r.shape, 1)
    rank_ref[...] = jnp.where(lane_i == 0, r1, jnp.where(lane_i == 1, r2, 0.0))
    base_s[...] = base_s[...] + jnp.sum(oh, axis=0, keepdims=True)
    counts_ref[...] = base_s[...]


def _expert_ranks(route):
    n = route.shape[0]
    tt = min(RANK_ROWS, n)
    return pl.pallas_call(
        _rank_kernel,
        grid=(n // tt,),
        in_specs=[pl.BlockSpec((tt, LANES), lambda i: (i, 0))],
        out_specs=[pl.BlockSpec((tt, LANES), lambda i: (i, 0)),
                   pl.BlockSpec((1, LANES), lambda i: (0, 0))],
        out_shape=[jax.ShapeDtypeStruct((n, LANES), f32),
                   jax.ShapeDtypeStruct((1, LANES), f32)],
        scratch_shapes=[pltpu.VMEM((1, LANES), f32)],
        compiler_params=_params("arbitrary"),
        name="expert_ranks",
    )(route)


INVERT_ROWS = 8192


def _slot_tokens_kernel(dest_ref, zeros_hbm, tok_ref, sem):
    i = pl.program_id(0)
    tile = dest_ref.shape[0]

    @pl.when(i == 0)
    def _():
        fill = pltpu.make_async_copy(zeros_hbm, tok_ref, sem)
        fill.start()
        fill.wait()

    def body(a, carry):
        tok_ref[dest_ref[a]] = (i * tile + a) // TOP_K
        return carry
    lax.fori_loop(0, tile, body, 0, unroll=16)


def _slot_tokens(dest, n_slots):
    n = dest.shape[0]
    tile = min(INVERT_ROWS, n)
    return pl.pallas_call(
        _slot_tokens_kernel,
        grid=(n // tile,),
        in_specs=[pl.BlockSpec((tile,), lambda i: (i,), memory_space=pltpu.SMEM),
                  pl.BlockSpec(memory_space=pl.ANY)],
        out_specs=pl.BlockSpec((n_slots,), lambda i: (0,), memory_space=pltpu.SMEM),
        out_shape=jax.ShapeDtypeStruct((n_slots,), i32),
        scratch_shapes=[pltpu.SemaphoreType.DMA(())],
        compiler_params=_params("arbitrary"),
        name="moe_slot_tokens",
    )(dest, jnp.zeros((n_slots,), i32))


def _expert_ffn_kernel(be_ref, nact_ref, tok_ref, tok_next_ref, x_hbm, wg_ref, wu_ref, wd_ref,
                       ys_ref, xbuf, sem):
    i = pl.program_id(0)
    n_active = nact_ref[0]
    bm = ys_ref.shape[0]
    slot = i % 2

    def gather(tok, s):
        def body(g, carry):
            for j in range(GATHER_UNROLL):
                r = g * GATHER_UNROLL + j
                pltpu.make_async_copy(x_hbm.at[pl.ds(tok[0, r], 1)], xbuf.at[s, pl.ds(r, 1)],
                                      sem.at[s]).start(priority=j % 2)
            return carry
        lax.fori_loop(0, bm // GATHER_UNROLL, body, 0)

    @pl.when(i == 0)
    def _():
        gather(tok_ref, 0)

    @pl.when(i + 1 < n_active)
    def _():
        gather(tok_next_ref, 1 - slot)

    @pl.when(i < n_active)
    def _():
        pltpu.make_async_copy(x_hbm.at[pl.ds(0, bm)], xbuf.at[slot], sem.at[slot]).wait()
        x = xbuf[slot].astype(bf16)
        g = jnp.dot(x, wg_ref[...], preferred_element_type=f32)
        u = jnp.dot(x, wu_ref[...], preferred_element_type=f32)
        hb = (g / (1.0 + jnp.exp(-g))) * u
        ys_ref[...] = jnp.dot(hb.astype(bf16), wd_ref[...], preferred_element_type=f32)

    @pl.when(i >= n_active)
    def _():
        ys_ref[...] = jnp.zeros(ys_ref.shape, f32)


def _expert_ffn(block_expert, n_active, slot_tok, x, wg, wu, wd, bm):
    n_blocks = slot_tok.shape[0] // bm
    tok3 = slot_tok.reshape(n_blocks, 1, bm)
    by_expert = lambda i, be, na: (be[i], 0, 0)
    smem_block = lambda index: pl.BlockSpec((None, 1, bm), index, memory_space=pltpu.SMEM)
    return pl.pallas_call(
        _expert_ffn_kernel,
        grid_spec=pltpu.PrefetchScalarGridSpec(
            num_scalar_prefetch=2,
            grid=(n_blocks,),
            in_specs=[smem_block(lambda i, be, na: (i, 0, 0)),
                      smem_block(lambda i, be, na: (jnp.minimum(i + 1, n_blocks - 1), 0, 0)),
                      pl.BlockSpec(memory_space=pl.ANY),
                      pl.BlockSpec((None, D_MODEL, D_EXPERT), by_expert),
                      pl.BlockSpec((None, D_MODEL, D_EXPERT), by_expert),
                      pl.BlockSpec((None, D_EXPERT, D_MODEL), by_expert)],
            out_specs=pl.BlockSpec((bm, D_MODEL), lambda i, be, na: (i, 0)),
            scratch_shapes=[pltpu.VMEM((2, bm, D_MODEL), f32),
                            pltpu.SemaphoreType.DMA((2,))],
        ),
        out_shape=jax.ShapeDtypeStruct((n_blocks * bm, D_MODEL), f32),
        compiler_params=_params("arbitrary"),
        name="expert_ffn",
    )(block_expert, n_active, tok3, tok3, x, wg, wu, wd)


COMBINE_ROWS = 128


def _combine_kernel(dest_ref, dest_next_ref, h_ref, route_ref, g_ref, ys_hbm, y_ref, buf, sem):
    i = pl.program_id(0)
    tt = h_ref.shape[0]
    slot = i % 2

    def gather(dest, s):
        def body(g, carry):
            for k in range(GATHER_UNROLL // TOP_K):
                t = g * (GATHER_UNROLL // TOP_K) + k
                for j in range(TOP_K):
                    pltpu.make_async_copy(ys_hbm.at[pl.ds(dest[0, TOP_K * t + j], 1)],
                                          buf.at[s, j, pl.ds(t, 1)], sem.at[s]).start(priority=j)
            return carry
        lax.fori_loop(0, tt * TOP_K // GATHER_UNROLL, body, 0)

    @pl.when(i == 0)
    def _():
        gather(dest_ref, 0)

    @pl.when(i + 1 < pl.num_programs(0))
    def _():
        gather(dest_next_ref, 1 - slot)

    for j in range(TOP_K):
        pltpu.make_async_copy(ys_hbm.at[pl.ds(0, tt)], buf.at[slot, j], sem.at[slot]).wait()
    r = route_ref[...]
    moe = buf[slot, 0] * r[:, 2:3] + buf[slot, 1] * r[:, 3:4]
    y_ref[...] = _rms(h_ref[...] + moe, g_ref[...])


def _combine(dest, h, route, final_g, ys):
    n = h.shape[0]
    tt = COMBINE_ROWS
    n_tiles = n // tt
    dest3 = dest.reshape(n_tiles, 1, tt * TOP_K)
    row = lambda i: (i, 0)
    smem_block = lambda index: pl.BlockSpec((None, 1, tt * TOP_K), index, memory_space=pltpu.SMEM)
    return pl.pallas_call(
        _combine_kernel,
        grid=(n_tiles,),
        in_specs=[smem_block(lambda i: (i, 0, 0)),
                  smem_block(lambda i: (jnp.minimum(i + 1, n_tiles - 1), 0, 0)),
                  pl.BlockSpec((tt, D_MODEL), row),
                  pl.BlockSpec((tt, LANES), row),
                  pl.BlockSpec((1, D_MODEL), lambda i: (0, 0)),
                  pl.BlockSpec(memory_space=pl.ANY)],
        out_specs=pl.BlockSpec((tt, D_MODEL), row),
        out_shape=jax.ShapeDtypeStruct((n, D_MODEL), f32),
        scratch_shapes=[pltpu.VMEM((2, TOP_K, tt, D_MODEL), f32),
                        pltpu.SemaphoreType.DMA((2,))],
        compiler_params=_params("arbitrary"),
        name="moe_combine",
    )(dest3, dest3, h, route, final_g, ys)


def _moe_and_final_norm(h, hn, route, lw, final_g, bm):
    n = h.shape[0]
    rank, counts = _expert_ranks(route)
    counts = counts[0, :N_EXPERTS].astype(i32)
    padded = (counts + bm - 1) // bm * bm
    pend = jnp.cumsum(padded)
    pstart = pend - padded
    e_id = route[:, 0:TOP_K].astype(i32)
    dest = (pstart[e_id] + rank[:, 0:TOP_K].astype(i32)).reshape(-1)
    n_blocks = (n * TOP_K + N_EXPERTS * (bm - 1)) // bm + 1
    n_active = (pend[-1] // bm).astype(i32)
    blk = jnp.minimum(jnp.arange(n_blocks, dtype=i32), n_active - 1) * bm
    block_expert = jnp.minimum(jnp.sum((pend[None, :] <= blk[:, None]).astype(i32), axis=1),
                               N_EXPERTS - 1)
    slot_tok = _slot_tokens(dest, n_blocks * bm)
    ys = _expert_ffn(block_expert, n_active.reshape(1), slot_tok, hn,
                     lw['w_eg'], lw['w_eu'], lw['w_ed'], bm)
    return _combine(dest, h, route, final_g, ys)


def _rope_tables(pos, reps):
    half = HEAD_DIM // 2
    inv_freq = 1.0 / (ROPE_THETA ** (jnp.arange(half, dtype=f32) / half))
    ang = pos.astype(f32)[:, None] * inv_freq[None, :]
    cos, sin = jnp.cos(ang), jnp.sin(ang)
    cos = jnp.concatenate([cos, cos, cos, cos], axis=1)
    sin = jnp.concatenate([-sin, sin, -sin, sin], axis=1)
    return jnp.tile(cos, (reps, 1)), jnp.tile(sin, (reps, 1))


def _layer_weights(l, attn_norm_g, w_in, w_pool, pool_scale, w_branch_attn, w_branch_pool, w_out,
                   ffn_norm_g, w_rg, b_rg, w_re, b_re, w_eg, w_eu, w_ed):
    cuts = [0]
    for s in SEGMENTS:
        cuts.append(cuts[-1] + s)
    w = w_in[l]
    kw_pad = LANES - IDX_DIM - N_IDX_HEADS
    w_packed = jnp.concatenate(
        [w[:, cuts[0]:cuts[4]], w[:, cuts[4]:cuts[6]], jnp.zeros((D_MODEL, kw_pad), f32),
         w[:, cuts[6]:]], axis=1).astype(bf16)
    r_pad = LANES - N_EXPERTS - N_GROUPS
    return dict(
        attn_g=attn_norm_g[l].reshape(1, D_MODEL), w_packed=w_packed,
        w_pool=w_pool[l].astype(bf16), pool_scale=pool_scale[l].reshape(1, POOL_WIDTH),
        w_ba=w_branch_attn[l].astype(bf16), w_bp=w_branch_pool[l].astype(bf16),
        w_out=w_out[l].astype(bf16), ffn_g=ffn_norm_g[l].reshape(1, D_MODEL),
        w_router=jnp.concatenate([w_re[l], w_rg[l], jnp.zeros((D_MODEL, r_pad), f32)], axis=1),
        b_router=jnp.concatenate([b_re[l], b_rg[l], jnp.zeros((r_pad,), f32)]).reshape(1, LANES),
        w_eg=w_eg[l].astype(bf16), w_eu=w_eu[l].astype(bf16), w_ed=w_ed[l].astype(bf16))


def _prompt_layer(x, lw, final_g):
    batch, seq, _ = x.shape
    x2d = x.reshape(batch * seq, D_MODEL)
    cos, sin = _rope_tables(jnp.arange(seq), 1)
    q, k, v, qi, kw, u, ga, gb = _project(x2d, lw['attn_g'], lw['w_packed'], cos, sin)
    ao = _prompt_attention(qi, q, kw, k, v, batch, seq, min(TOPK_MAX, seq // 4))
    h, hn, route = _merge(x2d, ao, u, ga, gb, lw, seq)
    y = _moe_and_final_norm(h, hn, route, lw, final_g, bm=256)
    return (y.reshape(batch, seq, D_MODEL),
            k.reshape(batch, seq, N_KV_HEADS, HEAD_DIM), v.reshape(batch, seq, N_KV_HEADS, HEAD_DIM),
            kw[:, :IDX_DIM].reshape(batch, seq, IDX_DIM),
            u.reshape(batch, seq, POOL_WIDTH)[:, seq - POOL_HIST:])


def _sample_layer(x, l, cache_k, cache_v, cache_idx_k, state_pool, page_table, lw, final_g):
    db, ds, _ = x.shape
    page = cache_k.shape[2]
    past = page_table.shape[1] * page
    x2d = x.reshape(db * ds, D_MODEL)
    cos, sin = _rope_tables(past + jnp.arange(ds), PROJ_ROWS // ds)
    q, k, v, qi, kw, u, ga, gb = _project(x2d, lw['attn_g'], lw['w_packed'], cos, sin)

    rows_hq = lambda a, nh, d: a.reshape(db, ds, nh, d).transpose(0, 2, 1, 3).reshape(db, nh * ds, d)
    qi_s = rows_hq(qi, N_IDX_HEADS, IDX_DIM)
    w_s = kw[:, IDX_DIM:IDX_DIM + N_IDX_HEADS].reshape(db, ds, N_IDX_HEADS)
    w_s = w_s.transpose(0, 2, 1).reshape(db, N_IDX_HEADS * ds, 1)
    pad_slots = lambda a: jnp.pad(a, [(0, 0)] * (a.ndim - 1) + [(0, page - ds)])
    ki_new_t = pad_slots(kw[:, :IDX_DIM].reshape(db, ds, IDX_DIM).transpose(0, 2, 1))
    scores = _sample_scores(page_table, qi_s, w_s, ki_new_t,
                            cache_idx_k.transpose(0, 1, 3, 2), l)
    width = scores.shape[-1]
    scores, t = _sample_threshold(scores.reshape(db * ds, width),
                                  min(TOPK_MAX, (past + ds) // 4))

    as_page = lambda a: pad_slots(
        a.reshape(db, ds, N_KV_HEADS, HEAD_DIM).transpose(0, 2, 3, 1)).astype(bf16)
    ao = _sample_attention(page_table, rows_hq(q, N_HEADS, HEAD_DIM),
                           scores.reshape(db, ds, width), t.reshape(db, ds, LANES),
                           as_page(k), as_page(v),
                           cache_k.transpose(0, 1, 3, 4, 2), cache_v.transpose(0, 1, 3, 4, 2), l)

    u3 = u.reshape(db, ds, POOL_WIDTH)
    u_all = jnp.concatenate([jnp.zeros((db, 1, POOL_WIDTH), f32), state_pool[l], u3], axis=1)
    h, hn, route = _merge(x2d, ao.reshape(db * ds, ATTN_WIDTH), u_all, ga, gb, lw, None)
    y = _moe_and_final_norm(h, hn, route, lw, final_g, bm=128)
    return (y.reshape(db, ds, D_MODEL),
            k.reshape(db, ds, N_KV_HEADS, HEAD_DIM), v.reshape(db, ds, N_KV_HEADS, HEAD_DIM),
            kw[:, :IDX_DIM].reshape(db, ds, IDX_DIM),
            u_all[:, -POOL_HIST:])


def kernel(x_prompt, x_sample, cache_k, cache_v, cache_idx_k, state_pool, page_table, attn_norm_g, w_in, w_pool, pool_scale, w_branch_attn, w_branch_pool, w_out, ffn_norm_g, w_router_group, b_router_group, w_router_expert, b_router_expert, w_exp_gate, w_exp_up, w_exp_down, final_norm_g):
    depth = w_in.shape[0]
    assert depth == 1, "the final norm is fused into the (single) layer's MoE combine"
    final_g = final_norm_g.reshape(1, D_MODEL)
    lw = _layer_weights(0, attn_norm_g, w_in, w_pool, pool_scale, w_branch_attn, w_branch_pool,
                        w_out, ffn_norm_g, w_router_group, b_router_group, w_router_expert,
                        b_router_expert, w_exp_gate, w_exp_up, w_exp_down)
    yp, kp, vp, kip, up = _prompt_layer(x_prompt, lw, final_g)
    ys, ks, vs, kis, us = _sample_layer(x_sample, 0, cache_k, cache_v, cache_idx_k, state_pool,
                                        page_table, lw, final_g)
    stack = lambda a: a[None]
    return (yp, ys, stack(kp), stack(vp), stack(kip), stack(up),
            stack(ks), stack(vs), stack(kis), stack(us))
```

```python
import functools

import jax
import jax.numpy as jnp
from jax import lax
from jax.experimental import pallas as pl
from jax.experimental.pallas import tpu as pltpu

D_MODEL = 1024
HEAD_DIM = 64
N_HEADS = 8
N_KV_HEADS = 4
ATTN_WIDTH = N_HEADS * HEAD_DIM
KV_WIDTH = N_KV_HEADS * HEAD_DIM
N_IDX_HEADS = 8
IDX_DIM = 64
TOPK_MAX = 256
ROPE_THETA = 10000.0
POOL_WINDOWS = (2, 4, 8, 16)
POOL_WIDTH = 512
POOL_GROUP = 128
POOL_HIST = 15
N_GROUPS = 4
EXPERTS_PER_GROUP = 8
N_EXPERTS = 32
TOP_K = 2
D_EXPERT = 512
RMS_EPS = 1e-6
SEGMENTS = (ATTN_WIDTH, KV_WIDTH, KV_WIDTH, N_IDX_HEADS * IDX_DIM, IDX_DIM, N_IDX_HEADS,
            POOL_WIDTH, D_MODEL, D_MODEL)

LANES = 128
VMEM_LIMIT = 56 * 1024 * 1024
KEY_CHUNK = 256
Q_ROWS = 256
PROJ_ROWS = 256
HIST_ROWS = 16
NEG_BIG = -1e30
INT_MIN = -2 ** 31
KEY_NEG_INF = -2139095041
FLT_LOWEST = -3.4028234663852886e38
SUM_EXP_FLOOR = 1e-22

_SEG = dict(q=(0, 512), k=(512, 768), v=(768, 1024), qi=(1024, 1536), kw=(1536, 1664),
            u=(1664, 2176), ga=(2176, 3200), gb=(3200, 4224))
PROJ_PACKED = 4224

f32 = jnp.float32
bf16 = jnp.bfloat16
i32 = jnp.int32


def _params(*sem):
    return pltpu.CompilerParams(dimension_semantics=sem, vmem_limit_bytes=VMEM_LIMIT)


def _rms(x, g):
    return x * lax.rsqrt(jnp.mean(x * x, axis=-1, keepdims=True) + RMS_EPS) * g


def _key_to_float(k):
    return lax.bitcast_convert_type(k ^ ((k >> 31) & 0x7FFFFFFF), f32)


def _rope(x, cos, sin_signed):
    lane = lax.broadcasted_iota(i32, (x.shape[0], LANES), 1)
    first_half = (lane % HEAD_DIM) < HEAD_DIM // 2
    outs = []
    for c in range(x.shape[1] // LANES):
        xc = x[:, c * LANES:(c + 1) * LANES]
        partner = jnp.where(first_half, pltpu.roll(xc, LANES - HEAD_DIM // 2, 1),
                            pltpu.roll(xc, HEAD_DIM // 2, 1))
        outs.append(xc * cos + partner * sin_signed)
    return outs[0] if len(outs) == 1 else jnp.concatenate(outs, axis=1)


def _proj_kernel(x_ref, g_ref, w_ref, cos_ref, sin_ref,
                 q_ref, k_ref, v_ref, qi_ref, kw_ref, u_ref, ga_ref, gb_ref):
    xb = _rms(x_ref[...], g_ref[...]).astype(bf16)
    cos = cos_ref[...]
    sin = sin_ref[...]

    def seg(name):
        a, b = _SEG[name]
        return jnp.dot(xb, w_ref[:, a:b], preferred_element_type=f32)

    q_ref[...] = (_rope(seg('q'), cos, sin) * (HEAD_DIM ** -0.5)).astype(bf16)
    k_ref[...] = _rope(seg('k'), cos, sin)
    v_ref[...] = seg('v')
    qi_ref[...] = (_rope(seg('qi'), cos, sin) * (IDX_DIM ** -0.5)).astype(bf16)
    kw = seg('kw')
    lane = lax.broadcasted_iota(i32, kw.shape, 1)
    kw_ref[...] = jnp.where(lane < IDX_DIM, _rope(kw, cos, sin), kw * (N_IDX_HEADS ** -0.5))
    u_ref[...] = seg('u')
    ga_ref[...] = seg('ga')
    gb_ref[...] = seg('gb')


def _project(x2d, g, w_packed, cos_tab, sin_tab):
    n = x2d.shape[0]
    tm = PROJ_ROWS
    ntab = cos_tab.shape[0] // tm
    row = lambda i: (i, 0)
    const = lambda i: (0, 0)
    tab = lambda i: (i % ntab, 0)
    widths = tuple(b - a for a, b in _SEG.values())
    dtypes = (bf16, f32, f32, bf16, f32, f32, f32, f32)
    return pl.pallas_call(
        _proj_kernel,
        grid=(n // tm,),
        in_specs=[pl.BlockSpec((tm, D_MODEL), row),
                  pl.BlockSpec((1, D_MODEL), const),
                  pl.BlockSpec((D_MODEL, PROJ_PACKED), const),
                  pl.BlockSpec((tm, LANES), tab),
                  pl.BlockSpec((tm, LANES), tab)],
        out_specs=[pl.BlockSpec((tm, w), row) for w in widths],
        out_shape=[jax.ShapeDtypeStruct((n, w), d) for w, d in zip(widths, dtypes)],
        compiler_params=_params("arbitrary"),
        name="project",
    )(x2d, g, w_packed, cos_tab, sin_tab)


def _count(load_chunk, nch, rows, pred):
    def body(c, acc):
        ind = jnp.where(pred(load_chunk(c), c), 1.0, 0.0)
        for s in range(KEY_CHUNK // LANES):
            acc = acc + ind[:, s * LANES:(s + 1) * LANES]
        return acc
    acc = jnp.zeros((rows, LANES), f32)
    if isinstance(nch, int):
        for c in range(nch):
            acc = body(c, acc)
    else:
        acc = lax.fori_loop(0, nch, body, acc)
    return jnp.sum(acc, axis=-1, keepdims=True)


def _kth_largest_score(load_chunk, nch, rows, ksel, search, zero_counts=None):
    if zero_counts is None:
        zero_counts = (_count(load_chunk, nch, rows, lambda sc, _: sc > 0.0),
                       _count(load_chunk, nch, rows, lambda sc, _: sc >= 0.0))
    c_pos, c_nonneg = zero_counts
    at_zero = jnp.logical_and(c_pos < ksel, c_nonneg >= ksel)

    def cond(state):
        it, _, cnt = state
        return jnp.logical_and(it < 32, jnp.max(jnp.where(at_zero, 0.0, cnt)) > ksel)

    def body(state):
        it, key, cnt = state
        cand = key + lax.shift_left(jnp.int32(1), 31 - it)
        cand_f = _key_to_float(cand)
        c = _count(load_chunk, nch, rows, lambda sc, _: sc >= cand_f)
        take = c >= ksel
        return it + 1, jnp.where(take, cand, key), jnp.where(take, c, cnt)

    total = jnp.where(search, nch * KEY_CHUNK, 0).astype(f32)
    _, key, cnt = lax.while_loop(
        cond, body, (jnp.int32(0), jnp.full((rows, 1), INT_MIN, i32), jnp.full((rows, 1), total)))
    lifted = key <= KEY_NEG_INF
    t = jnp.where(lifted, FLT_LOWEST, _key_to_float(key))
    cnt = jnp.where(lifted, 0.0, cnt)
    at_zero = jnp.logical_and(at_zero, search)
    return jnp.where(at_zero, 0.0, t), jnp.where(at_zero, c_nonneg, cnt)


def _demote_excess_ties(load_chunk, store_chunk, nch, rows, ksel, t, c_ge):
    @pl.when(jnp.max(c_ge) > ksel)
    def _():
        need = ksel - _count(load_chunk, nch, rows, lambda sc, c: sc > t)
        upto = (lax.broadcasted_iota(i32, (KEY_CHUNK, KEY_CHUNK), 0)
                <= lax.broadcasted_iota(i32, (KEY_CHUNK, KEY_CHUNK), 1))
        upto = jnp.where(upto, 1.0, 0.0).astype(bf16)

        def rewrite(c, seen):
            sc = load_chunk(c)
            tie = sc == t
            rank = seen + jnp.dot(jnp.where(tie, 1.0, 0.0).astype(bf16), upto,
                                  preferred_element_type=f32)
            store_chunk(c, jnp.where(jnp.where(tie, rank, 0.0) > need, -jnp.inf, sc))
            return rank[:, KEY_CHUNK - 1:KEY_CHUNK]
        seen = jnp.zeros((rows, 1), f32)
        if isinstance(nch, int):
            for c in range(nch):
                seen = rewrite(c, seen)
        else:
            lax.fori_loop(0, nch, rewrite, seen)


def _prompt_attn_kernel(qi_ref, q_ref, kwq_ref, kw_ref, k_ref, v_ref, o_ref,
                        kit2, kt2, vb, kn2, scores, qil, ql, m_s, mx_s, acc_s, *, ksel):
    i = pl.program_id(1)
    seq = kw_ref.shape[0]
    ck = KEY_CHUNK
    half = HEAD_DIM

    @pl.when(i == 0)
    def _prepare_keys():
        def body(c, carry):
            rows = pl.ds(pl.multiple_of(c * ck, ck), ck)
            kit = kw_ref[rows, :].T[0:half].astype(bf16)
            kit2[c, 0:half, :] = kit
            kit2[c, half:2 * half, :] = kit
            kc = k_ref[rows, :]
            for pair in range(2):
                kt = kc[:, pair * LANES:(pair + 1) * LANES].T.astype(bf16)
                for sub in range(2):
                    one = kt[sub * half:(sub + 1) * half]
                    kt2[c, 2 * pair + sub, 0:half, :] = one
                    kt2[c, 2 * pair + sub, half:2 * half, :] = one
                    sq = one.astype(f32)
                    sq = jnp.sum(sq * sq, axis=0, keepdims=True)
                    kn2[2 * pair + sub] = jnp.maximum(kn2[2 * pair + sub], sq)
            vb[c] = v_ref[rows, :].astype(bf16)
            return carry
        kn2[...] = jnp.zeros(kn2.shape, f32)
        lax.fori_loop(0, seq // ck, body, 0)

    lane = lax.broadcasted_iota(i32, (Q_ROWS, LANES), 1)
    low = lane < half
    for p in range(N_KV_HEADS):
        c = qi_ref[:, p * LANES:(p + 1) * LANES].astype(f32)
        qil[(2 * p) * Q_ROWS:(2 * p + 1) * Q_ROWS, :] = jnp.where(low, c, 0.0).astype(bf16)
        qil[(2 * p + 1) * Q_ROWS:(2 * p + 2) * Q_ROWS, :] = jnp.where(low, 0.0, c).astype(bf16)
        c = q_ref[:, p * LANES:(p + 1) * LANES].astype(f32)
        ql[p, 0:Q_ROWS, :] = jnp.where(low, c, 0.0).astype(bf16)
        ql[p, Q_ROWS:2 * Q_ROWS, :] = jnp.where(low, 0.0, c).astype(bf16)
    w_head = [kwq_ref[:, IDX_DIM + h:IDX_DIM + h + 1] for h in range(N_IDX_HEADS)]

    nch = ((i + 1) * Q_ROWS + ck - 1) // ck
    row_id = i * Q_ROWS + lax.broadcasted_iota(i32, (Q_ROWS, ck), 0)

    def lane_fold(x):
        return sum(x[:, j * LANES:(j + 1) * LANES] for j in range(ck // LANES))

    def score_chunk(c, carry):
        n_pos, n_nonneg = carry
        s = jnp.dot(qil[...], kit2[c], preferred_element_type=f32)
        acc = None
        for h in range(N_IDX_HEADS):
            term = jnp.maximum(s[h * Q_ROWS:(h + 1) * Q_ROWS], 0.0) * w_head[h]
            acc = term if acc is None else acc + term
        col = c * ck + lax.broadcasted_iota(i32, (Q_ROWS, ck), 1)
        sc = jnp.where(col <= row_id, acc, -jnp.inf)
        scores[c] = sc
        return (n_pos + lane_fold(jnp.where(sc > 0.0, 1.0, 0.0)),
                n_nonneg + lane_fold(jnp.where(sc >= 0.0, 1.0, 0.0)))
    zeros = jnp.zeros((Q_ROWS, LANES), f32)
    n_pos, n_nonneg = lax.fori_loop(0, nch, score_chunk, (zeros, zeros))
    zero_counts = (jnp.sum(n_pos, axis=-1, keepdims=True),
                   jnp.sum(n_nonneg, axis=-1, keepdims=True))

    load = lambda c: scores[c]

    def store(c, val):
        scores[c] = val
    t, c_ge = _kth_largest_score(load, nch, Q_ROWS, ksel, (i + 1) * Q_ROWS > ksel, zero_counts)
    _demote_excess_ties(load, store, nch, Q_ROWS, ksel, t, c_ge)

    def masked_logits(c, p, bias2):
        return jnp.dot(ql[p], kt2[c, p], preferred_element_type=f32) + bias2

    def bias_of(c):
        bias = jnp.where(scores[c] >= t, 0.0, NEG_BIG)
        return jnp.concatenate([bias, bias], axis=0)

    def attend_sweep():
        mx_s[...] = jnp.zeros(mx_s.shape, f32)
        acc_s[...] = jnp.zeros(acc_s.shape, f32)

        def attend_chunk(c, carry):
            bias2 = bias_of(c)
            for p in range(N_KV_HEADS):
                m = m_s[p]
                pr = jnp.exp(masked_logits(c, p, bias2) - jnp.concatenate([m, m], axis=1))
                mx_s[p] = mx_s[p] + pr
                acc_s[p] = acc_s[p] + jnp.dot(
                    pr.astype(bf16), vb[c, :, (p // 2) * LANES:(p // 2 + 1) * LANES],
                    preferred_element_type=f32)
            return carry
        lax.fori_loop(0, nch, attend_chunk, 0)

    smallest = None
    for p in range(N_KV_HEADS):
        qf = ql[p].astype(f32)
        qn2 = jnp.sum(qf * qf, axis=-1, keepdims=True)
        bound = jnp.sqrt(qn2 * jnp.max(kn2[p], axis=-1, keepdims=True))
        m_s[p] = jnp.broadcast_to(bound, m_s.shape[1:])
    attend_sweep()
    for p in range(N_KV_HEADS):
        low_p = jnp.min(jnp.sum(mx_s[p], axis=-1, keepdims=True))
        smallest = low_p if smallest is None else jnp.minimum(smallest, low_p)

    @pl.when(smallest < SUM_EXP_FLOOR)
    def _exact_maxima():
        mx_s[...] = jnp.full(mx_s.shape, NEG_BIG, f32)

        def max_chunk(c, carry):
            bias2 = bias_of(c)
            for p in range(N_KV_HEADS):
                mx_s[p] = jnp.maximum(mx_s[p], masked_logits(c, p, bias2))
            return carry
        lax.fori_loop(0, nch, max_chunk, 0)
        for p in range(N_KV_HEADS):
            m_s[p] = jnp.broadcast_to(jnp.max(mx_s[p], axis=-1, keepdims=True), m_s.shape[1:])
        attend_sweep()

    for p in range(N_KV_HEADS):
        o = acc_s[p] / jnp.sum(mx_s[p], axis=-1, keepdims=True)
        a, b = o[0:Q_ROWS], o[Q_ROWS:2 * Q_ROWS]
        if p % 2 == 0:
            b = pltpu.roll(b, half, 1)
        else:
            a = pltpu.roll(a, half, 1)
        o_ref[:, p * LANES:(p + 1) * LANES] = jnp.where(low, a, b).astype(bf16)


def _prompt_attention(qi, q, kw, k, v, batch, seq, ksel):
    nqb = seq // Q_ROWS
    nck = seq // KEY_CHUNK
    qrow = lambda b, i: (b * nqb + i, 0)
    whole = lambda b, i: (b, 0)
    return pl.pallas_call(
        functools.partial(_prompt_attn_kernel, ksel=ksel),
        grid=(batch, nqb),
        in_specs=[pl.BlockSpec((Q_ROWS, N_IDX_HEADS * IDX_DIM), qrow),
                  pl.BlockSpec((Q_ROWS, ATTN_WIDTH), qrow),
                  pl.BlockSpec((Q_ROWS, LANES), qrow),
                  pl.BlockSpec((seq, LANES), whole),
                  pl.BlockSpec((seq, KV_WIDTH), whole),
                  pl.BlockSpec((seq, KV_WIDTH), whole)],
        out_specs=pl.BlockSpec((Q_ROWS, ATTN_WIDTH), qrow),
        out_shape=jax.ShapeDtypeStruct((batch * seq, ATTN_WIDTH), bf16),
        scratch_shapes=[pltpu.VMEM((nck, LANES, KEY_CHUNK), bf16),
                        pltpu.VMEM((nck, N_KV_HEADS, LANES, KEY_CHUNK), bf16),
                        pltpu.VMEM((nck, KEY_CHUNK, KV_WIDTH), bf16),
                        pltpu.VMEM((N_KV_HEADS, 1, KEY_CHUNK), f32),
                        pltpu.VMEM((nck, Q_ROWS, KEY_CHUNK), f32),
                        pltpu.VMEM((N_IDX_HEADS * Q_ROWS, LANES), bf16),
                        pltpu.VMEM((N_KV_HEADS, 2 * Q_ROWS, LANES), bf16),
                        pltpu.VMEM((N_KV_HEADS, 2 * Q_ROWS, LANES), f32),
                        pltpu.VMEM((N_KV_HEADS, 2 * Q_ROWS, KEY_CHUNK), f32),
                        pltpu.VMEM((N_KV_HEADS, 2 * Q_ROWS, LANES), f32)],
        compiler_params=_params("arbitrary", "arbitrary"),
        name="prompt_attention",
    )(qi, q, kw, kw, k, v)


PAGES_PER_STEP = 16
PAGE = 128
SAMPLE_Q = 8
SAMPLE_ROWS = N_HEADS * SAMPLE_Q
THRESHOLD_ROWS = 128


def _sample_score_kernel(pt_ref, qi_ref, w_ref, kin_ref, *refs):
    pages = refs[:PAGES_PER_STEP]
    out_ref = refs[PAGES_PER_STEP]
    j = pl.program_id(1)
    last = pl.num_programs(1) - 1
    qi = qi_ref[...]
    w = w_ref[...]

    def score(kit):
        s = jnp.dot(qi, kit.astype(bf16), preferred_element_type=f32)
        s = jnp.maximum(s, 0.0) * w
        return jnp.sum(s.reshape(N_IDX_HEADS, SAMPLE_Q, s.shape[-1]), axis=0)

    @pl.when(j < last)
    def _():
        out_ref[...] = score(jnp.concatenate([r[...] for r in pages], axis=1))

    @pl.when(j == last)
    def _():
        sc = score(kin_ref[...])
        col = lax.broadcasted_iota(i32, sc.shape, 1)
        row = lax.broadcasted_iota(i32, sc.shape, 0)
        out_ref[...] = jnp.full(out_ref.shape, -jnp.inf, f32)
        out_ref[:, 0:PAGE] = jnp.where(col <= row, sc, -jnp.inf)


def _page_spec(block, layer, n, n_pages):
    def index(b, j, pt):
        page = pt[b, jnp.minimum(j * PAGES_PER_STEP + n, n_pages - 1)]
        return (layer, page) + (0,) * (len(block) - 2)
    return pl.BlockSpec(block, index)


def _sample_scores(page_table, qi_s, w_s, ki_new_t, idx_cache_t, layer):
    db, n_pages = page_table.shape
    steps = n_pages // PAGES_PER_STEP + 1
    per_b = lambda b, j, pt: (b, 0, 0)
    step_w = PAGES_PER_STEP * PAGE
    return pl.pallas_call(
        _sample_score_kernel,
        grid_spec=pltpu.PrefetchScalarGridSpec(
            num_scalar_prefetch=1,
            grid=(db, steps),
            in_specs=[pl.BlockSpec((None, SAMPLE_ROWS, IDX_DIM), per_b),
                      pl.BlockSpec((None, SAMPLE_ROWS, 1), per_b),
                      pl.BlockSpec((None, IDX_DIM, PAGE), per_b)]
                     + [_page_spec((None, None, IDX_DIM, PAGE), layer, n, n_pages)
                        for n in range(PAGES_PER_STEP)],
            out_specs=pl.BlockSpec((None, SAMPLE_Q, step_w), lambda b, j, pt: (b, 0, j)),
        ),
        out_shape=jax.ShapeDtypeStruct((db, SAMPLE_Q, steps * step_w), f32),
        compiler_params=_params("arbitrary", "arbitrary"),
        name="sample_scores",
    )(page_table, qi_s, w_s, ki_new_t, *([idx_cache_t] * PAGES_PER_STEP))


def _sample_threshold_kernel(scores_in, scores_out, t_out, *, ksel):
    rows, width = scores_in.shape
    nch = width // KEY_CHUNK
    scores_out[...] = scores_in[...]
    load = lambda c: scores_out[:, c * KEY_CHUNK:(c + 1) * KEY_CHUNK]

    def store(c, val):
        scores_out[:, c * KEY_CHUNK:(c + 1) * KEY_CHUNK] = val
    t, c_ge = _kth_largest_score(load, nch, rows, ksel, True)
    _demote_excess_ties(load, store, nch, rows, ksel, t, c_ge)
    t_out[...] = jnp.broadcast_to(t, t_out.shape)


def _sample_threshold(scores2d, ksel):
    n, width = scores2d.shape
    rows = THRESHOLD_ROWS
    return pl.pallas_call(
        functools.partial(_sample_threshold_kernel, ksel=ksel),
        grid=(n // rows,),
        in_specs=[pl.BlockSpec((rows, width), lambda i: (i, 0))],
        out_specs=[pl.BlockSpec((rows, width), lambda i: (i, 0)),
                   pl.BlockSpec((rows, LANES), lambda i: (i, 0))],
        out_shape=[jax.ShapeDtypeStruct((n, width), f32),
                   jax.ShapeDtypeStruct((n, LANES), f32)],
        compiler_params=_params("arbitrary"),
        name="sample_threshold",
    )(scores2d)


def _sample_attn_kernel(pt_ref, q_ref, sc_ref, t_ref, kn_ref, vn_ref, *refs):
    n = PAGES_PER_STEP
    k_pages, v_pages = refs[:n], refs[n:2 * n]
    o_ref, m_s, l_s, acc_s = refs[2 * n:]
    j = pl.program_id(1)
    last = pl.num_programs(1) - 1
    group_rows = 2 * SAMPLE_Q

    @pl.when(j == 0)
    def _():
        m_s[...] = jnp.full(m_s.shape, NEG_BIG, f32)
        l_s[...] = jnp.zeros(l_s.shape, f32)
        acc_s[...] = jnp.zeros(acc_s.shape, f32)

    t = t_ref[:, 0:1]

    def update(kt, vt, sc):
        bias = jnp.where(sc >= t, 0.0, NEG_BIG)
        bias = jnp.concatenate([bias, bias], axis=0)
        lg = jnp.concatenate(
            [jnp.dot(q_ref[h * group_rows:(h + 1) * group_rows, :], kt[h],
                     preferred_element_type=f32) + bias for h in range(N_KV_HEADS)], axis=0)
        m_old = m_s[...]
        m_new = jnp.maximum(m_old, jnp.max(lg, axis=-1, keepdims=True))
        alpha = jnp.exp(m_old - m_new)
        pr = jnp.exp(lg - m_new[:, 0:1])
        l_s[...] = alpha * l_s[...] + jnp.sum(pr, axis=-1, keepdims=True)
        pr = pr.astype(bf16)
        pv = jnp.concatenate(
            [lax.dot_general(pr[h * group_rows:(h + 1) * group_rows], vt[h],
                             (((1,), (1,)), ((), ())), preferred_element_type=f32)
             for h in range(N_KV_HEADS)], axis=0)
        acc_s[...] = acc_s[...] * alpha[:, 0:HEAD_DIM] + pv
        m_s[...] = m_new

    def lanes_of(page_refs):
        return jnp.concatenate([r[...] for r in page_refs], axis=2).astype(bf16)

    @pl.when(j < last)
    def _():
        update(lanes_of(k_pages), lanes_of(v_pages), sc_ref[...])

    @pl.when(j == last)
    def _():
        update(kn_ref[...], vn_ref[...], sc_ref[:, 0:PAGE])
        o = acc_s[...] / l_s[:, 0:HEAD_DIM]
        for h in range(N_HEADS):
            o_ref[:, h * HEAD_DIM:(h + 1) * HEAD_DIM] = o[h * SAMPLE_Q:(h + 1) * SAMPLE_Q]


def _sample_attention(page_table, q_s, scores3d, t3d, k_new_t, v_new_t, k_cache_t, v_cache_t,
                      layer):
    db, n_pages = page_table.shape
    steps = n_pages // PAGES_PER_STEP + 1
    per_b = lambda b, j, pt: (b, 0, 0)
    per_b4 = lambda b, j, pt: (b, 0, 0, 0)
    step_w = PAGES_PER_STEP * PAGE
    page_block = (None, None, N_KV_HEADS, HEAD_DIM, PAGE)
    return pl.pallas_call(
        _sample_attn_kernel,
        grid_spec=pltpu.PrefetchScalarGridSpec(
            num_scalar_prefetch=1,
            grid=(db, steps),
            in_specs=[pl.BlockSpec((None, SAMPLE_ROWS, HEAD_DIM), per_b),
                      pl.BlockSpec((None, SAMPLE_Q, step_w), lambda b, j, pt: (b, 0, j)),
                      pl.BlockSpec((None, SAMPLE_Q, LANES), per_b),
                      pl.BlockSpec((None, N_KV_HEADS, HEAD_DIM, PAGE), per_b4),
                      pl.BlockSpec((None, N_KV_HEADS, HEAD_DIM, PAGE), per_b4)]
                     + [_page_spec(page_block, layer, n, n_pages)
                        for n in range(PAGES_PER_STEP)] * 2,
            out_specs=pl.BlockSpec((None, SAMPLE_Q, ATTN_WIDTH), per_b),
            scratch_shapes=[pltpu.VMEM((SAMPLE_ROWS, LANES), f32),
                            pltpu.VMEM((SAMPLE_ROWS, LANES), f32),
                            pltpu.VMEM((SAMPLE_ROWS, HEAD_DIM), f32)],
        ),
        out_shape=jax.ShapeDtypeStruct((db, SAMPLE_Q, ATTN_WIDTH), f32),
        compiler_params=_params("arbitrary", "arbitrary"),
        name="sample_attention",
    )(page_table, q_s, scores3d, t3d, k_new_t, v_new_t,
      *([k_cache_t] * PAGES_PER_STEP), *([v_cache_t] * PAGES_PER_STEP))


def _pool_diff_prompt(u_ref, hist_ref, seq):
    tm = u_ref.shape[0]
    start = (pl.program_id(0) * tm) % seq
    u = u_ref[...]
    hist = hist_ref[...] * jnp.where(start == 0, 0.0, 1.0)
    ext = jnp.concatenate([hist, u], axis=0)
    pos = start + lax.broadcasted_iota(i32, (tm, 1), 0)
    outs = []
    for g, win in enumerate(POOL_WINDOWS):
        s = ext[:, g * POOL_GROUP:(g + 1) * POOL_GROUP]
        span = 1
        while span < win:
            s = s + pltpu.roll(s, span, 0)
            span *= 2
        cnt = jnp.minimum(pos + 1, win).astype(f32)
        outs.append(s[HIST_ROWS:] / cnt - u[:, g * POOL_GROUP:(g + 1) * POOL_GROUP])
    return outs


def _pool_diff_sample(u_ref):
    nb = u_ref.shape[0]
    outs = []
    for g, win in enumerate(POOL_WINDOWS):
        lanes = slice(g * POOL_GROUP, (g + 1) * POOL_GROUP)
        cur = u_ref[:, HIST_ROWS:HIST_ROWS + 8, lanes]
        s = cur
        for d in range(1, win):
            s = s + u_ref[:, HIST_ROWS - d:HIST_ROWS + 8 - d, lanes]
        outs.append((s / float(win) - cur).reshape(nb * 8, POOL_GROUP))
    return outs


def _merge_kernel(x_ref, ao_ref, u_ref, hist_ref, ga_ref, gb_ref, wpool_ref, pscale_ref,
                  wba_ref, wbp_ref, wout_ref, fg_ref, wr_ref, br_ref,
                  h_ref, hn_ref, route_ref, *, seq):
    diffs = _pool_diff_sample(u_ref) if seq is None else _pool_diff_prompt(u_ref, hist_ref, seq)
    pool = jnp.concatenate(
        [jnp.dot(d.astype(bf16), wpool_ref[g], preferred_element_type=f32)
         for g, d in enumerate(diffs)], axis=1) * pscale_ref[...]
    br_a = jnp.dot(ao_ref[...].astype(bf16), wba_ref[...], preferred_element_type=f32)
    br_p = jnp.dot(pool.astype(bf16), wbp_ref[...], preferred_element_type=f32)
    sig = lambda z: 1.0 / (1.0 + jnp.exp(-z))
    mix = sig(ga_ref[...]) * br_a + sig(gb_ref[...]) * br_p
    h = x_ref[...] + jnp.dot(mix.astype(bf16), wout_ref[...], preferred_element_type=f32)
    h_ref[...] = h
    hn = _rms(h, fg_ref[...])
    hn_ref[...] = hn

    logit = jnp.dot(hn, wr_ref[...], preferred_element_type=f32,
                    precision=lax.Precision.HIGHEST) + br_ref[...]
    lane_i = lax.broadcasted_iota(i32, logit.shape, 1)
    lane = lane_i.astype(f32)
    big = float(LANES)
    gl = jnp.where(lane_i < N_EXPERTS, -jnp.inf,
                   jnp.where(lane_i < N_EXPERTS + N_GROUPS, logit, -jnp.inf))
    gmax = jnp.max(gl, axis=-1, keepdims=True)
    g_sel = jnp.min(jnp.where(gl == gmax, lane, big), axis=-1, keepdims=True) - N_EXPERTS
    g_w = 1.0 / jnp.sum(jnp.exp(gl - gmax), axis=-1, keepdims=True)
    group_of_lane = (lane_i // EXPERTS_PER_GROUP).astype(f32)
    el = jnp.where(group_of_lane == g_sel, logit, -jnp.inf)
    v1 = jnp.max(el, axis=-1, keepdims=True)
    i1 = jnp.min(jnp.where(el == v1, lane, big), axis=-1, keepdims=True)
    el2 = jnp.where(lane == i1, -jnp.inf, el)
    v2 = jnp.max(el2, axis=-1, keepdims=True)
    i2 = jnp.min(jnp.where(el2 == v2, lane, big), axis=-1, keepdims=True)
    e21 = jnp.exp(v2 - v1)
    w1 = g_w / (1.0 + e21)
    w2 = g_w * e21 / (1.0 + e21)
    route_ref[...] = jnp.where(lane_i == 0, i1,
                     jnp.where(lane_i == 1, i2,
                     jnp.where(lane_i == 2, w1, jnp.where(lane_i == 3, w2, 0.0))))


def _merge(x2d, ao, u, ga, gb, lw, seq):
    n = x2d.shape[0]
    tm = PROJ_ROWS
    row = lambda i: (i, 0)
    const2 = lambda i: (0, 0)
    const3 = lambda i: (0, 0, 0)
    if seq is None:
        u_spec = pl.BlockSpec((tm // 8, HIST_ROWS + 8, POOL_WIDTH), lambda i: (i, 0, 0))
        hist = jnp.zeros((HIST_ROWS, POOL_WIDTH), f32)
        hist_spec = pl.BlockSpec((HIST_ROWS, POOL_WIDTH), const2)
    else:
        u_spec = pl.BlockSpec((tm, POOL_WIDTH), row)
        hist = u
        hist_spec = pl.BlockSpec((HIST_ROWS, POOL_WIDTH),
                                 lambda i: (jnp.maximum(i * (tm // HIST_ROWS) - 1, 0), 0))
    return pl.pallas_call(
        functools.partial(_merge_kernel, seq=seq),
        grid=(n // tm,),
        in_specs=[pl.BlockSpec((tm, D_MODEL), row),
                  pl.BlockSpec((tm, ATTN_WIDTH), row),
                  u_spec, hist_spec,
                  pl.BlockSpec((tm, D_MODEL), row),
                  pl.BlockSpec((tm, D_MODEL), row),
                  pl.BlockSpec((4, POOL_GROUP, POOL_GROUP), const3),
                  pl.BlockSpec((1, POOL_WIDTH), const2),
                  pl.BlockSpec((ATTN_WIDTH, D_MODEL), const2),
                  pl.BlockSpec((POOL_WIDTH, D_MODEL), const2),
                  pl.BlockSpec((D_MODEL, D_MODEL), const2),
                  pl.BlockSpec((1, D_MODEL), const2),
                  pl.BlockSpec((D_MODEL, LANES), const2),
                  pl.BlockSpec((1, LANES), const2)],
        out_specs=[pl.BlockSpec((tm, D_MODEL), row),
                   pl.BlockSpec((tm, D_MODEL), row),
                   pl.BlockSpec((tm, LANES), row)],
        out_shape=[jax.ShapeDtypeStruct((n, D_MODEL), f32),
                   jax.ShapeDtypeStruct((n, D_MODEL), f32),
                   jax.ShapeDtypeStruct((n, LANES), f32)],
        compiler_params=_params("arbitrary"),
        name="merge",
    )(x2d, ao, u, hist, ga, gb, lw['w_pool'], lw['pool_scale'], lw['w_ba'], lw['w_bp'],
      lw['w_out'], lw['ffn_g'], lw['w_router'], lw['b_router'])


RANK_ROWS = 512
MOE_BLOCK_PROMPT = 256
MOE_BLOCK_SAMPLE = 128
GATHER_UNROLL = 8


def _rank_kernel(route_ref, rank_ref, counts_ref, base_s):
    @pl.when(pl.program_id(0) == 0)
    def _():
        base_s[...] = jnp.zeros(base_s.shape, f32)
    r = route_ref[...]
    tt = r.shape[0]
    lane = lax.broadcasted_iota(i32, r.shape, 1).astype(f32)
    oh1 = lane == r[:, 0:1]
    oh2 = lane == r[:, 1:2]
    oh = jnp.where(oh1, 1.0, 0.0) + jnp.where(oh2, 1.0, 0.0)
    below = (lax.broadcasted_iota(i32, (tt, tt), 0) > lax.broadcasted_iota(i32, (tt, tt), 1))
    before = jnp.dot(jnp.where(below, 1.0, 0.0).astype(bf16), oh.astype(bf16),
                     preferred_element_type=f32) + base_s[...]
    r1 = jnp.sum(jnp.where(oh1, before, 0.0), axis=-1, keepdims=True)
    r2 = jnp.sum(jnp.where(oh2, before, 0.0), axis=-1, keepdims=True)
    lane_i = lax.broadcasted_iota(i32, r.shape, 1)
    rank_ref[...] = jnp.where(lane_i == 0, r1, jnp.where(lane_i == 1, r2, 0.0))
    base_s[...] = base_s[...] + jnp.sum(oh, axis=0, keepdims=True)
    counts_ref[...] = base_s[...]


def _expert_ranks(route):
    n = route.shape[0]
    tt = min(RANK_ROWS, n)
    return pl.pallas_call(
        _rank_kernel,
        grid=(n // tt,),
        in_specs=[pl.BlockSpec((tt, LANES), lambda i: (i, 0))],
        out_specs=[pl.BlockSpec((tt, LANES), lambda i: (i, 0)),
                   pl.BlockSpec((1, LANES), lambda i: (0, 0))],
        out_shape=[jax.ShapeDtypeStruct((n, LANES), f32),
                   jax.ShapeDtypeStruct((1, LANES), f32)],
        scratch_shapes=[pltpu.VMEM((1, LANES), f32)],
        compiler_params=_params("arbitrary"),
        name="expert_ranks",
    )(route)


INVERT_ROWS = 8192


def _slot_tokens_kernel(dest_ref, zeros_hbm, tok_ref, sem):
    i = pl.program_id(0)
    tile = dest_ref.shape[0]

    @pl.when(i == 0)
    def _():
        fill = pltpu.make_async_copy(zeros_hbm, tok_ref, sem)
        fill.start()
        fill.wait()

    def body(a, carry):
        tok_ref[dest_ref[a]] = (i * tile + a) // TOP_K
        return carry
    lax.fori_loop(0, tile, body, 0, unroll=16)


def _slot_tokens(dest, n_slots):
    n = dest.shape[0]
    tile = min(INVERT_ROWS, n)
    return pl.pallas_call(
        _slot_tokens_kernel,
        grid=(n // tile,),
        in_specs=[pl.BlockSpec((tile,), lambda i: (i,), memory_space=pltpu.SMEM),
                  pl.BlockSpec(memory_space=pl.ANY)],
        out_specs=pl.BlockSpec((n_slots,), lambda i: (0,), memory_space=pltpu.SMEM),
        out_shape=jax.ShapeDtypeStruct((n_slots,), i32),
        scratch_shapes=[pltpu.SemaphoreType.DMA(())],
        compiler_params=_params("arbitrary"),
        name="moe_slot_tokens",
    )(dest, jnp.zeros((n_slots,), i32))


def _expert_ffn_kernel(be_ref, nact_ref, tok_ref, tok_next_ref, x_hbm, wg_ref, wu_ref, wd_ref,
                       ys_ref, xbuf, sem):
    i = pl.program_id(0)
    n_active = nact_ref[0]
    bm = ys_ref.shape[0]
    slot = i % 2

    def gather(tok, s):
        def body(g, carry):
            for j in range(GATHER_UNROLL):
                r = g * GATHER_UNROLL + j
                pltpu.make_async_copy(x_hbm.at[pl.ds(tok[0, r], 1)], xbuf.at[s, pl.ds(r, 1)],
                                      sem.at[s]).start(priority=j % 2)
            return carry
        lax.fori_loop(0, bm // GATHER_UNROLL, body, 0)

    @pl.when(i == 0)
    def _():
        gather(tok_ref, 0)

    @pl.when(i + 1 < n_active)
    def _():
        gather(tok_next_ref, 1 - slot)

    @pl.when(i < n_active)
    def _():
        pltpu.make_async_copy(x_hbm.at[pl.ds(0, bm)], xbuf.at[slot], sem.at[slot]).wait()
        x = xbuf[slot].astype(bf16)
        g = jnp.dot(x, wg_ref[...], preferred_element_type=f32)
        u = jnp.dot(x, wu_ref[...], preferred_element_type=f32)
        hb = (g / (1.0 + jnp.exp(-g))) * u
        ys_ref[...] = jnp.dot(hb.astype(bf16), wd_ref[...], preferred_element_type=f32)

    @pl.when(i >= n_active)
    def _():
        ys_ref[...] = jnp.zeros(ys_ref.shape, f32)


def _expert_ffn(block_expert, n_active, slot_tok, x, wg, wu, wd, bm):
    n_blocks = slot_tok.shape[0] // bm
    tok3 = slot_tok.reshape(n_blocks, 1, bm)
    by_expert = lambda i, be, na: (be[i], 0, 0)
    smem_block = lambda index: pl.BlockSpec((None, 1, bm), index, memory_space=pltpu.SMEM)
    return pl.pallas_call(
        _expert_ffn_kernel,
        grid_spec=pltpu.PrefetchScalarGridSpec(
            num_scalar_prefetch=2,
            grid=(n_blocks,),
            in_specs=[smem_block(lambda i, be, na: (i, 0, 0)),
                      smem_block(lambda i, be, na: (jnp.minimum(i + 1, n_blocks - 1), 0, 0)),
                      pl.BlockSpec(memory_space=pl.ANY),
                      pl.BlockSpec((None, D_MODEL, D_EXPERT), by_expert),
                      pl.BlockSpec((None, D_MODEL, D_EXPERT), by_expert),
                      pl.BlockSpec((None, D_EXPERT, D_MODEL), by_expert)],
            out_specs=pl.BlockSpec((bm, D_MODEL), lambda i, be, na: (i, 0)),
            scratch_shapes=[pltpu.VMEM((2, bm, D_MODEL), f32),
                            pltpu.SemaphoreType.DMA((2,))],
        ),
        out_shape=jax.ShapeDtypeStruct((n_blocks * bm, D_MODEL), f32),
        compiler_params=_params("arbitrary"),
        name="expert_ffn",
    )(block_expert, n_active, tok3, tok3, x, wg, wu, wd)


COMBINE_ROWS = 128


def _combine_kernel(dest_ref, dest_next_ref, h_ref, route_ref, g_ref, ys_hbm, y_ref, buf, sem):
    i = pl.program_id(0)
    tt = h_ref.shape[0]
    slot = i % 2

    def gather(dest, s):
        def body(g, carry):
            for k in range(GATHER_UNROLL // TOP_K):
                t = g * (GATHER_UNROLL // TOP_K) + k
                for j in range(TOP_K):
                    pltpu.make_async_copy(ys_hbm.at[pl.ds(dest[0, TOP_K * t + j], 1)],
                                          buf.at[s, j, pl.ds(t, 1)], sem.at[s]).start(priority=j)
            return carry
        lax.fori_loop(0, tt * TOP_K // GATHER_UNROLL, body, 0)

    @pl.when(i == 0)
    def _():
        gather(dest_ref, 0)

    @pl.when(i + 1 < pl.num_programs(0))
    def _():
        gather(dest_next_ref, 1 - slot)

    for j in range(TOP_K):
        pltpu.make_async_copy(ys_hbm.at[pl.ds(0, tt)], buf.at[slot, j], sem.at[slot]).wait()
    r = route_ref[...]
    moe = buf[slot, 0] * r[:, 2:3] + buf[slot, 1] * r[:, 3:4]
    y_ref[...] = _rms(h_ref[...] + moe, g_ref[...])


def _combine(dest, h, route, final_g, ys):
    n = h.shape[0]
    tt = COMBINE_ROWS
    n_tiles = n // tt
    dest3 = dest.reshape(n_tiles, 1, tt * TOP_K)
    row = lambda i: (i, 0)
    smem_block = lambda index: pl.BlockSpec((None, 1, tt * TOP_K), index, memory_space=pltpu.SMEM)
    return pl.pallas_call(
        _combine_kernel,
        grid=(n_tiles,),
        in_specs=[smem_block(lambda i: (i, 0, 0)),
                  smem_block(lambda i: (jnp.minimum(i + 1, n_tiles - 1), 0, 0)),
                  pl.BlockSpec((tt, D_MODEL), row),
                  pl.BlockSpec((tt, LANES), row),
                  pl.BlockSpec((1, D_MODEL), lambda i: (0, 0)),
                  pl.BlockSpec(memory_space=pl.ANY)],
        out_specs=pl.BlockSpec((tt, D_MODEL), row),
        out_shape=jax.ShapeDtypeStruct((n, D_MODEL), f32),
        scratch_shapes=[pltpu.VMEM((2, TOP_K, tt, D_MODEL), f32),
                        pltpu.SemaphoreType.DMA((2,))],
        compiler_params=_params("arbitrary"),
        name="moe_combine",
    )(dest3, dest3, h, route, final_g, ys)


def _moe_and_final_norm(h, hn, route, lw, final_g, bm):
    n = h.shape[0]
    rank, counts = _expert_ranks(route)
    counts = counts[0, :N_EXPERTS].astype(i32)
    padded = (counts + bm - 1) // bm * bm
    pend = jnp.cumsum(padded)
    pstart = pend - padded
    e_id = route[:, 0:TOP_K].astype(i32)
    dest = (pstart[e_id] + rank[:, 0:TOP_K].astype(i32)).reshape(-1)
    n_blocks = (n * TOP_K + N_EXPERTS * (bm - 1)) // bm + 1
    n_active = (pend[-1] // bm).astype(i32)
    blk = jnp.minimum(jnp.arange(n_blocks, dtype=i32), n_active - 1) * bm
    block_expert = jnp.minimum(jnp.sum((pend[None, :] <= blk[:, None]).astype(i32), axis=1),
                               N_EXPERTS - 1)
    slot_tok = _slot_tokens(dest, n_blocks * bm)
    ys = _expert_ffn(block_expert, n_active.reshape(1), slot_tok, hn,
                     lw['w_eg'], lw['w_eu'], lw['w_ed'], bm)
    return _combine(dest, h, route, final_g, ys)


def _rope_tables(pos, reps):
    half = HEAD_DIM // 2
    inv_freq = 1.0 / (ROPE_THETA ** (jnp.arange(half, dtype=f32) / half))
    ang = pos.astype(f32)[:, None] * inv_freq[None, :]
    cos, sin = jnp.cos(ang), jnp.sin(ang)
    cos = jnp.concatenate([cos, cos, cos, cos], axis=1)
    sin = jnp.concatenate([-sin, sin, -sin, sin], axis=1)
    return jnp.tile(cos, (reps, 1)), jnp.tile(sin, (reps, 1))


def _layer_weights(l, attn_norm_g, w_in, w_pool, pool_scale, w_branch_attn, w_branch_pool, w_out,
                   ffn_norm_g, w_rg, b_rg, w_re, b_re, w_eg, w_eu, w_ed):
    cuts = [0]
    for s in SEGMENTS:
        cuts.append(cuts[-1] + s)
    w = w_in[l]
    kw_pad = LANES - IDX_DIM - N_IDX_HEADS
    w_packed = jnp.concatenate(
        [w[:, cuts[0]:cuts[4]], w[:, cuts[4]:cuts[6]], jnp.zeros((D_MODEL, kw_pad), f32),
         w[:, cuts[6]:]], axis=1).astype(bf16)
    r_pad = LANES - N_EXPERTS - N_GROUPS
    return dict(
        attn_g=attn_norm_g[l].reshape(1, D_MODEL), w_packed=w_packed,
        w_pool=w_pool[l].astype(bf16), pool_scale=pool_scale[l].reshape(1, POOL_WIDTH),
        w_ba=w_branch_attn[l].astype(bf16), w_bp=w_branch_pool[l].astype(bf16),
        w_out=w_out[l].astype(bf16), ffn_g=ffn_norm_g[l].reshape(1, D_MODEL),
        w_router=jnp.concatenate([w_re[l], w_rg[l], jnp.zeros((D_MODEL, r_pad), f32)], axis=1),
        b_router=jnp.concatenate([b_re[l], b_rg[l], jnp.zeros((r_pad,), f32)]).reshape(1, LANES),
        w_eg=w_eg[l].astype(bf16), w_eu=w_eu[l].astype(bf16), w_ed=w_ed[l].astype(bf16))


def _prompt_layer(x, lw, final_g):
    batch, seq, _ = x.shape
    x2d = x.reshape(batch * seq, D_MODEL)
    cos, sin = _rope_tables(jnp.arange(seq), 1)
    q, k, v, qi, kw, u, ga, gb = _project(x2d, lw['attn_g'], lw['w_packed'], cos, sin)
    ao = _prompt_attention(qi, q, kw, k, v, batch, seq, min(TOPK_MAX, seq // 4))
    h, hn, route = _merge(x2d, ao, u, ga, gb, lw, seq)
    y = _moe_and_final_norm(h, hn, route, lw, final_g, bm=MOE_BLOCK_PROMPT)
    return (y.reshape(batch, seq, D_MODEL),
            k.reshape(batch, seq, N_KV_HEADS, HEAD_DIM), v.reshape(batch, seq, N_KV_HEADS, HEAD_DIM),
            kw[:, :IDX_DIM].reshape(batch, seq, IDX_DIM),
            u.reshape(batch, seq, POOL_WIDTH)[:, seq - POOL_HIST:])


def _sample_layer(x, l, cache_k, cache_v, cache_idx_k, state_pool, page_table, lw, final_g):
    db, ds, _ = x.shape
    page = cache_k.shape[2]
    past = page_table.shape[1] * page
    x2d = x.reshape(db * ds, D_MODEL)
    cos, sin = _rope_tables(past + jnp.arange(ds), PROJ_ROWS // ds)
    q, k, v, qi, kw, u, ga, gb = _project(x2d, lw['attn_g'], lw['w_packed'], cos, sin)

    rows_hq = lambda a, nh, d: a.reshape(db, ds, nh, d).transpose(0, 2, 1, 3).reshape(db, nh * ds, d)
    qi_s = rows_hq(qi, N_IDX_HEADS, IDX_DIM)
    w_s = kw[:, IDX_DIM:IDX_DIM + N_IDX_HEADS].reshape(db, ds, N_IDX_HEADS)
    w_s = w_s.transpose(0, 2, 1).reshape(db, N_IDX_HEADS * ds, 1)
    pad_slots = lambda a: jnp.pad(a, [(0, 0)] * (a.ndim - 1) + [(0, page - ds)])
    ki_new_t = pad_slots(kw[:, :IDX_DIM].reshape(db, ds, IDX_DIM).transpose(0, 2, 1))
    scores = _sample_scores(page_table, qi_s, w_s, ki_new_t,
                            cache_idx_k.transpose(0, 1, 3, 2), l)
    width = scores.shape[-1]
    scores, t = _sample_threshold(scores.reshape(db * ds, width),
                                  min(TOPK_MAX, (past + ds) // 4))

    as_page = lambda a: pad_slots(
        a.reshape(db, ds, N_KV_HEADS, HEAD_DIM).transpose(0, 2, 3, 1)).astype(bf16)
    ao = _sample_attention(page_table, rows_hq(q, N_HEADS, HEAD_DIM),
                           scores.reshape(db, ds, width), t.reshape(db, ds, LANES),
                           as_page(k), as_page(v),
                           cache_k.transpose(0, 1, 3, 4, 2), cache_v.transpose(0, 1, 3, 4, 2), l)

    u3 = u.reshape(db, ds, POOL_WIDTH)
    u_all = jnp.concatenate([jnp.zeros((db, 1, POOL_WIDTH), f32), state_pool[l], u3], axis=1)
    h, hn, route = _merge(x2d, ao.reshape(db * ds, ATTN_WIDTH), u_all, ga, gb, lw, None)
    y = _moe_and_final_norm(h, hn, route, lw, final_g, bm=MOE_BLOCK_SAMPLE)
    return (y.reshape(db, ds, D_MODEL),
            k.reshape(db, ds, N_KV_HEADS, HEAD_DIM), v.reshape(db, ds, N_KV_HEADS, HEAD_DIM),
            kw[:, :IDX_DIM].reshape(db, ds, IDX_DIM),
            u_all[:, -POOL_HIST:])


def kernel(x_prompt, x_sample, cache_k, cache_v, cache_idx_k, state_pool, page_table, attn_norm_g, w_in, w_pool, pool_scale, w_branch_attn, w_branch_pool, w_out, ffn_norm_g, w_router_group, b_router_group, w_router_expert, b_router_expert, w_exp_gate, w_exp_up, w_exp_down, final_norm_g):
    depth = w_in.shape[0]
    assert depth == 1, "the final norm is fused into the (single) layer's MoE combine"
    final_g = final_norm_g.reshape(1, D_MODEL)
    lw = _layer_weights(0, attn_norm_g, w_in, w_pool, pool_scale, w_branch_attn, w_branch_pool,
                        w_out, ffn_norm_g, w_router_group, b_router_group, w_router_expert,
                        b_router_expert, w_exp_gate, w_exp_up, w_exp_down)
    yp, kp, vp, kip, up = _prompt_layer(x_prompt, lw, final_g)
    ys, ks, vs, kis, us = _sample_layer(x_sample, 0, cache_k, cache_v, cache_idx_k, state_pool,
                                        page_table, lw, final_g)
    stack = lambda a: a[None]
    return (yp, ys, stack(kp), stack(vp), stack(kip), stack(up),
            stack(ks), stack(vs), stack(kis), stack(us))
```

```python
import functools

import jax
import jax.numpy as jnp
from jax import lax
from jax.experimental import pallas as pl
from jax.experimental.pallas import tpu as pltpu

D_MODEL = 1024
HEAD_DIM = 64
N_HEADS = 8
N_KV_HEADS = 4
ATTN_WIDTH = N_HEADS * HEAD_DIM
KV_WIDTH = N_KV_HEADS * HEAD_DIM
N_IDX_HEADS = 8
IDX_DIM = 64
TOPK_MAX = 256
ROPE_THETA = 10000.0
POOL_WINDOWS = (2, 4, 8, 16)
POOL_WIDTH = 512
POOL_GROUP = 128
POOL_HIST = 15
N_GROUPS = 4
EXPERTS_PER_GROUP = 8
N_EXPERTS = 32
TOP_K = 2
D_EXPERT = 512
RMS_EPS = 1e-6
SEGMENTS = (ATTN_WIDTH, KV_WIDTH, KV_WIDTH, N_IDX_HEADS * IDX_DIM, IDX_DIM, N_IDX_HEADS,
            POOL_WIDTH, D_MODEL, D_MODEL)

LANES = 128
VMEM_LIMIT = 56 * 1024 * 1024
KEY_CHUNK = 256
Q_ROWS = 256
PROJ_ROWS = 256
HIST_ROWS = 16
NEG_BIG = -1e30
INT_MIN = -2 ** 31
KEY_NEG_INF = -2139095041
FLT_LOWEST = -3.4028234663852886e38
SUM_EXP_FLOOR = 1e-22

_SEG = dict(q=(0, 512), k=(512, 768), v=(768, 1024), qi=(1024, 1536), kw=(1536, 1664),
            u=(1664, 2176), ga=(2176, 3200), gb=(3200, 4224))
PROJ_PACKED = 4224

f32 = jnp.float32
bf16 = jnp.bfloat16
i32 = jnp.int32


def _params(*sem):
    return pltpu.CompilerParams(dimension_semantics=sem, vmem_limit_bytes=VMEM_LIMIT)


def _rms(x, g):
    return x * lax.rsqrt(jnp.mean(x * x, axis=-1, keepdims=True) + RMS_EPS) * g


def _key_to_float(k):
    return lax.bitcast_convert_type(k ^ ((k >> 31) & 0x7FFFFFFF), f32)


def _rope(x, cos, sin_signed):
    lane = lax.broadcasted_iota(i32, (x.shape[0], LANES), 1)
    first_half = (lane % HEAD_DIM) < HEAD_DIM // 2
    outs = []
    for c in range(x.shape[1] // LANES):
        xc = x[:, c * LANES:(c + 1) * LANES]
        partner = jnp.where(first_half, pltpu.roll(xc, LANES - HEAD_DIM // 2, 1),
                            pltpu.roll(xc, HEAD_DIM // 2, 1))
        outs.append(xc * cos + partner * sin_signed)
    return outs[0] if len(outs) == 1 else jnp.concatenate(outs, axis=1)


def _proj_kernel(x_ref, g_ref, w_ref, cos_ref, sin_ref,
                 q_ref, k_ref, v_ref, qi_ref, kw_ref, u_ref, ga_ref, gb_ref):
    xb = _rms(x_ref[...], g_ref[...]).astype(bf16)
    cos = cos_ref[...]
    sin = sin_ref[...]

    def seg(name):
        a, b = _SEG[name]
        return jnp.dot(xb, w_ref[:, a:b], preferred_element_type=f32)

    q_ref[...] = (_rope(seg('q'), cos, sin) * (HEAD_DIM ** -0.5)).astype(bf16)
    k_ref[...] = _rope(seg('k'), cos, sin)
    v_ref[...] = seg('v')
    qi_ref[...] = (_rope(seg('qi'), cos, sin) * (IDX_DIM ** -0.5)).astype(bf16)
    kw = seg('kw')
    lane = lax.broadcasted_iota(i32, kw.shape, 1)
    kw_ref[...] = jnp.where(lane < IDX_DIM, _rope(kw, cos, sin), kw * (N_IDX_HEADS ** -0.5))
    u_ref[...] = seg('u')
    ga_ref[...] = seg('ga')
    gb_ref[...] = seg('gb')


def _project(x2d, g, w_packed, cos_tab, sin_tab):
    n = x2d.shape[0]
    tm = PROJ_ROWS
    ntab = cos_tab.shape[0] // tm
    row = lambda i: (i, 0)
    const = lambda i: (0, 0)
    tab = lambda i: (i % ntab, 0)
    widths = tuple(b - a for a, b in _SEG.values())
    dtypes = (bf16, f32, f32, bf16, f32, f32, f32, f32)
    return pl.pallas_call(
        _proj_kernel,
        grid=(n // tm,),
        in_specs=[pl.BlockSpec((tm, D_MODEL), row),
                  pl.BlockSpec((1, D_MODEL), const),
                  pl.BlockSpec((D_MODEL, PROJ_PACKED), const),
                  pl.BlockSpec((tm, LANES), tab),
                  pl.BlockSpec((tm, LANES), tab)],
        out_specs=[pl.BlockSpec((tm, w), row) for w in widths],
        out_shape=[jax.ShapeDtypeStruct((n, w), d) for w, d in zip(widths, dtypes)],
        compiler_params=_params("arbitrary"),
        name="project",
    )(x2d, g, w_packed, cos_tab, sin_tab)


def _count(load_chunk, nch, rows, pred):
    def body(c, acc):
        ind = jnp.where(pred(load_chunk(c), c), 1.0, 0.0)
        for s in range(KEY_CHUNK // LANES):
            acc = acc + ind[:, s * LANES:(s + 1) * LANES]
        return acc
    acc = jnp.zeros((rows, LANES), f32)
    if isinstance(nch, int):
        for c in range(nch):
            acc = body(c, acc)
    else:
        acc = lax.fori_loop(0, nch, body, acc)
    return jnp.sum(acc, axis=-1, keepdims=True)


def _kth_largest_score(load_chunk, nch, rows, ksel, search, zero_counts=None):
    if zero_counts is None:
        zero_counts = (_count(load_chunk, nch, rows, lambda sc, _: sc > 0.0),
                       _count(load_chunk, nch, rows, lambda sc, _: sc >= 0.0))
    c_pos, c_nonneg = zero_counts
    at_zero = jnp.logical_and(c_pos < ksel, c_nonneg >= ksel)

    def cond(state):
        it, _, cnt = state
        return jnp.logical_and(it < 32, jnp.max(jnp.where(at_zero, 0.0, cnt)) > ksel)

    def body(state):
        it, key, cnt = state
        cand = key + lax.shift_left(jnp.int32(1), 31 - it)
        cand_f = _key_to_float(cand)
        c = _count(load_chunk, nch, rows, lambda sc, _: sc >= cand_f)
        take = c >= ksel
        return it + 1, jnp.where(take, cand, key), jnp.where(take, c, cnt)

    total = jnp.where(search, nch * KEY_CHUNK, 0).astype(f32)
    _, key, cnt = lax.while_loop(
        cond, body, (jnp.int32(0), jnp.full((rows, 1), INT_MIN, i32), jnp.full((rows, 1), total)))
    lifted = key <= KEY_NEG_INF
    t = jnp.where(lifted, FLT_LOWEST, _key_to_float(key))
    cnt = jnp.where(lifted, 0.0, cnt)
    at_zero = jnp.logical_and(at_zero, search)
    return jnp.where(at_zero, 0.0, t), jnp.where(at_zero, c_nonneg, cnt)


def _demote_excess_ties(load_chunk, store_chunk, nch, rows, ksel, t, c_ge):
    @pl.when(jnp.max(c_ge) > ksel)
    def _():
        need = ksel - _count(load_chunk, nch, rows, lambda sc, c: sc > t)
        upto = (lax.broadcasted_iota(i32, (KEY_CHUNK, KEY_CHUNK), 0)
                <= lax.broadcasted_iota(i32, (KEY_CHUNK, KEY_CHUNK), 1))
        upto = jnp.where(upto, 1.0, 0.0).astype(bf16)

        def rewrite(c, seen):
            sc = load_chunk(c)
            tie = sc == t
            rank = seen + jnp.dot(jnp.where(tie, 1.0, 0.0).astype(bf16), upto,
                                  preferred_element_type=f32)
            store_chunk(c, jnp.where(jnp.where(tie, rank, 0.0) > need, -jnp.inf, sc))
            return rank[:, KEY_CHUNK - 1:KEY_CHUNK]
        seen = jnp.zeros((rows, 1), f32)
        if isinstance(nch, int):
            for c in range(nch):
                seen = rewrite(c, seen)
        else:
            lax.fori_loop(0, nch, rewrite, seen)


def _prompt_attn_kernel(qi_ref, q_ref, kwq_ref, kw_ref, k_ref, v_ref, o_ref,
                        kit2, kt2, vb, kn2, scores, qil, ql, m_s, mx_s, acc_s, *, ksel):
    i = pl.program_id(1)
    seq = kw_ref.shape[0]
    ck = KEY_CHUNK
    half = HEAD_DIM

    @pl.when(i == 0)
    def _prepare_keys():
        def body(c, carry):
            rows = pl.ds(pl.multiple_of(c * ck, ck), ck)
            kit = kw_ref[rows, :].T[0:half].astype(bf16)
            kit2[c, 0:half, :] = kit
            kit2[c, half:2 * half, :] = kit
            kc = k_ref[rows, :]
            for pair in range(2):
                kt = kc[:, pair * LANES:(pair + 1) * LANES].T.astype(bf16)
                for sub in range(2):
                    one = kt[sub * half:(sub + 1) * half]
                    kt2[c, 2 * pair + sub, 0:half, :] = one
                    kt2[c, 2 * pair + sub, half:2 * half, :] = one
                    sq = one.astype(f32)
                    sq = jnp.sum(sq * sq, axis=0, keepdims=True)
                    kn2[2 * pair + sub] = jnp.maximum(kn2[2 * pair + sub], sq)
            vb[c] = v_ref[rows, :].astype(bf16)
            return carry
        kn2[...] = jnp.zeros(kn2.shape, f32)
        lax.fori_loop(0, seq // ck, body, 0)

    lane = lax.broadcasted_iota(i32, (Q_ROWS, LANES), 1)
    low = lane < half
    for p in range(N_KV_HEADS):
        c = qi_ref[:, p * LANES:(p + 1) * LANES].astype(f32)
        qil[(2 * p) * Q_ROWS:(2 * p + 1) * Q_ROWS, :] = jnp.where(low, c, 0.0).astype(bf16)
        qil[(2 * p + 1) * Q_ROWS:(2 * p + 2) * Q_ROWS, :] = jnp.where(low, 0.0, c).astype(bf16)
        c = q_ref[:, p * LANES:(p + 1) * LANES].astype(f32)
        ql[p, 0:Q_ROWS, :] = jnp.where(low, c, 0.0).astype(bf16)
        ql[p, Q_ROWS:2 * Q_ROWS, :] = jnp.where(low, 0.0, c).astype(bf16)
    w_head = [kwq_ref[:, IDX_DIM + h:IDX_DIM + h + 1] for h in range(N_IDX_HEADS)]

    nch = ((i + 1) * Q_ROWS + ck - 1) // ck
    row_id = i * Q_ROWS + lax.broadcasted_iota(i32, (Q_ROWS, ck), 0)

    def lane_fold(x):
        return sum(x[:, j * LANES:(j + 1) * LANES] for j in range(ck // LANES))

    def score_chunk(c, carry):
        n_pos, n_nonneg = carry
        s = jnp.dot(qil[...], kit2[c], preferred_element_type=f32)
        acc = None
        for h in range(N_IDX_HEADS):
            term = jnp.maximum(s[h * Q_ROWS:(h + 1) * Q_ROWS], 0.0) * w_head[h]
            acc = term if acc is None else acc + term
        col = c * ck + lax.broadcasted_iota(i32, (Q_ROWS, ck), 1)
        sc = jnp.where(col <= row_id, acc, -jnp.inf)
        scores[c] = sc
        return (n_pos + lane_fold(jnp.where(sc > 0.0, 1.0, 0.0)),
                n_nonneg + lane_fold(jnp.where(sc >= 0.0, 1.0, 0.0)))
    zeros = jnp.zeros((Q_ROWS, LANES), f32)
    n_pos, n_nonneg = lax.fori_loop(0, nch, score_chunk, (zeros, zeros))
    zero_counts = (jnp.sum(n_pos, axis=-1, keepdims=True),
                   jnp.sum(n_nonneg, axis=-1, keepdims=True))

    load = lambda c: scores[c]

    def store(c, val):
        scores[c] = val
    t, c_ge = _kth_largest_score(load, nch, Q_ROWS, ksel, (i + 1) * Q_ROWS > ksel, zero_counts)
    _demote_excess_ties(load, store, nch, Q_ROWS, ksel, t, c_ge)

    def masked_logits(c, p, bias2):
        return jnp.dot(ql[p], kt2[c, p], preferred_element_type=f32) + bias2

    def bias_of(c):
        bias = jnp.where(scores[c] >= t, 0.0, NEG_BIG)
        return jnp.concatenate([bias, bias], axis=0)

    def attend_sweep():
        mx_s[...] = jnp.zeros(mx_s.shape, f32)
        acc_s[...] = jnp.zeros(acc_s.shape, f32)

        def attend_chunk(c, carry):
            bias2 = bias_of(c)
            for p in range(N_KV_HEADS):
                m = m_s[p]
                pr = jnp.exp(masked_logits(c, p, bias2) - jnp.concatenate([m, m], axis=1))
                mx_s[p] = mx_s[p] + pr
                acc_s[p] = acc_s[p] + jnp.dot(
                    pr.astype(bf16), vb[c, :, (p // 2) * LANES:(p // 2 + 1) * LANES],
                    preferred_element_type=f32)
            return carry
        lax.fori_loop(0, nch, attend_chunk, 0)

    smallest = None
    for p in range(N_KV_HEADS):
        qf = ql[p].astype(f32)
        qn2 = jnp.sum(qf * qf, axis=-1, keepdims=True)
        bound = jnp.sqrt(qn2 * jnp.max(kn2[p], axis=-1, keepdims=True))
        m_s[p] = jnp.broadcast_to(bound, m_s.shape[1:])
    attend_sweep()
    for p in range(N_KV_HEADS):
        low_p = jnp.min(jnp.sum(mx_s[p], axis=-1, keepdims=True))
        smallest = low_p if smallest is None else jnp.minimum(smallest, low_p)

    @pl.when(smallest < SUM_EXP_FLOOR)
    def _exact_maxima():
        mx_s[...] = jnp.full(mx_s.shape, NEG_BIG, f32)

        def max_chunk(c, carry):
            bias2 = bias_of(c)
            for p in range(N_KV_HEADS):
                mx_s[p] = jnp.maximum(mx_s[p], masked_logits(c, p, bias2))
            return carry
        lax.fori_loop(0, nch, max_chunk, 0)
        for p in range(N_KV_HEADS):
            m_s[p] = jnp.broadcast_to(jnp.max(mx_s[p], axis=-1, keepdims=True), m_s.shape[1:])
        attend_sweep()

    for p in range(N_KV_HEADS):
        o = acc_s[p] / jnp.sum(mx_s[p], axis=-1, keepdims=True)
        a, b = o[0:Q_ROWS], o[Q_ROWS:2 * Q_ROWS]
        if p % 2 == 0:
            b = pltpu.roll(b, half, 1)
        else:
            a = pltpu.roll(a, half, 1)
        o_ref[:, p * LANES:(p + 1) * LANES] = jnp.where(low, a, b).astype(bf16)


def _prompt_attention(qi, q, kw, k, v, batch, seq, ksel):
    nqb = seq // Q_ROWS
    nck = seq // KEY_CHUNK
    qrow = lambda b, i: (b * nqb + i, 0)
    whole = lambda b, i: (b, 0)
    return pl.pallas_call(
        functools.partial(_prompt_attn_kernel, ksel=ksel),
        grid=(batch, nqb),
        in_specs=[pl.BlockSpec((Q_ROWS, N_IDX_HEADS * IDX_DIM), qrow),
                  pl.BlockSpec((Q_ROWS, ATTN_WIDTH), qrow),
                  pl.BlockSpec((Q_ROWS, LANES), qrow),
                  pl.BlockSpec((seq, LANES), whole),
                  pl.BlockSpec((seq, KV_WIDTH), whole),
                  pl.BlockSpec((seq, KV_WIDTH), whole)],
        out_specs=pl.BlockSpec((Q_ROWS, ATTN_WIDTH), qrow),
        out_shape=jax.ShapeDtypeStruct((batch * seq, ATTN_WIDTH), bf16),
        scratch_shapes=[pltpu.VMEM((nck, LANES, KEY_CHUNK), bf16),
                        pltpu.VMEM((nck, N_KV_HEADS, LANES, KEY_CHUNK), bf16),
                        pltpu.VMEM((nck, KEY_CHUNK, KV_WIDTH), bf16),
                        pltpu.VMEM((N_KV_HEADS, 1, KEY_CHUNK), f32),
                        pltpu.VMEM((nck, Q_ROWS, KEY_CHUNK), f32),
                        pltpu.VMEM((N_IDX_HEADS * Q_ROWS, LANES), bf16),
                        pltpu.VMEM((N_KV_HEADS, 2 * Q_ROWS, LANES), bf16),
                        pltpu.VMEM((N_KV_HEADS, 2 * Q_ROWS, LANES), f32),
                        pltpu.VMEM((N_KV_HEADS, 2 * Q_ROWS, KEY_CHUNK), f32),
                        pltpu.VMEM((N_KV_HEADS, 2 * Q_ROWS, LANES), f32)],
        compiler_params=_params("arbitrary", "arbitrary"),
        name="prompt_attention",
    )(qi, q, kw, kw, k, v)


PAGES_PER_STEP = 32
PAGE = 128
SAMPLE_Q = 8
SAMPLE_ROWS = N_HEADS * SAMPLE_Q
THRESHOLD_ROWS = 128


def _sample_score_kernel(pt_ref, qi_ref, w_ref, kin_ref, *refs):
    pages = refs[:PAGES_PER_STEP]
    out_ref = refs[PAGES_PER_STEP]
    j = pl.program_id(1)
    last = pl.num_programs(1) - 1
    qi = qi_ref[...]
    w = w_ref[...]

    def score(kit):
        s = jnp.dot(qi, kit.astype(bf16), preferred_element_type=f32)
        s = jnp.maximum(s, 0.0) * w
        return jnp.sum(s.reshape(N_IDX_HEADS, SAMPLE_Q, s.shape[-1]), axis=0)

    @pl.when(j < last)
    def _():
        out_ref[...] = score(jnp.concatenate([r[...] for r in pages], axis=1))

    @pl.when(j == last)
    def _():
        sc = score(kin_ref[...])
        col = lax.broadcasted_iota(i32, sc.shape, 1)
        row = lax.broadcasted_iota(i32, sc.shape, 0)
        out_ref[...] = jnp.full(out_ref.shape, -jnp.inf, f32)
        out_ref[:, 0:PAGE] = jnp.where(col <= row, sc, -jnp.inf)


def _page_spec(block, layer, n, n_pages):
    def index(b, j, pt):
        page = pt[b, jnp.minimum(j * PAGES_PER_STEP + n, n_pages - 1)]
        return (layer, page) + (0,) * (len(block) - 2)
    return pl.BlockSpec(block, index)


def _sample_scores(page_table, qi_s, w_s, ki_new_t, idx_cache_t, layer):
    db, n_pages = page_table.shape
    steps = n_pages // PAGES_PER_STEP + 1
    per_b = lambda b, j, pt: (b, 0, 0)
    step_w = PAGES_PER_STEP * PAGE
    return pl.pallas_call(
        _sample_score_kernel,
        grid_spec=pltpu.PrefetchScalarGridSpec(
            num_scalar_prefetch=1,
            grid=(db, steps),
            in_specs=[pl.BlockSpec((None, SAMPLE_ROWS, IDX_DIM), per_b),
                      pl.BlockSpec((None, SAMPLE_ROWS, 1), per_b),
                      pl.BlockSpec((None, IDX_DIM, PAGE), per_b)]
                     + [_page_spec((None, None, IDX_DIM, PAGE), layer, n, n_pages)
                        for n in range(PAGES_PER_STEP)],
            out_specs=pl.BlockSpec((None, SAMPLE_Q, step_w), lambda b, j, pt: (b, 0, j)),
        ),
        out_shape=jax.ShapeDtypeStruct((db, SAMPLE_Q, steps * step_w), f32),
        compiler_params=_params("arbitrary", "arbitrary"),
        name="sample_scores",
    )(page_table, qi_s, w_s, ki_new_t, *([idx_cache_t] * PAGES_PER_STEP))


def _sample_threshold_kernel(scores_in, scores_out, t_out, *, ksel):
    rows, width = scores_in.shape
    nch = width // KEY_CHUNK
    scores_out[...] = scores_in[...]
    load = lambda c: scores_out[:, c * KEY_CHUNK:(c + 1) * KEY_CHUNK]

    def store(c, val):
        scores_out[:, c * KEY_CHUNK:(c + 1) * KEY_CHUNK] = val
    t, c_ge = _kth_largest_score(load, nch, rows, ksel, True)
    _demote_excess_ties(load, store, nch, rows, ksel, t, c_ge)
    t_out[...] = jnp.broadcast_to(t, t_out.shape)


def _sample_threshold(scores2d, ksel):
    n, width = scores2d.shape
    rows = THRESHOLD_ROWS
    return pl.pallas_call(
        functools.partial(_sample_threshold_kernel, ksel=ksel),
        grid=(n // rows,),
        in_specs=[pl.BlockSpec((rows, width), lambda i: (i, 0))],
        out_specs=[pl.BlockSpec((rows, width), lambda i: (i, 0)),
                   pl.BlockSpec((rows, LANES), lambda i: (i, 0))],
        out_shape=[jax.ShapeDtypeStruct((n, width), f32),
                   jax.ShapeDtypeStruct((n, LANES), f32)],
        compiler_params=_params("arbitrary"),
        name="sample_threshold",
    )(scores2d)


def _sample_attn_kernel(pt_ref, q_ref, sc_ref, t_ref, kn_ref, vn_ref, *refs):
    n = PAGES_PER_STEP
    k_pages, v_pages = refs[:n], refs[n:2 * n]
    o_ref, m_s, l_s, acc_s = refs[2 * n:]
    j = pl.program_id(1)
    last = pl.num_programs(1) - 1
    group_rows = 2 * SAMPLE_Q

    @pl.when(j == 0)
    def _():
        m_s[...] = jnp.full(m_s.shape, NEG_BIG, f32)
        l_s[...] = jnp.zeros(l_s.shape, f32)
        acc_s[...] = jnp.zeros(acc_s.shape, f32)

    t = t_ref[:, 0:1]

    def update(kt, vt, sc):
        bias = jnp.where(sc >= t, 0.0, NEG_BIG)
        bias = jnp.concatenate([bias, bias], axis=0)
        lg = jnp.concatenate(
            [jnp.dot(q_ref[h * group_rows:(h + 1) * group_rows, :], kt[h],
                     preferred_element_type=f32) + bias for h in range(N_KV_HEADS)], axis=0)
        m_old = m_s[...]
        m_new = jnp.maximum(m_old, jnp.max(lg, axis=-1, keepdims=True))
        alpha = jnp.exp(m_old - m_new)
        pr = jnp.exp(lg - m_new[:, 0:1])
        l_s[...] = alpha * l_s[...] + jnp.sum(pr, axis=-1, keepdims=True)
        pr = pr.astype(bf16)
        pv = jnp.concatenate(
            [lax.dot_general(pr[h * group_rows:(h + 1) * group_rows], vt[h],
                             (((1,), (1,)), ((), ())), preferred_element_type=f32)
             for h in range(N_KV_HEADS)], axis=0)
        acc_s[...] = acc_s[...] * alpha[:, 0:HEAD_DIM] + pv
        m_s[...] = m_new

    def lanes_of(page_refs):
        return jnp.concatenate([r[...] for r in page_refs], axis=2).astype(bf16)

    @pl.when(j < last)
    def _():
        update(lanes_of(k_pages), lanes_of(v_pages), sc_ref[...])

    @pl.when(j == last)
    def _():
        update(kn_ref[...], vn_ref[...], sc_ref[:, 0:PAGE])
        o = acc_s[...] / l_s[:, 0:HEAD_DIM]
        for h in range(N_HEADS):
            o_ref[:, h * HEAD_DIM:(h + 1) * HEAD_DIM] = o[h * SAMPLE_Q:(h + 1) * SAMPLE_Q]


def _sample_attention(page_table, q_s, scores3d, t3d, k_new_t, v_new_t, k_cache_t, v_cache_t,
                      layer):
    db, n_pages = page_table.shape
    steps = n_pages // PAGES_PER_STEP + 1
    per_b = lambda b, j, pt: (b, 0, 0)
    per_b4 = lambda b, j, pt: (b, 0, 0, 0)
    step_w = PAGES_PER_STEP * PAGE
    page_block = (None, None, N_KV_HEADS, HEAD_DIM, PAGE)
    return pl.pallas_call(
        _sample_attn_kernel,
        grid_spec=pltpu.PrefetchScalarGridSpec(
            num_scalar_prefetch=1,
            grid=(db, steps),
            in_specs=[pl.BlockSpec((None, SAMPLE_ROWS, HEAD_DIM), per_b),
                      pl.BlockSpec((None, SAMPLE_Q, step_w), lambda b, j, pt: (b, 0, j)),
                      pl.BlockSpec((None, SAMPLE_Q, LANES), per_b),
                      pl.BlockSpec((None, N_KV_HEADS, HEAD_DIM, PAGE), per_b4),
                      pl.BlockSpec((None, N_KV_HEADS, HEAD_DIM, PAGE), per_b4)]
                     + [_page_spec(page_block, layer, n, n_pages)
                        for n in range(PAGES_PER_STEP)] * 2,
            out_specs=pl.BlockSpec((None, SAMPLE_Q, ATTN_WIDTH), per_b),
            scratch_shapes=[pltpu.VMEM((SAMPLE_ROWS, LANES), f32),
                            pltpu.VMEM((SAMPLE_ROWS, LANES), f32),
                            pltpu.VMEM((SAMPLE_ROWS, HEAD_DIM), f32)],
        ),
        out_shape=jax.ShapeDtypeStruct((db, SAMPLE_Q, ATTN_WIDTH), f32),
        compiler_params=_params("arbitrary", "arbitrary"),
        name="sample_attention",
    )(page_table, q_s, scores3d, t3d, k_new_t, v_new_t,
      *([k_cache_t] * PAGES_PER_STEP), *([v_cache_t] * PAGES_PER_STEP))


def _pool_diff_prompt(u_ref, hist_ref, seq):
    tm = u_ref.shape[0]
    start = (pl.program_id(0) * tm) % seq
    u = u_ref[...]
    hist = hist_ref[...] * jnp.where(start == 0, 0.0, 1.0)
    ext = jnp.concatenate([hist, u], axis=0)
    pos = start + lax.broadcasted_iota(i32, (tm, 1), 0)
    outs = []
    for g, win in enumerate(POOL_WINDOWS):
        s = ext[:, g * POOL_GROUP:(g + 1) * POOL_GROUP]
        span = 1
        while span < win:
            s = s + pltpu.roll(s, span, 0)
            span *= 2
        cnt = jnp.minimum(pos + 1, win).astype(f32)
        outs.append(s[HIST_ROWS:] / cnt - u[:, g * POOL_GROUP:(g + 1) * POOL_GROUP])
    return outs


def _pool_diff_sample(u_ref):
    nb = u_ref.shape[0]
    outs = []
    for g, win in enumerate(POOL_WINDOWS):
        lanes = slice(g * POOL_GROUP, (g + 1) * POOL_GROUP)
        cur = u_ref[:, HIST_ROWS:HIST_ROWS + 8, lanes]
        s = cur
        for d in range(1, win):
            s = s + u_ref[:, HIST_ROWS - d:HIST_ROWS + 8 - d, lanes]
        outs.append((s / float(win) - cur).reshape(nb * 8, POOL_GROUP))
    return outs


def _merge_kernel(x_ref, ao_ref, u_ref, hist_ref, ga_ref, gb_ref, wpool_ref, pscale_ref,
                  wba_ref, wbp_ref, wout_ref, fg_ref, wr_ref, br_ref,
                  h_ref, hn_ref, route_ref, *, seq):
    diffs = _pool_diff_sample(u_ref) if seq is None else _pool_diff_prompt(u_ref, hist_ref, seq)
    pool = jnp.concatenate(
        [jnp.dot(d.astype(bf16), wpool_ref[g], preferred_element_type=f32)
         for g, d in enumerate(diffs)], axis=1) * pscale_ref[...]
    br_a = jnp.dot(ao_ref[...].astype(bf16), wba_ref[...], preferred_element_type=f32)
    br_p = jnp.dot(pool.astype(bf16), wbp_ref[...], preferred_element_type=f32)
    sig = lambda z: 1.0 / (1.0 + jnp.exp(-z))
    mix = sig(ga_ref[...]) * br_a + sig(gb_ref[...]) * br_p
    h = x_ref[...] + jnp.dot(mix.astype(bf16), wout_ref[...], preferred_element_type=f32)
    h_ref[...] = h
    hn = _rms(h, fg_ref[...])
    hn_ref[...] = hn

    logit = jnp.dot(hn, wr_ref[...], preferred_element_type=f32,
                    precision=lax.Precision.HIGHEST) + br_ref[...]
    lane_i = lax.broadcasted_iota(i32, logit.shape, 1)
    lane = lane_i.astype(f32)
    big = float(LANES)
    gl = jnp.where(lane_i < N_EXPERTS, -jnp.inf,
                   jnp.where(lane_i < N_EXPERTS + N_GROUPS, logit, -jnp.inf))
    gmax = jnp.max(gl, axis=-1, keepdims=True)
    g_sel = jnp.min(jnp.where(gl == gmax, lane, big), axis=-1, keepdims=True) - N_EXPERTS
    g_w = 1.0 / jnp.sum(jnp.exp(gl - gmax), axis=-1, keepdims=True)
    group_of_lane = (lane_i // EXPERTS_PER_GROUP).astype(f32)
    el = jnp.where(group_of_lane == g_sel, logit, -jnp.inf)
    v1 = jnp.max(el, axis=-1, keepdims=True)
    i1 = jnp.min(jnp.where(el == v1, lane, big), axis=-1, keepdims=True)
    el2 = jnp.where(lane == i1, -jnp.inf, el)
    v2 = jnp.max(el2, axis=-1, keepdims=True)
    i2 = jnp.min(jnp.where(el2 == v2, lane, big), axis=-1, keepdims=True)
    e21 = jnp.exp(v2 - v1)
    w1 = g_w / (1.0 + e21)
    w2 = g_w * e21 / (1.0 + e21)
    route_ref[...] = jnp.where(lane_i == 0, i1,
                     jnp.where(lane_i == 1, i2,
                     jnp.where(lane_i == 2, w1, jnp.where(lane_i == 3, w2, 0.0))))


def _merge(x2d, ao, u, ga, gb, lw, seq):
    n = x2d.shape[0]
    tm = PROJ_ROWS
    row = lambda i: (i, 0)
    const2 = lambda i: (0, 0)
    const3 = lambda i: (0, 0, 0)
    if seq is None:
        u_spec = pl.BlockSpec((tm // 8, HIST_ROWS + 8, POOL_WIDTH), lambda i: (i, 0, 0))
        hist = jnp.zeros((HIST_ROWS, POOL_WIDTH), f32)
        hist_spec = pl.BlockSpec((HIST_ROWS, POOL_WIDTH), const2)
    else:
        u_spec = pl.BlockSpec((tm, POOL_WIDTH), row)
        hist = u
        hist_spec = pl.BlockSpec((HIST_ROWS, POOL_WIDTH),
                                 lambda i: (jnp.maximum(i * (tm // HIST_ROWS) - 1, 0), 0))
    return pl.pallas_call(
        functools.partial(_merge_kernel, seq=seq),
        grid=(n // tm,),
        in_specs=[pl.BlockSpec((tm, D_MODEL), row),
                  pl.BlockSpec((tm, ATTN_WIDTH), row),
                  u_spec, hist_spec,
                  pl.BlockSpec((tm, D_MODEL), row),
                  pl.BlockSpec((tm, D_MODEL), row),
                  pl.BlockSpec((4, POOL_GROUP, POOL_GROUP), const3),
                  pl.BlockSpec((1, POOL_WIDTH), const2),
                  pl.BlockSpec((ATTN_WIDTH, D_MODEL), const2),
                  pl.BlockSpec((POOL_WIDTH, D_MODEL), const2),
                  pl.BlockSpec((D_MODEL, D_MODEL), const2),
                  pl.BlockSpec((1, D_MODEL), const2),
                  pl.BlockSpec((D_MODEL, LANES), const2),
                  pl.BlockSpec((1, LANES), const2)],
        out_specs=[pl.BlockSpec((tm, D_MODEL), row),
                   pl.BlockSpec((tm, D_MODEL), row),
                   pl.BlockSpec((tm, LANES), row)],
        out_shape=[jax.ShapeDtypeStruct((n, D_MODEL), f32),
                   jax.ShapeDtypeStruct((n, D_MODEL), f32),
                   jax.ShapeDtypeStruct((n, LANES), f32)],
        compiler_params=_params("arbitrary"),
        name="merge",
    )(x2d, ao, u, hist, ga, gb, lw['w_pool'], lw['pool_scale'], lw['w_ba'], lw['w_bp'],
      lw['w_out'], lw['ffn_g'], lw['w_router'], lw['b_router'])


RANK_ROWS = 512
MOE_BLOCK_PROMPT = 256
MOE_BLOCK_SAMPLE = 128
GATHER_UNROLL = 8


def _rank_kernel(route_ref, rank_ref, counts_ref, base_s):
    @pl.when(pl.program_id(0) == 0)
    def _():
        base_s[...] = jnp.zeros(base_s.shape, f32)
    r = route_ref[...]
    tt = r.shape[0]
    lane = lax.broadcasted_iota(i32, r.shape, 1).astype(f32)
    oh1 = lane == r[:, 0:1]
    oh2 = lane == r[:, 1:2]
    oh = jnp.where(oh1, 1.0, 0.0) + jnp.where(oh2, 1.0, 0.0)
    below = (lax.broadcasted_iota(i32, (tt, tt), 0) > lax.broadcasted_iota(i32, (tt, tt), 1))
    before = jnp.dot(jnp.where(below, 1.0, 0.0).astype(bf16), oh.astype(bf16),
                     preferred_element_type=f32) + base_s[...]
    r1 = jnp.sum(jnp.where(oh1, before, 0.0), axis=-1, keepdims=True)
    r2 = jnp.sum(jnp.where(oh2, before, 0.0), axis=-1, keepdims=True)
    lane_i = lax.broadcasted_iota(i32, r.shape, 1)
    rank_ref[...] = jnp.where(lane_i == 0, r1, jnp.where(lane_i == 1, r2, 0.0))
    base_s[...] = base_s[...] + jnp.sum(oh, axis=0, keepdims=True)
    counts_ref[...] = base_s[...]


def _expert_ranks(route):
    n = route.shape[0]
    tt = min(RANK_ROWS, n)
    return pl.pallas_call(
        _rank_kernel,
        grid=(n // tt,),
        in_specs=[pl.BlockSpec((tt, LANES), lambda i: (i, 0))],
        out_specs=[pl.BlockSpec((tt, LANES), lambda i: (i, 0)),
                   pl.BlockSpec((1, LANES), lambda i: (0, 0))],
        out_shape=[jax.ShapeDtypeStruct((n, LANES), f32),
                   jax.ShapeDtypeStruct((1, LANES), f32)],
        scratch_shapes=[pltpu.VMEM((1, LANES), f32)],
        compiler_params=_params("arbitrary"),
        name="expert_ranks",
    )(route)


INVERT_ROWS = 8192


def _slot_tokens_kernel(dest_ref, zeros_hbm, tok_ref, sem):
    i = pl.program_id(0)
    tile = dest_ref.shape[0]

    @pl.when(i == 0)
    def _():
        fill = pltpu.make_async_copy(zeros_hbm, tok_ref, sem)
        fill.start()
        fill.wait()

    def body(a, carry):
        tok_ref[dest_ref[a]] = (i * tile + a) // TOP_K
        return carry
    lax.fori_loop(0, tile, body, 0, unroll=16)


def _slot_tokens(dest, n_slots):
    n = dest.shape[0]
    tile = min(INVERT_ROWS, n)
    return pl.pallas_call(
        _slot_tokens_kernel,
        grid=(n // tile,),
        in_specs=[pl.BlockSpec((tile,), lambda i: (i,), memory_space=pltpu.SMEM),
                  pl.BlockSpec(memory_space=pl.ANY)],
        out_specs=pl.BlockSpec((n_slots,), lambda i: (0,), memory_space=pltpu.SMEM),
        out_shape=jax.ShapeDtypeStruct((n_slots,), i32),
        scratch_shapes=[pltpu.SemaphoreType.DMA(())],
        compiler_params=_params("arbitrary"),
        name="moe_slot_tokens",
    )(dest, jnp.zeros((n_slots,), i32))


def _expert_ffn_kernel(be_ref, nact_ref, tok_ref, tok_next_ref, x_hbm, wg_ref, wu_ref, wd_ref,
                       ys_ref, xbuf, sem):
    i = pl.program_id(0)
    n_active = nact_ref[0]
    bm = ys_ref.shape[0]
    slot = i % 2

    def gather(tok, s):
        def body(g, carry):
            for j in range(GATHER_UNROLL):
                r = g * GATHER_UNROLL + j
                pltpu.make_async_copy(x_hbm.at[pl.ds(tok[0, r], 1)], xbuf.at[s, pl.ds(r, 1)],
                                      sem.at[s]).start(priority=j % 2)
            return carry
        lax.fori_loop(0, bm // GATHER_UNROLL, body, 0)

    @pl.when(i == 0)
    def _():
        gather(tok_ref, 0)

    @pl.when(i + 1 < n_active)
    def _():
        gather(tok_next_ref, 1 - slot)

    @pl.when(i < n_active)
    def _():
        pltpu.make_async_copy(x_hbm.at[pl.ds(0, bm)], xbuf.at[slot], sem.at[slot]).wait()
        x = xbuf[slot].astype(bf16)
        g = jnp.dot(x, wg_ref[...], preferred_element_type=f32)
        u = jnp.dot(x, wu_ref[...], preferred_element_type=f32)
        hb = (g / (1.0 + jnp.exp(-g))) * u
        ys_ref[...] = jnp.dot(hb.astype(bf16), wd_ref[...], preferred_element_type=f32)

    @pl.when(i >= n_active)
    def _():
        ys_ref[...] = jnp.zeros(ys_ref.shape, f32)


def _expert_ffn(block_expert, n_active, slot_tok, x, wg, wu, wd, bm):
    n_blocks = slot_tok.shape[0] // bm
    tok3 = slot_tok.reshape(n_blocks, 1, bm)
    by_expert = lambda i, be, na: (be[i], 0, 0)
    smem_block = lambda index: pl.BlockSpec((None, 1, bm), index, memory_space=pltpu.SMEM)
    return pl.pallas_call(
        _expert_ffn_kernel,
        grid_spec=pltpu.PrefetchScalarGridSpec(
            num_scalar_prefetch=2,
            grid=(n_blocks,),
            in_specs=[smem_block(lambda i, be, na: (i, 0, 0)),
                      smem_block(lambda i, be, na: (jnp.minimum(i + 1, n_blocks - 1), 0, 0)),
                      pl.BlockSpec(memory_space=pl.ANY),
                      pl.BlockSpec((None, D_MODEL, D_EXPERT), by_expert),
                      pl.BlockSpec((None, D_MODEL, D_EXPERT), by_expert),
                      pl.BlockSpec((None, D_EXPERT, D_MODEL), by_expert)],
            out_specs=pl.BlockSpec((bm, D_MODEL), lambda i, be, na: (i, 0)),
            scratch_shapes=[pltpu.VMEM((2, bm, D_MODEL), f32),
                            pltpu.SemaphoreType.DMA((2,))],
        ),
        out_shape=jax.ShapeDtypeStruct((n_blocks * bm, D_MODEL), f32),
        compiler_params=_params("arbitrary"),
        name="expert_ffn",
    )(block_expert, n_active, tok3, tok3, x, wg, wu, wd)


COMBINE_ROWS = 128


def _combine_kernel(dest_ref, dest_next_ref, h_ref, route_ref, g_ref, ys_hbm, y_ref, buf, sem):
    i = pl.program_id(0)
    tt = h_ref.shape[0]
    slot = i % 2

    def gather(dest, s):
        def body(g, carry):
            for k in range(GATHER_UNROLL // TOP_K):
                t = g * (GATHER_UNROLL // TOP_K) + k
                for j in range(TOP_K):
                    pltpu.make_async_copy(ys_hbm.at[pl.ds(dest[0, TOP_K * t + j], 1)],
                                          buf.at[s, j, pl.ds(t, 1)], sem.at[s]).start(priority=j)
            return carry
        lax.fori_loop(0, tt * TOP_K // GATHER_UNROLL, body, 0)

    @pl.when(i == 0)
    def _():
        gather(dest_ref, 0)

    @pl.when(i + 1 < pl.num_programs(0))
    def _():
        gather(dest_next_ref, 1 - slot)

    for j in range(TOP_K):
        pltpu.make_async_copy(ys_hbm.at[pl.ds(0, tt)], buf.at[slot, j], sem.at[slot]).wait()
    r = route_ref[...]
    moe = buf[slot, 0] * r[:, 2:3] + buf[slot, 1] * r[:, 3:4]
    y_ref[...] = _rms(h_ref[...] + moe, g_ref[...])


def _combine(dest, h, route, final_g, ys):
    n = h.shape[0]
    tt = COMBINE_ROWS
    n_tiles = n // tt
    dest3 = dest.reshape(n_tiles, 1, tt * TOP_K)
    row = lambda i: (i, 0)
    smem_block = lambda index: pl.BlockSpec((None, 1, tt * TOP_K), index, memory_space=pltpu.SMEM)
    return pl.pallas_call(
        _combine_kernel,
        grid=(n_tiles,),
        in_specs=[smem_block(lambda i: (i, 0, 0)),
                  smem_block(lambda i: (jnp.minimum(i + 1, n_tiles - 1), 0, 0)),
                  pl.BlockSpec((tt, D_MODEL), row),
                  pl.BlockSpec((tt, LANES), row),
                  pl.BlockSpec((1, D_MODEL), lambda i: (0, 0)),
                  pl.BlockSpec(memory_space=pl.ANY)],
        out_specs=pl.BlockSpec((tt, D_MODEL), row),
        out_shape=jax.ShapeDtypeStruct((n, D_MODEL), f32),
        scratch_shapes=[pltpu.VMEM((2, TOP_K, tt, D_MODEL), f32),
                        pltpu.SemaphoreType.DMA((2,))],
        compiler_params=_params("arbitrary"),
        name="moe_combine",
    )(dest3, dest3, h, route, final_g, ys)


def _moe_and_final_norm(h, hn, route, lw, final_g, bm):
    n = h.shape[0]
    rank, counts = _expert_ranks(route)
    counts = counts[0, :N_EXPERTS].astype(i32)
    padded = (counts + bm - 1) // bm * bm
    pend = jnp.cumsum(padded)
    pstart = pend - padded
    e_id = route[:, 0:TOP_K].astype(i32)
    dest = (pstart[e_id] + rank[:, 0:TOP_K].astype(i32)).reshape(-1)
    n_blocks = (n * TOP_K + N_EXPERTS * (bm - 1)) // bm + 1
    n_active = (pend[-1] // bm).astype(i32)
    blk = jnp.minimum(jnp.arange(n_blocks, dtype=i32), n_active - 1) * bm
    block_expert = jnp.minimum(jnp.sum((pend[None, :] <= blk[:, None]).astype(i32), axis=1),
                               N_EXPERTS - 1)
    slot_tok = _slot_tokens(dest, n_blocks * bm)
    ys = _expert_ffn(block_expert, n_active.reshape(1), slot_tok, hn,
                     lw['w_eg'], lw['w_eu'], lw['w_ed'], bm)
    return _combine(dest, h, route, final_g, ys)


def _rope_tables(pos, reps):
    half = HEAD_DIM // 2
    inv_freq = 1.0 / (ROPE_THETA ** (jnp.arange(half, dtype=f32) / half))
    ang = pos.astype(f32)[:, None] * inv_freq[None, :]
    cos, sin = jnp.cos(ang), jnp.sin(ang)
    cos = jnp.concatenate([cos, cos, cos, cos], axis=1)
    sin = jnp.concatenate([-sin, sin, -sin, sin], axis=1)
    return jnp.tile(cos, (reps, 1)), jnp.tile(sin, (reps, 1))


def _layer_weights(l, attn_norm_g, w_in, w_pool, pool_scale, w_branch_attn, w_branch_pool, w_out,
                   ffn_norm_g, w_rg, b_rg, w_re, b_re, w_eg, w_eu, w_ed):
    cuts = [0]
    for s in SEGMENTS:
        cuts.append(cuts[-1] + s)
    w = w_in[l]
    kw_pad = LANES - IDX_DIM - N_IDX_HEADS
    w_packed = jnp.concatenate(
        [w[:, cuts[0]:cuts[4]], w[:, cuts[4]:cuts[6]], jnp.zeros((D_MODEL, kw_pad), f32),
         w[:, cuts[6]:]], axis=1).astype(bf16)
    r_pad = LANES - N_EXPERTS - N_GROUPS
    return dict(
        attn_g=attn_norm_g[l].reshape(1, D_MODEL), w_packed=w_packed,
        w_pool=w_pool[l].astype(bf16), pool_scale=pool_scale[l].reshape(1, POOL_WIDTH),
        w_ba=w_branch_attn[l].astype(bf16), w_bp=w_branch_pool[l].astype(bf16),
        w_out=w_out[l].astype(bf16), ffn_g=ffn_norm_g[l].reshape(1, D_MODEL),
        w_router=jnp.concatenate([w_re[l], w_rg[l], jnp.zeros((D_MODEL, r_pad), f32)], axis=1),
        b_router=jnp.concatenate([b_re[l], b_rg[l], jnp.zeros((r_pad,), f32)]).reshape(1, LANES),
        w_eg=w_eg[l].astype(bf16), w_eu=w_eu[l].astype(bf16), w_ed=w_ed[l].astype(bf16))


def _prompt_layer(x, lw, final_g):
    batch, seq, _ = x.shape
    x2d = x.reshape(batch * seq, D_MODEL)
    cos, sin = _rope_tables(jnp.arange(seq), 1)
    q, k, v, qi, kw, u, ga, gb = _project(x2d, lw['attn_g'], lw['w_packed'], cos, sin)
    ao = _prompt_attention(qi, q, kw, k, v, batch, seq, min(TOPK_MAX, seq // 4))
    h, hn, route = _merge(x2d, ao, u, ga, gb, lw, seq)
    y = _moe_and_final_norm(h, hn, route, lw, final_g, bm=MOE_BLOCK_PROMPT)
    return (y.reshape(batch, seq, D_MODEL),
            k.reshape(batch, seq, N_KV_HEADS, HEAD_DIM), v.reshape(batch, seq, N_KV_HEADS, HEAD_DIM),
            kw[:, :IDX_DIM].reshape(batch, seq, IDX_DIM),
            u.reshape(batch, seq, POOL_WIDTH)[:, seq - POOL_HIST:])


def _sample_layer(x, l, cache_k, cache_v, cache_idx_k, state_pool, page_table, lw, final_g):
    db, ds, _ = x.shape
    page = cache_k.shape[2]
    past = page_table.shape[1] * page
    x2d = x.reshape(db * ds, D_MODEL)
    cos, sin = _rope_tables(past + jnp.arange(ds), PROJ_ROWS // ds)
    q, k, v, qi, kw, u, ga, gb = _project(x2d, lw['attn_g'], lw['w_packed'], cos, sin)

    rows_hq = lambda a, nh, d: a.reshape(db, ds, nh, d).transpose(0, 2, 1, 3).reshape(db, nh * ds, d)
    qi_s = rows_hq(qi, N_IDX_HEADS, IDX_DIM)
    w_s = kw[:, IDX_DIM:IDX_DIM + N_IDX_HEADS].reshape(db, ds, N_IDX_HEADS)
    w_s = w_s.transpose(0, 2, 1).reshape(db, N_IDX_HEADS * ds, 1)
    pad_slots = lambda a: jnp.pad(a, [(0, 0)] * (a.ndim - 1) + [(0, page - ds)])
    ki_new_t = pad_slots(kw[:, :IDX_DIM].reshape(db, ds, IDX_DIM).transpose(0, 2, 1))
    scores = _sample_scores(page_table, qi_s, w_s, ki_new_t,
                            cache_idx_k.transpose(0, 1, 3, 2), l)
    width = scores.shape[-1]
    scores, t = _sample_threshold(scores.reshape(db * ds, width),
                                  min(TOPK_MAX, (past + ds) // 4))

    as_page = lambda a: pad_slots(
        a.reshape(db, ds, N_KV_HEADS, HEAD_DIM).transpose(0, 2, 3, 1)).astype(bf16)
    ao = _sample_attention(page_table, rows_hq(q, N_HEADS, HEAD_DIM),
                           scores.reshape(db, ds, width), t.reshape(db, ds, LANES),
                           as_page(k), as_page(v),
                           cache_k.transpose(0, 1, 3, 4, 2), cache_v.transpose(0, 1, 3, 4, 2), l)

    u3 = u.reshape(db, ds, POOL_WIDTH)
    u_all = jnp.concatenate([jnp.zeros((db, 1, POOL_WIDTH), f32), state_pool[l], u3], axis=1)
    h, hn, route = _merge(x2d, ao.reshape(db * ds, ATTN_WIDTH), u_all, ga, gb, lw, None)
    y = _moe_and_final_norm(h, hn, route, lw, final_g, bm=MOE_BLOCK_SAMPLE)
    return (y.reshape(db, ds, D_MODEL),
            k.reshape(db, ds, N_KV_HEADS, HEAD_DIM), v.reshape(db, ds, N_KV_HEADS, HEAD_DIM),
            kw[:, :IDX_DIM].reshape(db, ds, IDX_DIM),
            u_all[:, -POOL_HIST:])


def kernel(x_prompt, x_sample, cache_k, cache_v, cache_idx_k, state_pool, page_table, attn_norm_g, w_in, w_pool, pool_scale, w_branch_attn, w_branch_pool, w_out, ffn_norm_g, w_router_group, b_router_group, w_router_expert, b_router_expert, w_exp_gate, w_exp_up, w_exp_down, final_norm_g):
    depth = w_in.shape[0]
    assert depth == 1, "the final norm is fused into the (single) layer's MoE combine"
    final_g = final_norm_g.reshape(1, D_MODEL)
    lw = _layer_weights(0, attn_norm_g, w_in, w_pool, pool_scale, w_branch_attn, w_branch_pool,
                        w_out, ffn_norm_g, w_router_group, b_router_group, w_router_expert,
                        b_router_expert, w_exp_gate, w_exp_up, w_exp_down)
    yp, kp, vp, kip, up = _prompt_layer(x_prompt, lw, final_g)
    ys, ks, vs, kis, us = _sample_layer(x_sample, 0, cache_k, cache_v, cache_idx_k, state_pool,
                                        page_table, lw, final_g)
    stack = lambda a: a[None]
    return (yp, ys, stack(kp), stack(vp), stack(kip), stack(up),
            stack(ks), stack(vs), stack(kis), stack(us))
```

```python
import functools

import jax
import jax.numpy as jnp
from jax import lax
from jax.experimental import pallas as pl
from jax.experimental.pallas import tpu as pltpu

D_MODEL = 1024
HEAD_DIM = 64
N_HEADS = 8
N_KV_HEADS = 4
ATTN_WIDTH = N_HEADS * HEAD_DIM
KV_WIDTH = N_KV_HEADS * HEAD_DIM
N_IDX_HEADS = 8
IDX_DIM = 64
TOPK_MAX = 256
ROPE_THETA = 10000.0
POOL_WINDOWS = (2, 4, 8, 16)
POOL_WIDTH = 512
POOL_GROUP = 128
POOL_HIST = 15
N_GROUPS = 4
EXPERTS_PER_GROUP = 8
N_EXPERTS = 32
TOP_K = 2
D_EXPERT = 512
RMS_EPS = 1e-6
SEGMENTS = (ATTN_WIDTH, KV_WIDTH, KV_WIDTH, N_IDX_HEADS * IDX_DIM, IDX_DIM, N_IDX_HEADS,
            POOL_WIDTH, D_MODEL, D_MODEL)

LANES = 128
VMEM_LIMIT = 56 * 1024 * 1024
KEY_CHUNK = 256
Q_ROWS = 256
PROJ_ROWS = 256
HIST_ROWS = 16
NEG_BIG = -1e30
INT_MIN = -2 ** 31
KEY_NEG_INF = -2139095041
FLT_LOWEST = -3.4028234663852886e38
SUM_EXP_FLOOR = 1e-22

_SEG = dict(q=(0, 512), k=(512, 768), v=(768, 1024), qi=(1024, 1536), kw=(1536, 1664),
            u=(1664, 2176), ga=(2176, 3200), gb=(3200, 4224))
PROJ_PACKED = 4224

f32 = jnp.float32
bf16 = jnp.bfloat16
i32 = jnp.int32


def _params(*sem):
    return pltpu.CompilerParams(dimension_semantics=sem, vmem_limit_bytes=VMEM_LIMIT)


def _rms(x, g):
    return x * lax.rsqrt(jnp.mean(x * x, axis=-1, keepdims=True) + RMS_EPS) * g


def _key_to_float(k):
    return lax.bitcast_convert_type(k ^ ((k >> 31) & 0x7FFFFFFF), f32)


def _rope(x, cos, sin_signed):
    lane = lax.broadcasted_iota(i32, (x.shape[0], LANES), 1)
    first_half = (lane % HEAD_DIM) < HEAD_DIM // 2
    outs = []
    for c in range(x.shape[1] // LANES):
        xc = x[:, c * LANES:(c + 1) * LANES]
        partner = jnp.where(first_half, pltpu.roll(xc, LANES - HEAD_DIM // 2, 1),
                            pltpu.roll(xc, HEAD_DIM // 2, 1))
        outs.append(xc * cos + partner * sin_signed)
    return outs[0] if len(outs) == 1 else jnp.concatenate(outs, axis=1)


def _proj_kernel(x_ref, g_ref, w_ref, cos_ref, sin_ref,
                 q_ref, k_ref, v_ref, qi_ref, kw_ref, u_ref, ga_ref, gb_ref):
    xb = _rms(x_ref[...], g_ref[...]).astype(bf16)
    cos = cos_ref[...]
    sin = sin_ref[...]

    def seg(name):
        a, b = _SEG[name]
        return jnp.dot(xb, w_ref[:, a:b], preferred_element_type=f32)

    q_ref[...] = (_rope(seg('q'), cos, sin) * (HEAD_DIM ** -0.5)).astype(bf16)
    k_ref[...] = _rope(seg('k'), cos, sin)
    v_ref[...] = seg('v')
    qi_ref[...] = (_rope(seg('qi'), cos, sin) * (IDX_DIM ** -0.5)).astype(bf16)
    kw = seg('kw')
    lane = lax.broadcasted_iota(i32, kw.shape, 1)
    kw_ref[...] = jnp.where(lane < IDX_DIM, _rope(kw, cos, sin), kw * (N_IDX_HEADS ** -0.5))
    u_ref[...] = seg('u')
    ga_ref[...] = seg('ga')
    gb_ref[...] = seg('gb')


def _project(x2d, g, w_packed, cos_tab, sin_tab):
    n = x2d.shape[0]
    tm = PROJ_ROWS
    ntab = cos_tab.shape[0] // tm
    row = lambda i: (i, 0)
    const = lambda i: (0, 0)
    tab = lambda i: (i % ntab, 0)
    widths = tuple(b - a for a, b in _SEG.values())
    dtypes = (bf16, f32, f32, bf16, f32, f32, f32, f32)
    return pl.pallas_call(
        _proj_kernel,
        grid=(n // tm,),
        in_specs=[pl.BlockSpec((tm, D_MODEL), row),
                  pl.BlockSpec((1, D_MODEL), const),
                  pl.BlockSpec((D_MODEL, PROJ_PACKED), const),
                  pl.BlockSpec((tm, LANES), tab),
                  pl.BlockSpec((tm, LANES), tab)],
        out_specs=[pl.BlockSpec((tm, w), row) for w in widths],
        out_shape=[jax.ShapeDtypeStruct((n, w), d) for w, d in zip(widths, dtypes)],
        compiler_params=_params("arbitrary"),
        name="project",
    )(x2d, g, w_packed, cos_tab, sin_tab)


def _count(load_chunk, nch, rows, pred):
    def body(c, acc):
        ind = jnp.where(pred(load_chunk(c), c), 1.0, 0.0)
        for s in range(KEY_CHUNK // LANES):
            acc = acc + ind[:, s * LANES:(s + 1) * LANES]
        return acc
    acc = jnp.zeros((rows, LANES), f32)
    if isinstance(nch, int):
        for c in range(nch):
            acc = body(c, acc)
    else:
        acc = lax.fori_loop(0, nch, body, acc)
    return jnp.sum(acc, axis=-1, keepdims=True)


def _kth_largest_score(load_chunk, nch, rows, ksel, search, zero_counts=None):
    if zero_counts is None:
        zero_counts = (_count(load_chunk, nch, rows, lambda sc, _: sc > 0.0),
                       _count(load_chunk, nch, rows, lambda sc, _: sc >= 0.0))
    c_pos, c_nonneg = zero_counts
    at_zero = jnp.logical_and(c_pos < ksel, c_nonneg >= ksel)

    def cond(state):
        it, _, cnt = state
        return jnp.logical_and(it < 32, jnp.max(jnp.where(at_zero, 0.0, cnt)) > ksel)

    def body(state):
        it, key, cnt = state
        cand = key + lax.shift_left(jnp.int32(1), 31 - it)
        cand_f = _key_to_float(cand)
        c = _count(load_chunk, nch, rows, lambda sc, _: sc >= cand_f)
        take = c >= ksel
        return it + 1, jnp.where(take, cand, key), jnp.where(take, c, cnt)

    total = jnp.where(search, nch * KEY_CHUNK, 0).astype(f32)
    _, key, cnt = lax.while_loop(
        cond, body, (jnp.int32(0), jnp.full((rows, 1), INT_MIN, i32), jnp.full((rows, 1), total)))
    lifted = key <= KEY_NEG_INF
    t = jnp.where(lifted, FLT_LOWEST, _key_to_float(key))
    cnt = jnp.where(lifted, 0.0, cnt)
    at_zero = jnp.logical_and(at_zero, search)
    return jnp.where(at_zero, 0.0, t), jnp.where(at_zero, c_nonneg, cnt)


def _demote_excess_ties(load_chunk, store_chunk, nch, rows, ksel, t, c_ge):
    @pl.when(jnp.max(c_ge) > ksel)
    def _():
        need = ksel - _count(load_chunk, nch, rows, lambda sc, c: sc > t)
        upto = (lax.broadcasted_iota(i32, (KEY_CHUNK, KEY_CHUNK), 0)
                <= lax.broadcasted_iota(i32, (KEY_CHUNK, KEY_CHUNK), 1))
        upto = jnp.where(upto, 1.0, 0.0).astype(bf16)

        def rewrite(c, seen):
            sc = load_chunk(c)
            tie = sc == t
            rank = seen + jnp.dot(jnp.where(tie, 1.0, 0.0).astype(bf16), upto,
                                  preferred_element_type=f32)
            store_chunk(c, jnp.where(jnp.where(tie, rank, 0.0) > need, -jnp.inf, sc))
            return rank[:, KEY_CHUNK - 1:KEY_CHUNK]
        seen = jnp.zeros((rows, 1), f32)
        if isinstance(nch, int):
            for c in range(nch):
                seen = rewrite(c, seen)
        else:
            lax.fori_loop(0, nch, rewrite, seen)


def _prompt_attn_kernel(qi_ref, q_ref, kwq_ref, kw_ref, k_ref, v_ref, o_ref,
                        kit2, kt2, vb, kn2, scores, qil, ql, m_s, mx_s, acc_s, *, ksel):
    i = pl.program_id(1)
    seq = kw_ref.shape[0]
    ck = KEY_CHUNK
    half = HEAD_DIM

    @pl.when(i == 0)
    def _prepare_keys():
        def body(c, carry):
            rows = pl.ds(pl.multiple_of(c * ck, ck), ck)
            kit = kw_ref[rows, :].T[0:half].astype(bf16)
            kit2[c, 0:half, :] = kit
            kit2[c, half:2 * half, :] = kit
            kc = k_ref[rows, :]
            for pair in range(2):
                kt = kc[:, pair * LANES:(pair + 1) * LANES].T.astype(bf16)
                for sub in range(2):
                    one = kt[sub * half:(sub + 1) * half]
                    kt2[c, 2 * pair + sub, 0:half, :] = one
                    kt2[c, 2 * pair + sub, half:2 * half, :] = one
                    sq = one.astype(f32)
                    sq = jnp.sum(sq * sq, axis=0, keepdims=True)
                    kn2[2 * pair + sub] = jnp.maximum(kn2[2 * pair + sub], sq)
            vb[c] = v_ref[rows, :].astype(bf16)
            return carry
        kn2[...] = jnp.zeros(kn2.shape, f32)
        lax.fori_loop(0, seq // ck, body, 0)

    lane = lax.broadcasted_iota(i32, (Q_ROWS, LANES), 1)
    low = lane < half
    for p in range(N_KV_HEADS):
        c = qi_ref[:, p * LANES:(p + 1) * LANES].astype(f32)
        qil[(2 * p) * Q_ROWS:(2 * p + 1) * Q_ROWS, :] = jnp.where(low, c, 0.0).astype(bf16)
        qil[(2 * p + 1) * Q_ROWS:(2 * p + 2) * Q_ROWS, :] = jnp.where(low, 0.0, c).astype(bf16)
        c = q_ref[:, p * LANES:(p + 1) * LANES].astype(f32)
        ql[p, 0:Q_ROWS, :] = jnp.where(low, c, 0.0).astype(bf16)
        ql[p, Q_ROWS:2 * Q_ROWS, :] = jnp.where(low, 0.0, c).astype(bf16)
    w_head = [kwq_ref[:, IDX_DIM + h:IDX_DIM + h + 1] for h in range(N_IDX_HEADS)]

    nch = ((i + 1) * Q_ROWS + ck - 1) // ck
    row_id = i * Q_ROWS + lax.broadcasted_iota(i32, (Q_ROWS, ck), 0)

    def lane_fold(x):
        return sum(x[:, j * LANES:(j + 1) * LANES] for j in range(ck // LANES))

    def score_chunk(c, carry):
        n_pos, n_nonneg = carry
        s = jnp.dot(qil[...], kit2[c], preferred_element_type=f32)
        acc = None
        for h in range(N_IDX_HEADS):
            term = jnp.maximum(s[h * Q_ROWS:(h + 1) * Q_ROWS], 0.0) * w_head[h]
            acc = term if acc is None else acc + term
        col = c * ck + lax.broadcasted_iota(i32, (Q_ROWS, ck), 1)
        sc = jnp.where(col <= row_id, acc, -jnp.inf)
        scores[c] = sc
        return (n_pos + lane_fold(jnp.where(sc > 0.0, 1.0, 0.0)),
                n_nonneg + lane_fold(jnp.where(sc >= 0.0, 1.0, 0.0)))
    zeros = jnp.zeros((Q_ROWS, LANES), f32)
    n_pos, n_nonneg = lax.fori_loop(0, nch, score_chunk, (zeros, zeros))
    zero_counts = (jnp.sum(n_pos, axis=-1, keepdims=True),
                   jnp.sum(n_nonneg, axis=-1, keepdims=True))

    load = lambda c: scores[c]

    def store(c, val):
        scores[c] = val
    t, c_ge = _kth_largest_score(load, nch, Q_ROWS, ksel, (i + 1) * Q_ROWS > ksel, zero_counts)
    _demote_excess_ties(load, store, nch, Q_ROWS, ksel, t, c_ge)

    def masked_logits(c, p, bias2):
        return jnp.dot(ql[p], kt2[c, p], preferred_element_type=f32) + bias2

    def bias_of(c):
        bias = jnp.where(scores[c] >= t, 0.0, NEG_BIG)
        return jnp.concatenate([bias, bias], axis=0)

    def attend_sweep():
        mx_s[...] = jnp.zeros(mx_s.shape, f32)
        acc_s[...] = jnp.zeros(acc_s.shape, f32)

        def attend_chunk(c, carry):
            bias2 = bias_of(c)
            for p in range(N_KV_HEADS):
                m = m_s[p]
                pr = jnp.exp(masked_logits(c, p, bias2) - jnp.concatenate([m, m], axis=1))
                mx_s[p] = mx_s[p] + pr
                acc_s[p] = acc_s[p] + jnp.dot(
                    pr.astype(bf16), vb[c, :, (p // 2) * LANES:(p // 2 + 1) * LANES],
                    preferred_element_type=f32)
            return carry
        lax.fori_loop(0, nch, attend_chunk, 0)

    smallest = None
    for p in range(N_KV_HEADS):
        qf = ql[p].astype(f32)
        qn2 = jnp.sum(qf * qf, axis=-1, keepdims=True)
        bound = jnp.sqrt(qn2 * jnp.max(kn2[p], axis=-1, keepdims=True))
        m_s[p] = jnp.broadcast_to(bound, m_s.shape[1:])
    attend_sweep()
    for p in range(N_KV_HEADS):
        low_p = jnp.min(jnp.sum(mx_s[p], axis=-1, keepdims=True))
        smallest = low_p if smallest is None else jnp.minimum(smallest, low_p)

    @pl.when(smallest < SUM_EXP_FLOOR)
    def _exact_maxima():
        mx_s[...] = jnp.full(mx_s.shape, NEG_BIG, f32)

        def max_chunk(c, carry):
            bias2 = bias_of(c)
            for p in range(N_KV_HEADS):
                mx_s[p] = jnp.maximum(mx_s[p], masked_logits(c, p, bias2))
            return carry
        lax.fori_loop(0, nch, max_chunk, 0)
        for p in range(N_KV_HEADS):
            m_s[p] = jnp.broadcast_to(jnp.max(mx_s[p], axis=-1, keepdims=True), m_s.shape[1:])
        attend_sweep()

    for p in range(N_KV_HEADS):
        o = acc_s[p] / jnp.sum(mx_s[p], axis=-1, keepdims=True)
        a, b = o[0:Q_ROWS], o[Q_ROWS:2 * Q_ROWS]
        if p % 2 == 0:
            b = pltpu.roll(b, half, 1)
        else:
            a = pltpu.roll(a, half, 1)
        o_ref[:, p * LANES:(p + 1) * LANES] = jnp.where(low, a, b).astype(bf16)


def _prompt_attention(qi, q, kw, k, v, batch, seq, ksel):
    nqb = seq // Q_ROWS
    nck = seq // KEY_CHUNK
    qrow = lambda b, i: (b * nqb + i, 0)
    whole = lambda b, i: (b, 0)
    return pl.pallas_call(
        functools.partial(_prompt_attn_kernel, ksel=ksel),
        grid=(batch, nqb),
        in_specs=[pl.BlockSpec((Q_ROWS, N_IDX_HEADS * IDX_DIM), qrow),
                  pl.BlockSpec((Q_ROWS, ATTN_WIDTH), qrow),
                  pl.BlockSpec((Q_ROWS, LANES), qrow),
                  pl.BlockSpec((seq, LANES), whole),
                  pl.BlockSpec((seq, KV_WIDTH), whole),
                  pl.BlockSpec((seq, KV_WIDTH), whole)],
        out_specs=pl.BlockSpec((Q_ROWS, ATTN_WIDTH), qrow),
        out_shape=jax.ShapeDtypeStruct((batch * seq, ATTN_WIDTH), bf16),
        scratch_shapes=[pltpu.VMEM((nck, LANES, KEY_CHUNK), bf16),
                        pltpu.VMEM((nck, N_KV_HEADS, LANES, KEY_CHUNK), bf16),
                        pltpu.VMEM((nck, KEY_CHUNK, KV_WIDTH), bf16),
                        pltpu.VMEM((N_KV_HEADS, 1, KEY_CHUNK), f32),
                        pltpu.VMEM((nck, Q_ROWS, KEY_CHUNK), f32),
                        pltpu.VMEM((N_IDX_HEADS * Q_ROWS, LANES), bf16),
                        pltpu.VMEM((N_KV_HEADS, 2 * Q_ROWS, LANES), bf16),
                        pltpu.VMEM((N_KV_HEADS, 2 * Q_ROWS, LANES), f32),
                        pltpu.VMEM((N_KV_HEADS, 2 * Q_ROWS, KEY_CHUNK), f32),
                        pltpu.VMEM((N_KV_HEADS, 2 * Q_ROWS, LANES), f32)],
        compiler_params=_params("arbitrary", "arbitrary"),
        name="prompt_attention",
    )(qi, q, kw, kw, k, v)


PAGES_PER_STEP = 16
PAGE = 128
SAMPLE_Q = 8
SAMPLE_ROWS = N_HEADS * SAMPLE_Q
THRESHOLD_ROWS = 128


def _sample_score_kernel(pt_ref, qi_ref, w_ref, kin_ref, *refs):
    pages = refs[:PAGES_PER_STEP]
    out_ref = refs[PAGES_PER_STEP]
    j = pl.program_id(1)
    last = pl.num_programs(1) - 1
    qi = qi_ref[...]
    w = w_ref[...]

    def score(kit):
        s = jnp.dot(qi, kit.astype(bf16), preferred_element_type=f32)
        s = jnp.maximum(s, 0.0) * w
        return jnp.sum(s.reshape(N_IDX_HEADS, SAMPLE_Q, s.shape[-1]), axis=0)

    @pl.when(j < last)
    def _():
        out_ref[...] = score(jnp.concatenate([r[...] for r in pages], axis=1))

    @pl.when(j == last)
    def _():
        sc = score(kin_ref[...])
        col = lax.broadcasted_iota(i32, sc.shape, 1)
        row = lax.broadcasted_iota(i32, sc.shape, 0)
        out_ref[...] = jnp.full(out_ref.shape, -jnp.inf, f32)
        out_ref[:, 0:PAGE] = jnp.where(col <= row, sc, -jnp.inf)


def _page_spec(block, layer, n, n_pages):
    def index(b, j, pt):
        page = pt[b, jnp.minimum(j * PAGES_PER_STEP + n, n_pages - 1)]
        return (layer, page) + (0,) * (len(block) - 2)
    return pl.BlockSpec(block, index)


def _sample_scores(page_table, qi_s, w_s, ki_new_t, idx_cache_t, layer):
    db, n_pages = page_table.shape
    steps = n_pages // PAGES_PER_STEP + 1
    per_b = lambda b, j, pt: (b, 0, 0)
    step_w = PAGES_PER_STEP * PAGE
    return pl.pallas_call(
        _sample_score_kernel,
        grid_spec=pltpu.PrefetchScalarGridSpec(
            num_scalar_prefetch=1,
            grid=(db, steps),
            in_specs=[pl.BlockSpec((None, SAMPLE_ROWS, IDX_DIM), per_b),
                      pl.BlockSpec((None, SAMPLE_ROWS, 1), per_b),
                      pl.BlockSpec((None, IDX_DIM, PAGE), per_b)]
                     + [_page_spec((None, None, IDX_DIM, PAGE), layer, n, n_pages)
                        for n in range(PAGES_PER_STEP)],
            out_specs=pl.BlockSpec((None, SAMPLE_Q, step_w), lambda b, j, pt: (b, 0, j)),
        ),
        out_shape=jax.ShapeDtypeStruct((db, SAMPLE_Q, steps * step_w), f32),
        compiler_params=_params("arbitrary", "arbitrary"),
        name="sample_scores",
    )(page_table, qi_s, w_s, ki_new_t, *([idx_cache_t] * PAGES_PER_STEP))


def _sample_threshold_kernel(scores_in, scores_out, t_out, *, ksel):
    rows, width = scores_in.shape
    nch = width // KEY_CHUNK
    scores_out[...] = scores_in[...]
    load = lambda c: scores_out[:, c * KEY_CHUNK:(c + 1) * KEY_CHUNK]

    def store(c, val):
        scores_out[:, c * KEY_CHUNK:(c + 1) * KEY_CHUNK] = val
    t, c_ge = _kth_largest_score(load, nch, rows, ksel, True)
    _demote_excess_ties(load, store, nch, rows, ksel, t, c_ge)
    t_out[...] = jnp.broadcast_to(t, t_out.shape)


def _sample_threshold(scores2d, ksel):
    n, width = scores2d.shape
    rows = THRESHOLD_ROWS
    return pl.pallas_call(
        functools.partial(_sample_threshold_kernel, ksel=ksel),
        grid=(n // rows,),
        in_specs=[pl.BlockSpec((rows, width), lambda i: (i, 0))],
        out_specs=[pl.BlockSpec((rows, width), lambda i: (i, 0)),
                   pl.BlockSpec((rows, LANES), lambda i: (i, 0))],
        out_shape=[jax.ShapeDtypeStruct((n, width), f32),
                   jax.ShapeDtypeStruct((n, LANES), f32)],
        compiler_params=_params("arbitrary"),
        name="sample_threshold",
    )(scores2d)


def _sample_attn_kernel(pt_ref, q_ref, sc_ref, t_ref, kn_ref, vn_ref, *refs):
    n = PAGES_PER_STEP
    k_pages, v_pages = refs[:n], refs[n:2 * n]
    o_ref, m_s, l_s, acc_s = refs[2 * n:]
    j = pl.program_id(1)
    last = pl.num_programs(1) - 1
    group_rows = 2 * SAMPLE_Q

    @pl.when(j == 0)
    def _():
        m_s[...] = jnp.full(m_s.shape, NEG_BIG, f32)
        l_s[...] = jnp.zeros(l_s.shape, f32)
        acc_s[...] = jnp.zeros(acc_s.shape, f32)

    t = t_ref[:, 0:1]

    def update(kt, vt, sc):
        bias = jnp.where(sc >= t, 0.0, NEG_BIG)
        bias = jnp.concatenate([bias, bias], axis=0)
        lg = jnp.concatenate(
            [jnp.dot(q_ref[h * group_rows:(h + 1) * group_rows, :], kt[h],
                     preferred_element_type=f32) + bias for h in range(N_KV_HEADS)], axis=0)
        m_old = m_s[...]
        m_new = jnp.maximum(m_old, jnp.max(lg, axis=-1, keepdims=True))
        alpha = jnp.exp(m_old - m_new)
        pr = jnp.exp(lg - m_new[:, 0:1])
        l_s[...] = alpha * l_s[...] + jnp.sum(pr, axis=-1, keepdims=True)
        pr = pr.astype(bf16)
        pv = jnp.concatenate(
            [lax.dot_general(pr[h * group_rows:(h + 1) * group_rows], vt[h],
                             (((1,), (1,)), ((), ())), preferred_element_type=f32)
             for h in range(N_KV_HEADS)], axis=0)
        acc_s[...] = acc_s[...] * alpha[:, 0:HEAD_DIM] + pv
        m_s[...] = m_new

    def lanes_of(page_refs):
        return jnp.concatenate([r[...] for r in page_refs], axis=2).astype(bf16)

    @pl.when(j < last)
    def _():
        update(lanes_of(k_pages), lanes_of(v_pages), sc_ref[...])

    @pl.when(j == last)
    def _():
        update(kn_ref[...], vn_ref[...], sc_ref[:, 0:PAGE])
        o = acc_s[...] / l_s[:, 0:HEAD_DIM]
        for h in range(N_HEADS):
            o_ref[:, h * HEAD_DIM:(h + 1) * HEAD_DIM] = o[h * SAMPLE_Q:(h + 1) * SAMPLE_Q]


def _sample_attention(page_table, q_s, scores3d, t3d, k_new_t, v_new_t, k_cache_t, v_cache_t,
                      layer):
    db, n_pages = page_table.shape
    steps = n_pages // PAGES_PER_STEP + 1
    per_b = lambda b, j, pt: (b, 0, 0)
    per_b4 = lambda b, j, pt: (b, 0, 0, 0)
    step_w = PAGES_PER_STEP * PAGE
    page_block = (None, None, N_KV_HEADS, HEAD_DIM, PAGE)
    return pl.pallas_call(
        _sample_attn_kernel,
        grid_spec=pltpu.PrefetchScalarGridSpec(
            num_scalar_prefetch=1,
            grid=(db, steps),
            in_specs=[pl.BlockSpec((None, SAMPLE_ROWS, HEAD_DIM), per_b),
                      pl.BlockSpec((None, SAMPLE_Q, step_w), lambda b, j, pt: (b, 0, j)),
                      pl.BlockSpec((None, SAMPLE_Q, LANES), per_b),
                      pl.BlockSpec((None, N_KV_HEADS, HEAD_DIM, PAGE), per_b4),
                      pl.BlockSpec((None, N_KV_HEADS, HEAD_DIM, PAGE), per_b4)]
                     + [_page_spec(page_block, layer, n, n_pages)
                        for n in range(PAGES_PER_STEP)] * 2,
            out_specs=pl.BlockSpec((None, SAMPLE_Q, ATTN_WIDTH), per_b),
            scratch_shapes=[pltpu.VMEM((SAMPLE_ROWS, LANES), f32),
                            pltpu.VMEM((SAMPLE_ROWS, LANES), f32),
                            pltpu.VMEM((SAMPLE_ROWS, HEAD_DIM), f32)],
        ),
        out_shape=jax.ShapeDtypeStruct((db, SAMPLE_Q, ATTN_WIDTH), f32),
        compiler_params=_params("arbitrary", "arbitrary"),
        name="sample_attention",
    )(page_table, q_s, scores3d, t3d, k_new_t, v_new_t,
      *([k_cache_t] * PAGES_PER_STEP), *([v_cache_t] * PAGES_PER_STEP))


def _pool_diff_prompt(u_ref, hist_ref, seq):
    tm = u_ref.shape[0]
    start = (pl.program_id(0) * tm) % seq
    u = u_ref[...]
    hist = hist_ref[...] * jnp.where(start == 0, 0.0, 1.0)
    ext = jnp.concatenate([hist, u], axis=0)
    pos = start + lax.broadcasted_iota(i32, (tm, 1), 0)
    outs = []
    for g, win in enumerate(POOL_WINDOWS):
        s = ext[:, g * POOL_GROUP:(g + 1) * POOL_GROUP]
        span = 1
        while span < win:
            s = s + pltpu.roll(s, span, 0)
            span *= 2
        cnt = jnp.minimum(pos + 1, win).astype(f32)
        outs.append(s[HIST_ROWS:] / cnt - u[:, g * POOL_GROUP:(g + 1) * POOL_GROUP])
    return outs


def _pool_diff_sample(u_ref):
    nb = u_ref.shape[0]
    outs = []
    for g, win in enumerate(POOL_WINDOWS):
        lanes = slice(g * POOL_GROUP, (g + 1) * POOL_GROUP)
        cur = u_ref[:, HIST_ROWS:HIST_ROWS + 8, lanes]
        s = cur
        for d in range(1, win):
            s = s + u_ref[:, HIST_ROWS - d:HIST_ROWS + 8 - d, lanes]
        outs.append((s / float(win) - cur).reshape(nb * 8, POOL_GROUP))
    return outs


def _merge_kernel(x_ref, ao_ref, u_ref, hist_ref, ga_ref, gb_ref, wpool_ref, pscale_ref,
                  wba_ref, wbp_ref, wout_ref, fg_ref, wr_ref, br_ref,
                  h_ref, hn_ref, route_ref, *, seq):
    diffs = _pool_diff_sample(u_ref) if seq is None else _pool_diff_prompt(u_ref, hist_ref, seq)
    pool = jnp.concatenate(
        [jnp.dot(d.astype(bf16), wpool_ref[g], preferred_element_type=f32)
         for g, d in enumerate(diffs)], axis=1) * pscale_ref[...]
    br_a = jnp.dot(ao_ref[...].astype(bf16), wba_ref[...], preferred_element_type=f32)
    br_p = jnp.dot(pool.astype(bf16), wbp_ref[...], preferred_element_type=f32)
    sig = lambda z: 1.0 / (1.0 + jnp.exp(-z))
    mix = sig(ga_ref[...]) * br_a + sig(gb_ref[...]) * br_p
    h = x_ref[...] + jnp.dot(mix.astype(bf16), wout_ref[...], preferred_element_type=f32)
    h_ref[...] = h
    hn = _rms(h, fg_ref[...])
    hn_ref[...] = hn

    logit = jnp.dot(hn, wr_ref[...], preferred_element_type=f32,
                    precision=lax.Precision.HIGHEST) + br_ref[...]
    lane_i = lax.broadcasted_iota(i32, logit.shape, 1)
    lane = lane_i.astype(f32)
    big = float(LANES)
    gl = jnp.where(lane_i < N_EXPERTS, -jnp.inf,
                   jnp.where(lane_i < N_EXPERTS + N_GROUPS, logit, -jnp.inf))
    gmax = jnp.max(gl, axis=-1, keepdims=True)
    g_sel = jnp.min(jnp.where(gl == gmax, lane, big), axis=-1, keepdims=True) - N_EXPERTS
    g_w = 1.0 / jnp.sum(jnp.exp(gl - gmax), axis=-1, keepdims=True)
    group_of_lane = (lane_i // EXPERTS_PER_GROUP).astype(f32)
    el = jnp.where(group_of_lane == g_sel, logit, -jnp.inf)
    v1 = jnp.max(el, axis=-1, keepdims=True)
    i1 = jnp.min(jnp.where(el == v1, lane, big), axis=-1, keepdims=True)
    el2 = jnp.where(lane == i1, -jnp.inf, el)
    v2 = jnp.max(el2, axis=-1, keepdims=True)
    i2 = jnp.min(jnp.where(el2 == v2, lane, big), axis=-1, keepdims=True)
    e21 = jnp.exp(v2 - v1)
    w1 = g_w / (1.0 + e21)
    w2 = g_w * e21 / (1.0 + e21)
    route_ref[...] = jnp.where(lane_i == 0, i1,
                     jnp.where(lane_i == 1, i2,
                     jnp.where(lane_i == 2, w1, jnp.where(lane_i == 3, w2, 0.0))))


def _merge(x2d, ao, u, ga, gb, lw, seq):
    n = x2d.shape[0]
    tm = PROJ_ROWS
    row = lambda i: (i, 0)
    const2 = lambda i: (0, 0)
    const3 = lambda i: (0, 0, 0)
    if seq is None:
        u_spec = pl.BlockSpec((tm // 8, HIST_ROWS + 8, POOL_WIDTH), lambda i: (i, 0, 0))
        hist = jnp.zeros((HIST_ROWS, POOL_WIDTH), f32)
        hist_spec = pl.BlockSpec((HIST_ROWS, POOL_WIDTH), const2)
    else:
        u_spec = pl.BlockSpec((tm, POOL_WIDTH), row)
        hist = u
        hist_spec = pl.BlockSpec((HIST_ROWS, POOL_WIDTH),
                                 lambda i: (jnp.maximum(i * (tm // HIST_ROWS) - 1, 0), 0))
    return pl.pallas_call(
        functools.partial(_merge_kernel, seq=seq),
        grid=(n // tm,),
        in_specs=[pl.BlockSpec((tm, D_MODEL), row),
                  pl.BlockSpec((tm, ATTN_WIDTH), row),
                  u_spec, hist_spec,
                  pl.BlockSpec((tm, D_MODEL), row),
                  pl.BlockSpec((tm, D_MODEL), row),
                  pl.BlockSpec((4, POOL_GROUP, POOL_GROUP), const3),
                  pl.BlockSpec((1, POOL_WIDTH), const2),
                  pl.BlockSpec((ATTN_WIDTH, D_MODEL), const2),
                  pl.BlockSpec((POOL_WIDTH, D_MODEL), const2),
                  pl.BlockSpec((D_MODEL, D_MODEL), const2),
                  pl.BlockSpec((1, D_MODEL), const2),
                  pl.BlockSpec((D_MODEL, LANES), const2),
                  pl.BlockSpec((1, LANES), const2)],
        out_specs=[pl.BlockSpec((tm, D_MODEL), row),
                   pl.BlockSpec((tm, D_MODEL), row),
                   pl.BlockSpec((tm, LANES), row)],
        out_shape=[jax.ShapeDtypeStruct((n, D_MODEL), f32),
                   jax.ShapeDtypeStruct((n, D_MODEL), f32),
                   jax.ShapeDtypeStruct((n, LANES), f32)],
        compiler_params=_params("arbitrary"),
        name="merge",
    )(x2d, ao, u, hist, ga, gb, lw['w_pool'], lw['pool_scale'], lw['w_ba'], lw['w_bp'],
      lw['w_out'], lw['ffn_g'], lw['w_router'], lw['b_router'])


RANK_ROWS = 512
MOE_BLOCK_PROMPT = 256
MOE_BLOCK_SAMPLE = 128
GATHER_UNROLL = 8


def _rank_kernel(route_ref, rank_ref, counts_ref, base_s):
    @pl.when(pl.program_id(0) == 0)
    def _():
        base_s[...] = jnp.zeros(base_s.shape, f32)
    r = route_ref[...]
    tt = r.shape[0]
    lane = lax.broadcasted_iota(i32, r.shape, 1).astype(f32)
    oh1 = lane == r[:, 0:1]
    oh2 = lane == r[:, 1:2]
    oh = jnp.where(oh1, 1.0, 0.0) + jnp.where(oh2, 1.0, 0.0)
    below = (lax.broadcasted_iota(i32, (tt, tt), 0) > lax.broadcasted_iota(i32, (tt, tt), 1))
    before = jnp.dot(jnp.where(below, 1.0, 0.0).astype(bf16), oh.astype(bf16),
                     preferred_element_type=f32) + base_s[...]
    r1 = jnp.sum(jnp.where(oh1, before, 0.0), axis=-1, keepdims=True)
    r2 = jnp.sum(jnp.where(oh2, before, 0.0), axis=-1, keepdims=True)
    lane_i = lax.broadcasted_iota(i32, r.shape, 1)
    rank_ref[...] = jnp.where(lane_i == 0, r1, jnp.where(lane_i == 1, r2, 0.0))
    base_s[...] = base_s[...] + jnp.sum(oh, axis=0, keepdims=True)
    counts_ref[...] = base_s[...]


def _expert_ranks(route):
    n = route.shape[0]
    tt = min(RANK_ROWS, n)
    return pl.pallas_call(
        _rank_kernel,
        grid=(n // tt,),
        in_specs=[pl.BlockSpec((tt, LANES), lambda i: (i, 0))],
        out_specs=[pl.BlockSpec((tt, LANES), lambda i: (i, 0)),
                   pl.BlockSpec((1, LANES), lambda i: (0, 0))],
        out_shape=[jax.ShapeDtypeStruct((n, LANES), f32),
                   jax.ShapeDtypeStruct((1, LANES), f32)],
        scratch_shapes=[pltpu.VMEM((1, LANES), f32)],
        compiler_params=_params("arbitrary"),
        name="expert_ranks",
    )(route)


INVERT_ROWS = 8192


def _slot_tokens_kernel(dest_ref, zeros_hbm, tok_ref, sem):
    i = pl.program_id(0)
    tile = dest_ref.shape[0]

    @pl.when(i == 0)
    def _():
        fill = pltpu.make_async_copy(zeros_hbm, tok_ref, sem)
        fill.start()
        fill.wait()

    def body(a, carry):
        tok_ref[dest_ref[a]] = (i * tile + a) // TOP_K
        return carry
    lax.fori_loop(0, tile, body, 0, unroll=16)


def _slot_tokens(dest, n_slots):
    n = dest.shape[0]
    tile = min(INVERT_ROWS, n)
    return pl.pallas_call(
        _slot_tokens_kernel,
        grid=(n // tile,),
        in_specs=[pl.BlockSpec((tile,), lambda i: (i,), memory_space=pltpu.SMEM),
                  pl.BlockSpec(memory_space=pl.ANY)],
        out_specs=pl.BlockSpec((n_slots,), lambda i: (0,), memory_space=pltpu.SMEM),
        out_shape=jax.ShapeDtypeStruct((n_slots,), i32),
        scratch_shapes=[pltpu.SemaphoreType.DMA(())],
        compiler_params=_params("arbitrary"),
        name="moe_slot_tokens",
    )(dest, jnp.zeros((n_slots,), i32))


def _expert_ffn_kernel(be_ref, nact_ref, tok_ref, tok_next_ref, x_hbm, wg_ref, wu_ref, wd_ref,
                       ys_ref, xbuf, sem):
    i = pl.program_id(0)
    n_active = nact_ref[0]
    bm = ys_ref.shape[0]
    slot = i % 2

    def gather(tok, s):
        def body(g, carry):
            for j in range(GATHER_UNROLL):
                r = g * GATHER_UNROLL + j
                pltpu.make_async_copy(x_hbm.at[pl.ds(tok[0, r], 1)], xbuf.at[s, pl.ds(r, 1)],
                                      sem.at[s]).start(priority=j % 2)
            return carry
        lax.fori_loop(0, bm // GATHER_UNROLL, body, 0)

    @pl.when(i == 0)
    def _():
        gather(tok_ref, 0)

    @pl.when(i + 1 < n_active)
    def _():
        gather(tok_next_ref, 1 - slot)

    @pl.when(i < n_active)
    def _():
        pltpu.make_async_copy(x_hbm.at[pl.ds(0, bm)], xbuf.at[slot], sem.at[slot]).wait()
        x = xbuf[slot].astype(bf16)
        g = jnp.dot(x, wg_ref[...], preferred_element_type=f32)
        u = jnp.dot(x, wu_ref[...], preferred_element_type=f32)
        hb = (g / (1.0 + jnp.exp(-g))) * u
        ys_ref[...] = jnp.dot(hb.astype(bf16), wd_ref[...], preferred_element_type=f32)

    @pl.when(i >= n_active)
    def _():
        ys_ref[...] = jnp.zeros(ys_ref.shape, f32)


def _expert_ffn(block_expert, n_active, slot_tok, x, wg, wu, wd, bm):
    n_blocks = slot_tok.shape[0] // bm
    tok3 = slot_tok.reshape(n_blocks, 1, bm)
    by_expert = lambda i, be, na: (be[i], 0, 0)
    smem_block = lambda index: pl.BlockSpec((None, 1, bm), index, memory_space=pltpu.SMEM)
    return pl.pallas_call(
        _expert_ffn_kernel,
        grid_spec=pltpu.PrefetchScalarGridSpec(
            num_scalar_prefetch=2,
            grid=(n_blocks,),
            in_specs=[smem_block(lambda i, be, na: (i, 0, 0)),
                      smem_block(lambda i, be, na: (jnp.minimum(i + 1, n_blocks - 1), 0, 0)),
                      pl.BlockSpec(memory_space=pl.ANY),
                      pl.BlockSpec((None, D_MODEL, D_EXPERT), by_expert),
                      pl.BlockSpec((None, D_MODEL, D_EXPERT), by_expert),
                      pl.BlockSpec((None, D_EXPERT, D_MODEL), by_expert)],
            out_specs=pl.BlockSpec((bm, D_MODEL), lambda i, be, na: (i, 0)),
            scratch_shapes=[pltpu.VMEM((2, bm, D_MODEL), f32),
                            pltpu.SemaphoreType.DMA((2,))],
        ),
        out_shape=jax.ShapeDtypeStruct((n_blocks * bm, D_MODEL), f32),
        compiler_params=_params("arbitrary"),
        name="expert_ffn",
    )(block_expert, n_active, tok3, tok3, x, wg, wu, wd)


COMBINE_ROWS = 128


def _combine_kernel(dest_ref, dest_next_ref, h_ref, route_ref, g_ref, ys_hbm, y_ref, buf, sem):
    i = pl.program_id(0)
    tt = h_ref.shape[0]
    slot = i % 2

    def gather(dest, s):
        def body(g, carry):
            for k in range(GATHER_UNROLL // TOP_K):
                t = g * (GATHER_UNROLL // TOP_K) + k
                for j in range(TOP_K):
                    pltpu.make_async_copy(ys_hbm.at[pl.ds(dest[0, TOP_K * t + j], 1)],
                                          buf.at[s, j, pl.ds(t, 1)], sem.at[s]).start(priority=j)
            return carry
        lax.fori_loop(0, tt * TOP_K // GATHER_UNROLL, body, 0)

    @pl.when(i == 0)
    def _():
        gather(dest_ref, 0)

    @pl.when(i + 1 < pl.num_programs(0))
    def _():
        gather(dest_next_ref, 1 - slot)

    for j in range(TOP_K):
        pltpu.make_async_copy(ys_hbm.at[pl.ds(0, tt)], buf.at[slot, j], sem.at[slot]).wait()
    r = route_ref[...]
    moe = buf[slot, 0] * r[:, 2:3] + buf[slot, 1] * r[:, 3:4]
    y_ref[...] = _rms(h_ref[...] + moe, g_ref[...])


def _combine(dest, h, route, final_g, ys):
    n = h.shape[0]
    tt = COMBINE_ROWS
    n_tiles = n // tt
    dest3 = dest.reshape(n_tiles, 1, tt * TOP_K)
    row = lambda i: (i, 0)
    smem_block = lambda index: pl.BlockSpec((None, 1, tt * TOP_K), index, memory_space=pltpu.SMEM)
    return pl.pallas_call(
        _combine_kernel,
        grid=(n_tiles,),
        in_specs=[smem_block(lambda i: (i, 0, 0)),
                  smem_block(lambda i: (jnp.minimum(i + 1, n_tiles - 1), 0, 0)),
                  pl.BlockSpec((tt, D_MODEL), row),
                  pl.BlockSpec((tt, LANES), row),
                  pl.BlockSpec((1, D_MODEL), lambda i: (0, 0)),
                  pl.BlockSpec(memory_space=pl.ANY)],
        out_specs=pl.BlockSpec((tt, D_MODEL), row),
        out_shape=jax.ShapeDtypeStruct((n, D_MODEL), f32),
        scratch_shapes=[pltpu.VMEM((2, TOP_K, tt, D_MODEL), f32),
                        pltpu.SemaphoreType.DMA((2,))],
        compiler_params=_params("arbitrary"),
        name="moe_combine",
    )(dest3, dest3, h, route, final_g, ys)


def _moe_and_final_norm(h, hn, route, lw, final_g, bm):
    n = h.shape[0]
    rank, counts = _expert_ranks(route)
    counts = counts[0, :N_EXPERTS].astype(i32)
    padded = (counts + bm - 1) // bm * bm
    pend = jnp.cumsum(padded)
    pstart = pend - padded
    e_id = route[:, 0:TOP_K].astype(i32)
    dest = (pstart[e_id] + rank[:, 0:TOP_K].astype(i32)).reshape(-1)
    n_blocks = (n * TOP_K + N_EXPERTS * (bm - 1)) // bm + 1
    n_active = (pend[-1] // bm).astype(i32)
    blk = jnp.minimum(jnp.arange(n_blocks, dtype=i32), n_active - 1) * bm
    block_expert = jnp.minimum(jnp.sum((pend[None, :] <= blk[:, None]).astype(i32), axis=1),
                               N_EXPERTS - 1)
    slot_tok = jnp.zeros((n_blocks * bm,), i32).at[dest].set(
        jnp.arange(n * TOP_K, dtype=i32) // TOP_K, unique_indices=True)
    ys = _expert_ffn(block_expert, n_active.reshape(1), slot_tok, hn,
                     lw['w_eg'], lw['w_eu'], lw['w_ed'], bm)
    return _combine(dest, h, route, final_g, ys)


def _rope_tables(pos, reps):
    half = HEAD_DIM // 2
    inv_freq = 1.0 / (ROPE_THETA ** (jnp.arange(half, dtype=f32) / half))
    ang = pos.astype(f32)[:, None] * inv_freq[None, :]
    cos, sin = jnp.cos(ang), jnp.sin(ang)
    cos = jnp.concatenate([cos, cos, cos, cos], axis=1)
    sin = jnp.concatenate([-sin, sin, -sin, sin], axis=1)
    return jnp.tile(cos, (reps, 1)), jnp.tile(sin, (reps, 1))


def _layer_weights(l, attn_norm_g, w_in, w_pool, pool_scale, w_branch_attn, w_branch_pool, w_out,
                   ffn_norm_g, w_rg, b_rg, w_re, b_re, w_eg, w_eu, w_ed):
    cuts = [0]
    for s in SEGMENTS:
        cuts.append(cuts[-1] + s)
    w = w_in[l]
    kw_pad = LANES - IDX_DIM - N_IDX_HEADS
    w_packed = jnp.concatenate(
        [w[:, cuts[0]:cuts[4]], w[:, cuts[4]:cuts[6]], jnp.zeros((D_MODEL, kw_pad), f32),
         w[:, cuts[6]:]], axis=1).astype(bf16)
    r_pad = LANES - N_EXPERTS - N_GROUPS
    return dict(
        attn_g=attn_norm_g[l].reshape(1, D_MODEL), w_packed=w_packed,
        w_pool=w_pool[l].astype(bf16), pool_scale=pool_scale[l].reshape(1, POOL_WIDTH),
        w_ba=w_branch_attn[l].astype(bf16), w_bp=w_branch_pool[l].astype(bf16),
        w_out=w_out[l].astype(bf16), ffn_g=ffn_norm_g[l].reshape(1, D_MODEL),
        w_router=jnp.concatenate([w_re[l], w_rg[l], jnp.zeros((D_MODEL, r_pad), f32)], axis=1),
        b_router=jnp.concatenate([b_re[l], b_rg[l], jnp.zeros((r_pad,), f32)]).reshape(1, LANES),
        w_eg=w_eg[l].astype(bf16), w_eu=w_eu[l].astype(bf16), w_ed=w_ed[l].astype(bf16))


def _prompt_layer(x, lw, final_g):
    batch, seq, _ = x.shape
    x2d = x.reshape(batch * seq, D_MODEL)
    cos, sin = _rope_tables(jnp.arange(seq), 1)
    q, k, v, qi, kw, u, ga, gb = _project(x2d, lw['attn_g'], lw['w_packed'], cos, sin)
    ao = _prompt_attention(qi, q, kw, k, v, batch, seq, min(TOPK_MAX, seq // 4))
    h, hn, route = _merge(x2d, ao, u, ga, gb, lw, seq)
    y = _moe_and_final_norm(h, hn, route, lw, final_g, bm=MOE_BLOCK_PROMPT)
    return (y.reshape(batch, seq, D_MODEL),
            k.reshape(batch, seq, N_KV_HEADS, HEAD_DIM), v.reshape(batch, seq, N_KV_HEADS, HEAD_DIM),
            kw[:, :IDX_DIM].reshape(batch, seq, IDX_DIM),
            u.reshape(batch, seq, POOL_WIDTH)[:, seq - POOL_HIST:])


def _sample_layer(x, l, cache_k, cache_v, cache_idx_k, state_pool, page_table, lw, final_g):
    db, ds, _ = x.shape
    page = cache_k.shape[2]
    past = page_table.shape[1] * page
    x2d = x.reshape(db * ds, D_MODEL)
    cos, sin = _rope_tables(past + jnp.arange(ds), PROJ_ROWS // ds)
    q, k, v, qi, kw, u, ga, gb = _project(x2d, lw['attn_g'], lw['w_packed'], cos, sin)

    rows_hq = lambda a, nh, d: a.reshape(db, ds, nh, d).transpose(0, 2, 1, 3).reshape(db, nh * ds, d)
    qi_s = rows_hq(qi, N_IDX_HEADS, IDX_DIM)
    w_s = kw[:, IDX_DIM:IDX_DIM + N_IDX_HEADS].reshape(db, ds, N_IDX_HEADS)
    w_s = w_s.transpose(0, 2, 1).reshape(db, N_IDX_HEADS * ds, 1)
    pad_slots = lambda a: jnp.pad(a, [(0, 0)] * (a.ndim - 1) + [(0, page - ds)])
    ki_new_t = pad_slots(kw[:, :IDX_DIM].reshape(db, ds, IDX_DIM).transpose(0, 2, 1))
    scores = _sample_scores(page_table, qi_s, w_s, ki_new_t,
                            cache_idx_k.transpose(0, 1, 3, 2), l)
    width = scores.shape[-1]
    scores, t = _sample_threshold(scores.reshape(db * ds, width),
                                  min(TOPK_MAX, (past + ds) // 4))

    as_page = lambda a: pad_slots(
        a.reshape(db, ds, N_KV_HEADS, HEAD_DIM).transpose(0, 2, 3, 1)).astype(bf16)
    ao = _sample_attention(page_table, rows_hq(q, N_HEADS, HEAD_DIM),
                           scores.reshape(db, ds, width), t.reshape(db, ds, LANES),
                           as_page(k), as_page(v),
                           cache_k.transpose(0, 1, 3, 4, 2), cache_v.transpose(0, 1, 3, 4, 2), l)

    u3 = u.reshape(db, ds, POOL_WIDTH)
    u_all = jnp.concatenate([jnp.zeros((db, 1, POOL_WIDTH), f32), state_pool[l], u3], axis=1)
    h, hn, route = _merge(x2d, ao.reshape(db * ds, ATTN_WIDTH), u_all, ga, gb, lw, None)
    y = _moe_and_final_norm(h, hn, route, lw, final_g, bm=MOE_BLOCK_SAMPLE)
    return (y.reshape(db, ds, D_MODEL),
            k.reshape(db, ds, N_KV_HEADS, HEAD_DIM), v.reshape(db, ds, N_KV_HEADS, HEAD_DIM),
            kw[:, :IDX_DIM].reshape(db, ds, IDX_DIM),
            u_all[:, -POOL_HIST:])


def kernel(x_prompt, x_sample, cache_k, cache_v, cache_idx_k, state_pool, page_table, attn_norm_g, w_in, w_pool, pool_scale, w_branch_attn, w_branch_pool, w_out, ffn_norm_g, w_router_group, b_router_group, w_router_expert, b_router_expert, w_exp_gate, w_exp_up, w_exp_down, final_norm_g):
    depth = w_in.shape[0]
    assert depth == 1, "the final norm is fused into the (single) layer's MoE combine"
    final_g = final_norm_g.reshape(1, D_MODEL)
    lw = _layer_weights(0, attn_norm_g, w_in, w_pool, pool_scale, w_branch_attn, w_branch_pool,
                        w_out, ffn_norm_g, w_router_group, b_router_group, w_router_expert,
                        b_router_expert, w_exp_gate, w_exp_up, w_exp_down)
    yp, kp, vp, kip, up = _prompt_layer(x_prompt, lw, final_g)
    ys, ks, vs, kis, us = _sample_layer(x_sample, 0, cache_k, cache_v, cache_idx_k, state_pool,
                                        page_table, lw, final_g)
    stack = lambda a: a[None]
    return (yp, ys, stack(kp), stack(vp), stack(kip), stack(up),
            stack(ks), stack(vs), stack(kis), stack(us))
```
